```python
import jax, jax.numpy as jnp
from jax import lax
import numpy as np

D_MODEL = 1024
BATCH = 1
SEQ = 16384
DEPTH = 2

CHUNK = 64
N_EVEN = (DEPTH + 1) // 2
N_ODD = DEPTH // 2
HG_HEADS = 4
HG_DK = 128
HG_DV = 128
RET_HEADS = 4
RET_DK = 128
RET_DV = 128
HG_WIDTH = HG_HEADS * HG_DK
RET_WIDTH = RET_HEADS * RET_DK
MIX_WIDTH = HG_HEADS * HG_DV + RET_HEADS * RET_DV
IN_WIDTH = 4 * HG_WIDTH + 4 * RET_WIDTH
RET_GAMMA_EXP0 = 5.0
ROPE_BASE = 10000.0
POOL_WINDOWS = (2, 4, 8, 16)
N_POOL = 4
POOL_GROUP = D_MODEL // N_POOL
N_EXPERTS = 32
TOP_K = 4
D_EXPERT = D_MODEL
SWIGLU_LIMIT = 7.0
SWIGLU_ALPHA = 1.702
MOE_BLOCK = 256
EPS = 1e-6

kernel_name = "hybrid_hgrn2_retention_pool_moe_adaln"


def rmsnorm(x):
    xf = x.astype(jnp.float32)
    return (xf * lax.rsqrt(jnp.mean(xf * xf, axis=-1, keepdims=True) + EPS)).astype(x.dtype)


def head_rmsnorm(o, gain):
    o = o * lax.rsqrt(jnp.mean(o * o, axis=-1, keepdims=True) + EPS)
    return o * gain.astype(jnp.float32)[None, :, None, None, :]


def to_chunks(a, heads):
    B, S, W = a.shape
    return a.reshape(B, S // CHUNK, CHUNK, heads, W // heads).transpose(0, 3, 1, 2, 4)


def from_chunks(a):
    B, H, N, C, d = a.shape
    return a.transpose(0, 2, 3, 1, 4).reshape(B, N * C, H * d)


def rotary(x):
    S, d = x.shape[1], x.shape[-1]
    half = d // 2
    inv = ROPE_BASE ** (-jnp.arange(half, dtype=jnp.float32) / half)
    ang = jnp.arange(S, dtype=jnp.float32)[:, None] * inv[None, :]
    cos = jnp.cos(ang)[None, :, None, :]
    sin = jnp.sin(ang)[None, :, None, :]
    x1, x2 = x[..., :half], x[..., half:]
    return jnp.concatenate([x1 * cos - x2 * sin, x1 * sin + x2 * cos], axis=-1)


def hgrn2_chunked(q, k, v, g):
    B, H, N, C, dk = q.shape
    dv = v.shape[-1]
    mask = jnp.tril(jnp.ones((C, C), dtype=bool))[:, :, None]

    def step(S, inp):
        qc, kc, vc, gc = inp
        b = jnp.cumsum(gc, axis=2)
        o_inter = jnp.einsum('bhtd,bhde->bhte', qc * jnp.exp(b), S)
        diff = b[:, :, :, None, :] - b[:, :, None, :, :]
        decay = jnp.where(mask, jnp.exp(jnp.where(mask, diff, 0.0)), 0.0)
        attn = jnp.einsum('bhtd,bhsd,bhtsd->bhts', qc, kc, decay)
        o_intra = jnp.einsum('bhts,bhse->bhte', attn, vc)
        b_last = b[:, :, -1:, :]
        S_new = jnp.exp(b_last[:, :, 0, :])[..., None] * S + jnp.einsum(
            'bhsd,bhse->bhde', kc * jnp.exp(b_last - b), vc)
        return S_new, o_inter + o_intra

    xs = tuple(jnp.moveaxis(a, 2, 0) for a in (q, k, v, g))
    S0 = jnp.zeros((B, H, dk, dv), jnp.float32)
    _, o = lax.scan(step, S0, xs)
    return jnp.moveaxis(o, 0, 2)


def retention_chunked(q, k, v):
    B, H, N, C, dk = q.shape
    dv = v.shape[-1]
    lg = jnp.log(1.0 - 2.0 ** (-RET_GAMMA_EXP0 - jnp.arange(H, dtype=jnp.float32)))
    idx = jnp.arange(C, dtype=jnp.float32)
    rel = idx[:, None] - idx[None, :]
    causal = rel >= 0
    dmat = jnp.where(causal, jnp.exp(jnp.where(causal, rel, 0.0)[None] * lg[:, None, None]), 0.0)
    scores = jnp.einsum('bhnqd,bhnkd->bhnqk', q, k) * dmat[None, :, None]
    o_intra = jnp.einsum('bhnqk,bhnke->bhnqe', scores, v)
    k_dec = k * jnp.exp((C - 1 - idx)[None, :] * lg[:, None])[None, :, None, :, None]
    kv = jnp.einsum('bhnsd,bhnse->nbhde', k_dec, v)
    chunk_decay = jnp.exp(C * lg)[None, :, None, None]

    def step(R, kv_n):
        return chunk_decay * R + kv_n, R

    _, R_prev = lax.scan(step, jnp.zeros((B, H, dk, dv), jnp.float32), kv)
    q_dec = q * jnp.exp((idx + 1.0)[None, :] * lg[:, None])[None, :, None, :, None]
    o_inter = jnp.einsum('bhnqd,nbhde->bhnqe', q_dec, R_prev)
    return o_intra + o_inter


def hgrn2_retention_mixer(h, w_in, lower_bound, hg_gain, ret_gain, w_out):
    B, S, _ = h.shape
    proj = (h @ w_in).astype(jnp.float32)
    sizes = [HG_WIDTH] * 4 + [RET_WIDTH] * 4
    split_at = np.cumsum(sizes)[:-1].tolist()
    hq, hf, hi, hog, rq, rk, rv, rg = jnp.split(proj, split_at, axis=-1)
    lb = lower_bound.astype(jnp.float32)
    fgate = lb + (1.0 - lb) * jax.nn.sigmoid(hf)
    o_hg = hgrn2_chunked(to_chunks(jax.nn.silu(hq), HG_HEADS),
                         to_chunks(1.0 - fgate, HG_HEADS),
                         to_chunks(hi, HG_HEADS),
                         to_chunks(jnp.log(fgate), HG_HEADS))
    o_hg = from_chunks(head_rmsnorm(o_hg, hg_gain)) * jax.nn.silu(hog)
    q = rotary(rq.reshape(B, S, RET_HEADS, RET_DK)).reshape(B, S, RET_WIDTH)
    k = (rotary(rk.reshape(B, S, RET_HEADS, RET_DK)) * RET_DK ** -0.5).reshape(B, S, RET_WIDTH)
    o_ret = retention_chunked(to_chunks(q, RET_HEADS), to_chunks(k, RET_HEADS),
                              to_chunks(rv, RET_HEADS))
    o_ret = from_chunks(head_rmsnorm(o_ret, ret_gain)) * jax.nn.silu(rg)
    cat = jnp.concatenate([o_hg, o_ret], axis=-1).astype(h.dtype)
    return cat @ w_out


def pool_mixer(h, pool_w, pool_b, pool_scale):
    B, S, D = h.shape
    hf = h.astype(jnp.float32).reshape(B, S, N_POOL, POOL_GROUP)
    csp = jnp.concatenate([jnp.zeros((B, 1, N_POOL, POOL_GROUP), jnp.float32),
                           jnp.cumsum(hf, axis=1)], axis=1)
    t = jnp.arange(S)
    outs = []
    for gi, w in enumerate(POOL_WINDOWS):
        cg = csp[:, :, gi]
        lo = jnp.maximum(t + 1 - w, 0)
        win_sum = cg[:, 1:] - cg[:, lo]
        cnt = jnp.minimum(t + 1, w).astype(jnp.float32)
        outs.append(win_sum / cnt[None, :, None] - hf[:, :, gi])
    p = jnp.stack(outs, axis=2).astype(h.dtype)
    y = jnp.einsum('bsgc,gcd->bsgd', p, pool_w) + pool_b
    return (y.reshape(B, S, D) * pool_scale).astype(h.dtype)


def moe_ffn(h, w_router, b_router, w_gu, b_gu, w_down, b_down):
    B, S, D = h.shape
    T = B * S
    hf = h.reshape(T, D)
    logits = (hf @ w_router + b_router).astype(jnp.float32)
    top_val, top_idx = lax.top_k(logits, TOP_K)
    gates = jax.nn.softmax(top_val, axis=-1)
    flat_e = top_idx.reshape(-1).astype(jnp.int32)
    flat_tok = jnp.arange(T * TOP_K, dtype=jnp.int32) // TOP_K
    flat_w = gates.reshape(-1)
    order = jnp.argsort(flat_e)
    sorted_e = flat_e[order]
    counts = jnp.bincount(flat_e, length=N_EXPERTS).astype(jnp.int32)
    padded = (counts + MOE_BLOCK - 1) // MOE_BLOCK * MOE_BLOCK
    pad_end = jnp.cumsum(padded)
    pad_start = pad_end - padded
    grp_start = jnp.cumsum(counts) - counts
    pos = jnp.arange(T * TOP_K, dtype=jnp.int32)
    dest = pad_start[sorted_e] + pos - grp_start[sorted_e]
    n_rows = T * TOP_K + N_EXPERTS * MOE_BLOCK
    n_blocks = n_rows // MOE_BLOCK
    row_tok = jnp.zeros((n_rows,), jnp.int32).at[dest].set(flat_tok[order])
    row_w = jnp.zeros((n_rows,), jnp.float32).at[dest].set(flat_w[order])
    blk_start = jnp.arange(n_blocks, dtype=jnp.int32) * MOE_BLOCK
    blk_e = jnp.minimum(jnp.searchsorted(pad_end, blk_start, side='right'), N_EXPERTS - 1)
    xs = hf[row_tok].reshape(n_blocks, MOE_BLOCK, D)

    def expert_block(args):
        xb, e = args
        gu = xb @ w_gu[e] + b_gu[e]
        x_glu = jnp.minimum(gu[:, :D_EXPERT], SWIGLU_LIMIT)
        x_lin = jnp.clip(gu[:, D_EXPERT:], -SWIGLU_LIMIT, SWIGLU_LIMIT)
        act = x_glu * jax.nn.sigmoid(SWIGLU_ALPHA * x_glu) * (x_lin + 1.0)
        return act @ w_down[e] + b_down[e]

    ys = lax.map(expert_block, (xs, blk_e)).reshape(n_rows, D)
    out = jax.ops.segment_sum(ys * row_w[:, None], row_tok, num_segments=T)
    return out.reshape(B, S, D).astype(h.dtype)


def setup_inputs(seed: int = 0) -> dict:
    key = jax.random.key(seed)
    ks = jax.random.split(key, 19)

    def nrm(k, shape, scale):
        return jax.random.normal(k, shape, jnp.float32) * scale

    return {
        "x": nrm(ks[0], (BATCH, SEQ, D_MODEL), 1.0),
        "c": nrm(ks[1], (BATCH, D_MODEL), 1.0),
        "w_ada": nrm(ks[2], (DEPTH, D_MODEL, 6 * D_MODEL), 0.5 * D_MODEL ** -0.5),
        "b_ada": nrm(ks[3], (DEPTH, 6 * D_MODEL), 0.02),
        "w_in": nrm(ks[4], (N_EVEN, D_MODEL, IN_WIDTH), D_MODEL ** -0.5),
        "hg_lower_bounds": nrm(ks[5], (N_EVEN + 1, HG_WIDTH), 0.5),
        "hg_norm": 1.0 + nrm(ks[6], (N_EVEN, HG_HEADS, HG_DV), 0.02),
        "ret_norm": 1.0 + nrm(ks[7], (N_EVEN, RET_HEADS, RET_DV), 0.02),
        "w_out": nrm(ks[8], (N_EVEN, MIX_WIDTH, D_MODEL), MIX_WIDTH ** -0.5),
        "pool_w": nrm(ks[9], (N_ODD, N_POOL, POOL_GROUP, POOL_GROUP), POOL_GROUP ** -0.5),
        "pool_b": nrm(ks[10], (N_ODD, N_POOL, POOL_GROUP), 0.02),
        "pool_scale": 1.0 + nrm(ks[11], (N_ODD, D_MODEL), 0.1),
        "w_router": nrm(ks[12], (DEPTH, D_MODEL, N_EXPERTS), D_MODEL ** -0.5),
        "b_router": nrm(ks[13], (DEPTH, N_EXPERTS), 0.01),
        "w_gu": nrm(ks[14], (DEPTH, N_EXPERTS, D_MODEL, 2 * D_EXPERT), D_MODEL ** -0.5),
        "b_gu": nrm(ks[15], (DEPTH, N_EXPERTS, 2 * D_EXPERT), 0.02),
        "w_down": nrm(ks[16], (DEPTH, N_EXPERTS, D_EXPERT, D_MODEL), D_EXPERT ** -0.5),
        "b_down": nrm(ks[17], (DEPTH, N_EXPERTS, D_MODEL), 0.02),
        "final_norm": 1.0 + nrm(ks[18], (D_MODEL,), 0.02),
    }


def reference(x, c, w_ada, b_ada, w_in, hg_lower_bounds, hg_norm, ret_norm, w_out,
              pool_w, pool_b, pool_scale, w_router, b_router, w_gu, b_gu, w_down,
              b_down, final_norm):
    cond = jax.nn.silu(c.astype(jnp.float32)).astype(x.dtype)
    lb_all = jnp.cumsum(jax.nn.softmax(hg_lower_bounds.astype(jnp.float32), axis=0), axis=0)
    for layer in range(DEPTH):
        mod = (cond @ w_ada[layer] + b_ada[layer])[:, None, :]
        sh_m, sc_m, g_m, sh_f, sc_f, g_f = jnp.split(mod, 6, axis=-1)
        hn = rmsnorm(x) * (1.0 + sc_m) + sh_m
        j = layer // 2
        if layer % 2 == 0:
            mix = hgrn2_retention_mixer(hn, w_in[j], lb_all[j], hg_norm[j], ret_norm[j], w_out[j])
        else:
            mix = pool_mixer(hn, pool_w[j], pool_b[j], pool_scale[j])
        x = x + g_m * mix
        hn = rmsnorm(x) * (1.0 + sc_f) + sh_f
        x = x + g_f * moe_ffn(hn, w_router[layer], b_router[layer], w_gu[layer],
                              b_gu[layer], w_down[layer], b_down[layer])
    return rmsnorm(x) * final_norm
```

```python
import functools
import math

import jax
import jax.numpy as jnp
from jax import lax
from jax.experimental import pallas as pl
from jax.experimental.pallas import tpu as pltpu

_D = 1024
_E = 32
_K = 4
_F = 1024
_HEADS = 4
_DH = 128
_CH = 64
_SUB = 16
_EPS = 1e-6
_BLK = 256
_TB = 256
_TBR = 512
_TBD = 512
_TBC = 256
_NEG = -1e30
_SWIGLU_LIMIT = 7.0
_SWIGLU_ALPHA = 1.702
_POOL_WINDOWS = (2, 4, 8, 16)
_HALO = 16
_VMEM_LIMIT = 56 * 1024 * 1024

_f32 = jnp.float32
_bf16 = jnp.bfloat16


def _dot(a, b):
    return jnp.dot(a, b, preferred_element_type=_f32)


def _dot_nt(a, b):
    return lax.dot_general(a, b, (((1,), (1,)), ((), ())), preferred_element_type=_f32)


def _split3(a):
    a0 = a.astype(_bf16)
    r1 = a - a0.astype(_f32)
    a1 = r1.astype(_bf16)
    a2 = (r1 - a1.astype(_f32)).astype(_bf16)
    return a0, a1, a2


def _dot_exact_lhs(m, a):
    a0, a1, a2 = _split3(a)
    return _dot(m, a0) + _dot(m, a1) + _dot(m, a2)


def _dot_split(a, w):
    a0, a1, _ = _split3(a)
    w0, w1, _ = _split3(w)
    return _dot(a0, w0) + _dot(a1, w0) + _dot(a0, w1)


def _rms(x):
    return x * lax.rsqrt(jnp.mean(x * x, axis=-1, keepdims=True) + _EPS)


def _silu(x):
    return x * jax.nn.sigmoid(x)


def _ada_kernel(c_ref, w_ref, b_ref, o_ref):
    c = c_ref[...]
    cond = _silu(c)
    o_ref[0] = jnp.sum(w_ref[0] * cond, axis=0, keepdims=True) + b_ref[0]


def _ada(c, w_ada, b_ada):
    depth = w_ada.shape[0]
    n = depth * 6
    out = pl.pallas_call(
        _ada_kernel,
        grid=(n,),
        in_specs=[
            pl.BlockSpec((_D, 1), lambda j: (0, 0)),
            pl.BlockSpec((1, _D, _D), lambda j: (j // 6, 0, j % 6)),
            pl.BlockSpec((1, 1, _D), lambda j: (j, 0, 0)),
        ],
        out_specs=pl.BlockSpec((1, 1, _D), lambda j: (j, 0, 0)),
        out_shape=jax.ShapeDtypeStruct((n, 1, _D), _f32),
        name="ada_mod",
    )(c.reshape(_D, 1), w_ada, b_ada.reshape(n, 1, _D))
    return out.reshape(depth, 6, _D)


def _ffn_prologue(x1, mod, wr_ref, br_ref, x1_ref, hn2_ref, lg_ref):
    sh_f, sc_f = mod[3:4, :], mod[4:5, :]
    hn2 = _rms(x1) * (1.0 + sc_f) + sh_f
    x1_ref[...] = x1
    hn2_ref[...] = hn2
    lg_ref[...] = _dot_split(hn2, wr_ref[...]) + br_ref[...]


def _mixer0_kernel(x_ref, mod_ref, win_ref, lb_ref, hgg_ref, rtg_ref, wout_ref, cos_ref, sin_ref,
                   wr_ref, br_ref, x1_ref, hn2_ref, lg_ref,
                   q_s, k_s, v_s, b_s, o_s, st_hg, st_rt):
    tb = x_ref.shape[0]
    nch = tb // _CH
    nsub = _CH // _SUB

    @pl.when(pl.program_id(0) == 0)
    def _():
        st_hg[...] = jnp.zeros_like(st_hg)
        st_rt[...] = jnp.zeros_like(st_rt)

    x = x_ref[...]
    mod = mod_ref[...]
    sh_m, sc_m, g_m = mod[0:1, :], mod[1:2, :], mod[2:3, :]
    hb = (_rms(x) * (1.0 + sc_m) + sh_m).astype(_bf16)
    width = _HEADS * _DH

    def sec(j):
        return _dot(hb, win_ref[:, j * width:(j + 1) * width])

    hq = sec(0)
    lb = lb_ref[...]
    fg = lb + (1.0 - lb) * jax.nn.sigmoid(sec(1))
    hi = sec(2)
    g = jnp.log(fg)
    r_i = lax.broadcasted_iota(jnp.int32, (tb, tb), 0)
    c_i = lax.broadcasted_iota(jnp.int32, (tb, tb), 1)
    tri = jnp.where(((r_i >> 6) == (c_i >> 6)) & (c_i <= r_i), 1.0, 0.0).astype(_bf16)
    bcum = _dot_exact_lhs(tri, g)
    qh = _silu(hq)
    kk = 1.0 - fg
    for h in range(_HEADS):
        sl = slice(h * _DH, (h + 1) * _DH)
        q_s[h] = qh[:, sl]
        k_s[h] = kk[:, sl]
        v_s[h] = hi[:, sl]
        b_s[h] = bcum[:, sl]

    hgg = hgg_ref[...]
    row64 = lax.broadcasted_iota(jnp.int32, (_HEADS, _CH, _DH), 1)
    row16 = lax.broadcasted_iota(jnp.int32, (_HEADS * nsub, _SUB, _DH), 1)

    def chunk(c, carry):
        rows = pl.ds(pl.multiple_of(c * _CH, _CH), _CH)
        q = q_s[:, rows, :]
        k = k_s[:, rows, :]
        v = v_s[:, rows, :]
        b = b_s[:, rows, :]
        st = st_hg[...]
        vb = v.astype(_bf16)
        qe = (q * jnp.exp(b)).astype(_bf16)
        o = jnp.einsum('htd,hed->hte', qe, st.astype(_bf16), preferred_element_type=_f32)
        blocks = [jnp.zeros((_HEADS, _SUB, _CH), _f32)]
        for i in range(1, nsub):
            ref = b[:, _SUB * i - 1:_SUB * i, :]
            qp = q[:, _SUB * i:_SUB * (i + 1), :] * jnp.exp(b[:, _SUB * i:_SUB * (i + 1), :] - ref)
            kp = k * jnp.exp(jnp.where(row64 < _SUB * i, ref - b, _NEG))
            blocks.append(jnp.einsum('htd,hsd->hts', qp.astype(_bf16), kp.astype(_bf16),
                                     preferred_element_type=_f32))
        a = jnp.concatenate(blocks, axis=1)
        o = o + jnp.einsum('hts,hse->hte', a.astype(_bf16), vb, preferred_element_type=_f32)
        q3 = q.reshape(_HEADS * nsub, _SUB, _DH)
        k3 = k.reshape(_HEADS * nsub, _SUB, _DH)
        v3 = v.reshape(_HEADS * nsub, _SUB, _DH)
        b3 = b.reshape(_HEADS * nsub, _SUB, _DH)
        od = jnp.zeros((_HEADS * nsub, _SUB, _DH), _f32)
        for s in range(_SUB):
            e = jnp.exp(jnp.where(row16 >= s, b3 - b3[:, s:s + 1, :], _NEG))
            w = jnp.sum(q3 * e * k3[:, s:s + 1, :], axis=-1, keepdims=True)
            od = od + w * v3[:, s:s + 1, :]
        o = o + od.reshape(_HEADS, _CH, _DH)
        bl = b[:, _CH - 1:_CH, :]
        kd = (k * jnp.exp(bl - b)).astype(_bf16)
        dec = jnp.exp(bl)
        for h in range(_HEADS):
            st_hg[h] = st[h] * dec[h] + _dot(v[h].T.astype(_bf16), kd[h])
        o = o * lax.rsqrt(jnp.mean(o * o, axis=-1, keepdims=True) + _EPS) * hgg
        o_s[:, rows, :] = o
        return carry

    lax.fori_loop(0, nch, chunk, 0)
    hog = sec(3)
    cat_hg = jnp.concatenate([o_s[h] for h in range(_HEADS)], axis=1) * _silu(hog)

    rq = sec(4)
    rk = sec(5)
    rv = sec(6)
    cosf = cos_ref[...]
    sinf = sin_ref[...]
    t_i = lax.broadcasted_iota(jnp.int32, (tb, tb), 0)
    s_i = lax.broadcasted_iota(jnp.int32, (tb, tb), 1)
    rel = (t_i - s_i).astype(_f32)
    causal = t_i >= s_i
    tcol = lax.broadcasted_iota(jnp.int32, (tb, 1), 0).astype(_f32)
    rtg = rtg_ref[...]
    outs = []
    for h in range(_HEADS):
        sl = slice(h * _DH, (h + 1) * _DH)
        lg = math.log(1.0 - 2.0 ** (-5.0 - h))
        xq = rq[:, sl]
        xk = rk[:, sl]
        q = xq * cosf + pltpu.roll(xq, _DH // 2, 1) * sinf
        k = (xk * cosf + pltpu.roll(xk, _DH // 2, 1) * sinf) * (_DH ** -0.5)
        v = rv[:, sl]
        dmat = jnp.where(causal, jnp.exp(jnp.where(causal, rel, 0.0) * lg), 0.0)
        scores = _dot_nt(q.astype(_bf16), k.astype(_bf16)) * dmat
        o = _dot(scores.astype(_bf16), v.astype(_bf16))
        st = st_rt[h]
        qd = q * jnp.exp((tcol + 1.0) * lg)
        o = o + _dot_nt(qd.astype(_bf16), st.astype(_bf16))
        kd = k * jnp.exp((tb - 1.0 - tcol) * lg)
        st_rt[h] = st * math.exp(tb * lg) + _dot(v.T.astype(_bf16), kd.astype(_bf16))
        o = o * lax.rsqrt(jnp.mean(o * o, axis=-1, keepdims=True) + _EPS) * rtg[h]
        outs.append(o)
    rg = sec(7)
    cat_rt = jnp.concatenate(outs, axis=1) * _silu(rg)

    cat = jnp.concatenate([cat_hg, cat_rt], axis=1).astype(_bf16)
    mix = _dot(cat, wout_ref[...])
    x1 = x + g_m * mix
    _ffn_prologue(x1, mod, wr_ref, br_ref, x1_ref, hn2_ref, lg_ref)


def _mixer0(x, mod, w_in, lb, hg_gain, ret_gain, w_out, cosf, sinf, w_router, b_router):
    s = x.shape[0]
    tb = min(_TB, s)
    width = _HEADS * _DH
    full = lambda shape: pl.BlockSpec(shape, lambda i: (0,) * len(shape))
    rowblk = lambda w: pl.BlockSpec((tb, w), lambda i: (i, 0))
    return pl.pallas_call(
        _mixer0_kernel,
        grid=(s // tb,),
        in_specs=[
            rowblk(_D), full((6, _D)), full((_D, 8 * width)), full((1, width)),
            full((_HEADS, 1, _DH)), full((_HEADS, 1, _DH)), full((2 * width, _D)),
            rowblk(_DH), rowblk(_DH), full((_D, _E)), full((1, _E)),
        ],
        out_specs=[rowblk(_D), rowblk(_D), rowblk(_E)],
        out_shape=[jax.ShapeDtypeStruct((s, _D), _f32), jax.ShapeDtypeStruct((s, _D), _f32),
                   jax.ShapeDtypeStruct((s, _E), _f32)],
        scratch_shapes=[pltpu.VMEM((_HEADS, tb, _DH), _f32)] * 5
        + [pltpu.VMEM((_HEADS, _DH, _DH), _f32)] * 2,
        compiler_params=pltpu.CompilerParams(dimension_semantics=("arbitrary",),
                                             vmem_limit_bytes=_VMEM_LIMIT),
        name="mixer_hgrn2_retention",
    )(x, mod, w_in.astype(_bf16), lb.reshape(1, width), hg_gain.reshape(_HEADS, 1, _DH),
      ret_gain.reshape(_HEADS, 1, _DH), w_out.astype(_bf16), cosf, sinf, w_router,
      b_router.reshape(1, _E))


def _pool_kernel(x_ref, mod_ref, pw_ref, pb_ref, ps_ref, wr_ref, br_ref,
                 x1_ref, hn2_ref, lg_ref, halo):
    tb = x_ref.shape[0]
    i = pl.program_id(0)

    @pl.when(i == 0)
    def _():
        halo[...] = jnp.zeros_like(halo)

    x = x_ref[...]
    mod = mod_ref[...]
    sh_m, sc_m, g_m = mod[0:1, :], mod[1:2, :], mod[2:3, :]
    hn = _rms(x) * (1.0 + sc_m) + sh_m
    ext = jnp.concatenate([halo[...], hn], axis=0)
    halo[...] = hn[tb - _HALO:, :]
    t = (i * tb + lax.broadcasted_iota(jnp.int32, (tb, 1), 0)).astype(_f32)
    gw = _D // len(_POOL_WINDOWS)
    ys = []
    for gi, w in enumerate(_POOL_WINDOWS):
        acc = ext[:, gi * gw:(gi + 1) * gw]
        shift = 1
        while shift < w:
            acc = acc + pltpu.roll(acc, shift, 0)
            shift *= 2
        win = acc[_HALO:, :]
        cnt = jnp.minimum(t + 1.0, float(w))
        p = win / cnt - hn[:, gi * gw:(gi + 1) * gw]
        ys.append(_dot(p.astype(_bf16), pw_ref[gi]) + pb_ref[gi])
    mix = jnp.concatenate(ys, axis=1) * ps_ref[...]
    x1 = x + g_m * mix
    _ffn_prologue(x1, mod, wr_ref, br_ref, x1_ref, hn2_ref, lg_ref)


def _pool_mixer(x, mod, pool_w, pool_b, pool_scale, w_router, b_router):
    s = x.shape[0]
    tb = min(_TB, s)
    ng = len(_POOL_WINDOWS)
    gw = _D // ng
    full = lambda shape: pl.BlockSpec(shape, lambda i: (0,) * len(shape))
    rowblk = lambda w: pl.BlockSpec((tb, w), lambda i: (i, 0))
    return pl.pallas_call(
        _pool_kernel,
        grid=(s // tb,),
        in_specs=[rowblk(_D), full((6, _D)), full((ng, gw, gw)), full((ng, 1, gw)), full((1, _D)),
                  full((_D, _E)), full((1, _E))],
        out_specs=[rowblk(_D), rowblk(_D), rowblk(_E)],
        out_shape=[jax.ShapeDtypeStruct((s, _D), _f32), jax.ShapeDtypeStruct((s, _D), _f32),
                   jax.ShapeDtypeStruct((s, _E), _f32)],
        scratch_shapes=[pltpu.VMEM((_HALO, _D), _f32)],
        compiler_params=pltpu.CompilerParams(dimension_semantics=("arbitrary",),
                                             vmem_limit_bytes=_VMEM_LIMIT),
        name="mixer_pool",
    )(x, mod, pool_w.astype(_bf16), pool_b.reshape(ng, 1, gw), pool_scale.reshape(1, _D),
      w_router, b_router.reshape(1, _E))


def _route_kernel(lg_ref, route_ref, cnt_ref, carry):
    tb = lg_ref.shape[0]

    @pl.when(pl.program_id(0) == 0)
    def _():
        carry[...] = jnp.zeros_like(carry)

    vals = lg_ref[...]
    lane = lax.broadcasted_iota(jnp.int32, (tb, _E), 1).astype(_f32)
    tops, idxs, hots = [], [], []
    for _ in range(_K):
        m = jnp.max(vals, axis=-1, keepdims=True)
        idx = jnp.min(jnp.where(vals == m, lane, float(_E)), axis=-1, keepdims=True)
        hot = lane == idx
        vals = jnp.where(hot, -jnp.inf, vals)
        tops.append(m)
        idxs.append(idx)
        hots.append(hot)
    ex = [jnp.exp(m - tops[0]) for m in tops]
    den = ex[0] + ex[1] + ex[2] + ex[3]
    member = jnp.zeros((tb, _E), _f32)
    for hot in hots:
        member = member + jnp.where(hot, 1.0, 0.0)
    r_i = lax.broadcasted_iota(jnp.int32, (tb, tb), 0)
    c_i = lax.broadcasted_iota(jnp.int32, (tb, tb), 1)
    below = jnp.where(c_i < r_i, 1.0, 0.0).astype(_bf16)
    cum = _dot(below, member.astype(_bf16)) + carry[...]
    carry[...] = carry[...] + jnp.sum(member, axis=0, keepdims=True)
    cnt_ref[...] = carry[...]
    out_lane = lax.broadcasted_iota(jnp.int32, (tb, 128), 1)
    out = jnp.zeros((tb, 128), _f32)
    for k in range(_K):
        rank = jnp.sum(jnp.where(hots[k], cum, 0.0), axis=-1, keepdims=True)
        out = jnp.where(out_lane == k, idxs[k], out)
        out = jnp.where(out_lane == _K + k, ex[k] / den, out)
        out = jnp.where(out_lane == 2 * _K + k, rank, out)
    route_ref[...] = out


def _route(logits):
    s = logits.shape[0]
    tb = min(_TBR, s)
    return pl.pallas_call(
        _route_kernel,
        grid=(s // tb,),
        in_specs=[pl.BlockSpec((tb, _E), lambda i: (i, 0))],
        out_specs=[pl.BlockSpec((tb, 128), lambda i: (i, 0)), pl.BlockSpec((1, _E), lambda i: (0, 0))],
        out_shape=[jax.ShapeDtypeStruct((s, 128), _f32), jax.ShapeDtypeStruct((1, _E), _f32)],
        scratch_shapes=[pltpu.VMEM((1, _E), _f32)],
        compiler_params=pltpu.CompilerParams(dimension_semantics=("arbitrary",)),
        name="route_topk_rank",
    )(logits)


def _dest_kernel(route_ref, cnt_ref, dest_ref, blk_ref):
    tb = route_ref.shape[0]
    nb = blk_ref.shape[0]
    cnt = cnt_ref[...]
    padded = jnp.floor((cnt + (_BLK - 1.0)) * (1.0 / _BLK)) * _BLK
    j_i = lax.broadcasted_iota(jnp.int32, (_E, _E), 0)
    e_i = lax.broadcasted_iota(jnp.int32, (_E, _E), 1)
    before = jnp.where(j_i < e_i, 1.0, 0.0).astype(_bf16)
    p0, p1, p2 = _split3(jnp.broadcast_to(padded, (8, _E)))
    start = (_dot(p0, before) + _dot(p1, before) + _dot(p2, before))[0:1, :]
    route = route_ref[...]
    lane = lax.broadcasted_iota(jnp.int32, (tb, _E), 1).astype(_f32)
    out_lane = lax.broadcasted_iota(jnp.int32, (tb, 128), 1)
    out = jnp.zeros((tb, 128), _f32)
    for k in range(_K):
        idx = route[:, k:k + 1]
        rank = route[:, 2 * _K + k:2 * _K + k + 1]
        base = jnp.sum(jnp.where(lane == idx, start, 0.0), axis=-1, keepdims=True)
        out = jnp.where(out_lane == k, base + rank, out)
    dest_ref[...] = out.astype(jnp.int32)

    @pl.when(pl.program_id(0) == 0)
    def _():
        end = start + padded
        row0 = (lax.broadcasted_iota(jnp.int32, (nb, _E), 0) * _BLK).astype(_f32)
        blk_e = jnp.sum(jnp.where(end <= row0, 1.0, 0.0), axis=-1, keepdims=True)
        blk_e = jnp.minimum(blk_e, _E - 1.0)
        total = jnp.max(end, axis=-1, keepdims=True)
        used = jnp.where(row0[:, 0:1] < total, 1.0, 0.0)
        bl = lax.broadcasted_iota(jnp.int32, (nb, 128), 1)
        blk_ref[...] = jnp.where(bl == 0, blk_e, jnp.where(bl == 1, used, 0.0)).astype(jnp.int32)


def _dest(route, counts, nb):
    s = route.shape[0]
    tb = min(_TBR, s)
    return pl.pallas_call(
        _dest_kernel,
        grid=(s // tb,),
        in_specs=[pl.BlockSpec((tb, 128), lambda i: (i, 0)), pl.BlockSpec((1, _E), lambda i: (0, 0))],
        out_specs=[pl.BlockSpec((tb, 128), lambda i: (i, 0)), pl.BlockSpec((nb, 128), lambda i: (0, 0))],
        out_shape=[jax.ShapeDtypeStruct((s, 128), jnp.int32), jax.ShapeDtypeStruct((nb, 128), jnp.int32)],
        compiler_params=pltpu.CompilerParams(dimension_semantics=("arbitrary",)),
        name="route_dest",
    )(route, counts)


def _row_copy(src, src_row, dst, dst_row, sem):
    return pltpu.make_async_copy(src.at[pl.ds(src_row, 1)], dst.at[pl.ds(dst_row, 1)], sem)


def _dispatch_kernel(dest_ref, hn_ref, xs_in_ref, xs_ref, sem):
    del xs_in_ref
    n = dest_ref.shape[2] // _K
    base = pl.program_id(0) * n

    def issue(t, carry):
        for k in range(_K):
            _row_copy(hn_ref, base + t, xs_ref, dest_ref[0, 0, t * _K + k], sem).start()
        return carry

    lax.fori_loop(0, n, issue, 0)

    def drain(t, carry):
        for k in range(_K):
            _row_copy(hn_ref, 0, xs_ref, 0, sem).wait()
        return carry

    lax.fori_loop(0, n, drain, 0)


def _dispatch(dest, hn, n_rows):
    s = hn.shape[0]
    tb = min(_TBD, s)
    xs0 = jnp.zeros((n_rows, _D), _f32)
    return pl.pallas_call(
        _dispatch_kernel,
        grid=(s // tb,),
        in_specs=[
            pl.BlockSpec((1, 1, tb * _K), lambda i: (i, 0, 0), memory_space=pltpu.SMEM),
            pl.BlockSpec(memory_space=pl.ANY),
            pl.BlockSpec(memory_space=pl.ANY),
        ],
        out_specs=pl.BlockSpec(memory_space=pl.ANY),
        out_shape=jax.ShapeDtypeStruct((n_rows, _D), _f32),
        scratch_shapes=[pltpu.SemaphoreType.DMA(())],
        input_output_aliases={2: 0},
        compiler_params=pltpu.CompilerParams(dimension_semantics=("arbitrary",)),
        name="moe_dispatch",
    )(dest.reshape(s // tb, 1, tb * _K), hn, xs0)


def _expert_kernel(blk_e_ref, blk_ok_ref, xs_ref, wgu_ref, bgu_ref, wd_ref, bd_ref, ys_ref,
                   wgu_c, wd_c):
    i = pl.program_id(0)
    e = blk_e_ref[i]
    prev = blk_e_ref[jnp.maximum(i - 1, 0)]

    @pl.when((i == 0) | (e != prev))
    def _():
        step = 128
        for r in range(0, _D, step):
            wgu_c[r:r + step, :] = wgu_ref[0, r:r + step, :].astype(_bf16)
        for r in range(0, _F, step):
            wd_c[r:r + step, :] = wd_ref[0, r:r + step, :].astype(_bf16)

    @pl.when(blk_ok_ref[i] != 0)
    def _():
        x = xs_ref[...].astype(_bf16)
        gu = _dot(x, wgu_c[...]) + bgu_ref[0]
        x_glu = jnp.minimum(gu[:, :_F], _SWIGLU_LIMIT)
        x_lin = jnp.clip(gu[:, _F:], -_SWIGLU_LIMIT, _SWIGLU_LIMIT)
        act = x_glu * jax.nn.sigmoid(_SWIGLU_ALPHA * x_glu) * (x_lin + 1.0)
        ys_ref[...] = _dot(act.astype(_bf16), wd_c[...]) + bd_ref[0]

    @pl.when(blk_ok_ref[i] == 0)
    def _():
        ys_ref[...] = jnp.zeros_like(ys_ref)


def _experts(blk_e, blk_ok, xs, w_gu, b_gu, w_down, b_down):
    n_rows = xs.shape[0]
    nb = n_rows // _BLK
    grid_spec = pltpu.PrefetchScalarGridSpec(
        num_scalar_prefetch=2,
        grid=(nb,),
        in_specs=[
            pl.BlockSpec((_BLK, _D), lambda i, be, ok: (i, 0)),
            pl.BlockSpec((1, _D, 2 * _F), lambda i, be, ok: (be[i], 0, 0)),
            pl.BlockSpec((1, 1, 2 * _F), lambda i, be, ok: (be[i], 0, 0)),
            pl.BlockSpec((1, _F, _D), lambda i, be, ok: (be[i], 0, 0)),
            pl.BlockSpec((1, 1, _D), lambda i, be, ok: (be[i], 0, 0)),
        ],
        out_specs=pl.BlockSpec((_BLK, _D), lambda i, be, ok: (i, 0)),
        scratch_shapes=[pltpu.VMEM((_D, 2 * _F), _bf16), pltpu.VMEM((_F, _D), _bf16)],
    )
    return pl.pallas_call(
        _expert_kernel,
        grid_spec=grid_spec,
        out_shape=jax.ShapeDtypeStruct((n_rows, _D), _f32),
        compiler_params=pltpu.CompilerParams(dimension_semantics=("arbitrary",),
                                             vmem_limit_bytes=_VMEM_LIMIT),
        name="moe_experts",
    )(blk_e, blk_ok, xs, w_gu, b_gu.reshape(_E, 1, 2 * _F), w_down, b_down.reshape(_E, 1, _D))


def _combine_kernel(dest_ref, route_ref, x1_ref, mod_ref, fin_ref, ys_ref, out_ref, buf, sem, *,
                    final):
    tb = x1_ref.shape[0]

    def issue(t, carry):
        for k in range(_K):
            pltpu.make_async_copy(ys_ref.at[pl.ds(dest_ref[0, 0, t * _K + k], 1)],
                                  buf.at[k, pl.ds(t, 1)], sem).start()
        return carry

    lax.fori_loop(0, tb, issue, 0)

    def drain(t, carry):
        for k in range(_K):
            pltpu.make_async_copy(ys_ref.at[pl.ds(0, 1)], buf.at[k, pl.ds(t, 1)], sem).wait()
        return carry

    lax.fori_loop(0, tb, drain, 0)

    route = route_ref[...]
    moe = jnp.zeros((tb, _D), _f32)
    for k in range(_K):
        moe = moe + route[:, _K + k:_K + k + 1] * buf[k]
    x2 = x1_ref[...] + mod_ref[5:6, :] * moe
    if final:
        x2 = _rms(x2) * fin_ref[...]
    out_ref[...] = x2


def _combine(dest, route, x1, mod, final_norm, ys, final):
    s = x1.shape[0]
    tb = min(_TBC, s)
    return pl.pallas_call(
        functools.partial(_combine_kernel, final=final),
        grid=(s // tb,),
        in_specs=[
            pl.BlockSpec((1, 1, tb * _K), lambda i: (i, 0, 0), memory_space=pltpu.SMEM),
            pl.BlockSpec((tb, 128), lambda i: (i, 0)),
            pl.BlockSpec((tb, _D), lambda i: (i, 0)),
            pl.BlockSpec((6, _D), lambda i: (0, 0)),
            pl.BlockSpec((1, _D), lambda i: (0, 0)),
            pl.BlockSpec(memory_space=pl.ANY),
        ],
        out_specs=pl.BlockSpec((tb, _D), lambda i: (i, 0)),
        out_shape=jax.ShapeDtypeStruct((s, _D), _f32),
        scratch_shapes=[pltpu.VMEM((_K, tb, _D), _f32), pltpu.SemaphoreType.DMA(())],
        compiler_params=pltpu.CompilerParams(dimension_semantics=("arbitrary",),
                                             vmem_limit_bytes=_VMEM_LIMIT),
        name="moe_combine",
    )(dest.reshape(s // tb, 1, tb * _K), route, x1, mod, final_norm.reshape(1, _D), ys)


def _moe(x1, hn2, logits, mod, final_norm, w_gu, b_gu, w_down, b_down, final):
    s = x1.shape[0]
    n_rows = s * _K + _E * _BLK
    nb = n_rows // _BLK
    route, counts = _route(logits)
    dest_c, blk = _dest(route, counts, nb)
    dest = dest_c[:, :_K]
    xs = _dispatch(dest, hn2, n_rows)
    ys = _experts(blk[:, 0], blk[:, 1], xs, w_gu, b_gu, w_down, b_down)
    return _combine(dest, route, x1, mod, final_norm, ys, final)


def _rotary_tables(s):
    half = _DH // 2
    inv = 10000.0 ** (-jnp.arange(half, dtype=_f32) / half)
    ang = jnp.arange(s, dtype=_f32)[:, None] * inv[None, :]
    cos, sin = jnp.cos(ang), jnp.sin(ang)
    return jnp.concatenate([cos, cos], axis=1), jnp.concatenate([-sin, sin], axis=1)


def _lower_bounds_kernel(p_ref, o_ref):
    p = p_ref[...]
    n = p.shape[0]
    m = p[0:1, :]
    for j in range(1, n):
        m = jnp.maximum(m, p[j:j + 1, :])
    ex = [jnp.exp(p[j:j + 1, :] - m) for j in range(n)]
    den = ex[0]
    for j in range(1, n):
        den = den + ex[j]
    acc = jnp.zeros_like(den)
    rows = []
    for j in range(n):
        acc = acc + ex[j] / den
        rows.append(acc)
    o_ref[...] = jnp.concatenate(rows, axis=0)


def _lower_bounds(p):
    return pl.pallas_call(
        _lower_bounds_kernel,
        out_shape=jax.ShapeDtypeStruct(p.shape, _f32),
        name="hgrn2_lower_bounds",
    )(p)


def kernel(x, c, w_ada, b_ada, w_in, hg_lower_bounds, hg_norm, ret_norm, w_out, pool_w, pool_b,
           pool_scale, w_router, b_router, w_gu, b_gu, w_down, b_down, final_norm):
    b, s, d = x.shape
    assert b == 1 and d == _D
    depth = w_ada.shape[0]
    assert depth == 2
    xs = x.reshape(s, d)
    mod = _ada(c, w_ada, b_ada)
    lb_all = _lower_bounds(hg_lower_bounds)
    cosf, sinf = _rotary_tables(s)
    x1, hn2, logits = _mixer0(xs, mod[0], w_in[0], lb_all[0], hg_norm[0], ret_norm[0], w_out[0],
                              cosf, sinf, w_router[0], b_router[0])
    xs = _moe(x1, hn2, logits, mod[0], final_norm, w_gu[0], b_gu[0], w_down[0], b_down[0], False)
    x1, hn2, logits = _pool_mixer(xs, mod[1], pool_w[0], pool_b[0], pool_scale[0], w_router[1],
                                  b_router[1])
    xs = _moe(x1, hn2, logits, mod[1], final_norm, w_gu[1], b_gu[1], w_down[1], b_down[1], True)
    return xs.reshape(b, s, d)
```

```python
import functools
import math

import jax
import jax.numpy as jnp
from jax import lax
from jax.experimental import pallas as pl
from jax.experimental.pallas import tpu as pltpu

_D = 1024
_E = 32
_K = 4
_F = 1024
_HEADS = 4
_DH = 128
_CH = 64
_SUB = 16
_EPS = 1e-6
_BLK = 256
_TB = 256
_TT = 512
_ALIGN = 8
_RT_ROWS = 16
_NEG = -1e30
_SWIGLU_LIMIT = 7.0
_SWIGLU_ALPHA = 1.702
_POOL_WINDOWS = (2, 4, 8, 16)
_HALO = 16
_VMEM_LIMIT = 56 * 1024 * 1024

_f32 = jnp.float32
_bf16 = jnp.bfloat16


def _dot(a, b):
    return jnp.dot(a, b, preferred_element_type=_f32)


def _dot_nt(a, b):
    return lax.dot_general(a, b, (((1,), (1,)), ((), ())), preferred_element_type=_f32)


def _split3(a):
    a0 = a.astype(_bf16)
    r1 = a - a0.astype(_f32)
    a1 = r1.astype(_bf16)
    a2 = (r1 - a1.astype(_f32)).astype(_bf16)
    return a0, a1, a2


def _dot_exact_lhs(m, a):
    a0, a1, a2 = _split3(a)
    return _dot(m, a0) + _dot(m, a1) + _dot(m, a2)


def _dot_split(a, w):
    a0, a1, _ = _split3(a)
    w0, w1, _ = _split3(w)
    return _dot(a0, w0) + _dot(a1, w0) + _dot(a0, w1)


def _rms(x):
    return x * lax.rsqrt(jnp.mean(x * x, axis=-1, keepdims=True) + _EPS)


def _silu(x):
    return x * jax.nn.sigmoid(x)


def _ada_kernel(c_ref, w_ref, b_ref, o_ref):
    c = c_ref[...]
    cond = _silu(c)
    o_ref[0] = jnp.sum(w_ref[0] * cond, axis=0, keepdims=True) + b_ref[0]


def _ada(c, w_ada, b_ada):
    depth = w_ada.shape[0]
    n = depth * 6
    out = pl.pallas_call(
        _ada_kernel,
        grid=(n,),
        in_specs=[
            pl.BlockSpec((_D, 1), lambda j: (0, 0)),
            pl.BlockSpec((1, _D, _D), lambda j: (j // 6, 0, j % 6)),
            pl.BlockSpec((1, 1, _D), lambda j: (j, 0, 0)),
        ],
        out_specs=pl.BlockSpec((1, 1, _D), lambda j: (j, 0, 0)),
        out_shape=jax.ShapeDtypeStruct((n, 1, _D), _f32),
        name="ada_mod",
    )(c.reshape(_D, 1), w_ada, b_ada.reshape(n, 1, _D))
    return out.reshape(depth, 6, _D)


def _ffn_prologue(x1, mod, wr_ref, br_ref, x1_ref, hn2_ref, lg_ref):
    sh_f, sc_f = mod[3:4, :], mod[4:5, :]
    hn2 = _rms(x1) * (1.0 + sc_f) + sh_f
    x1_ref[...] = x1
    hn2_ref[...] = hn2
    lg_ref[...] = _dot_split(hn2, wr_ref[...]) + br_ref[...]


def _mixer0_kernel(x_ref, mod_ref, win_ref, lb_ref, hgg_ref, rtg_ref, wout_ref, cos_ref, sin_ref,
                   wr_ref, br_ref, x1_ref, hn2_ref, lg_ref,
                   q_s, k_s, v_s, b_s, o_s, st_hg, st_rt):
    tb = x_ref.shape[0]
    nch = tb // _CH
    nsub = _CH // _SUB

    @pl.when(pl.program_id(0) == 0)
    def _():
        st_hg[...] = jnp.zeros_like(st_hg)
        st_rt[...] = jnp.zeros_like(st_rt)

    x = x_ref[...]
    mod = mod_ref[...]
    sh_m, sc_m, g_m = mod[0:1, :], mod[1:2, :], mod[2:3, :]
    hb = (_rms(x) * (1.0 + sc_m) + sh_m).astype(_bf16)
    width = _HEADS * _DH

    def sec(j):
        return _dot(hb, win_ref[:, j * width:(j + 1) * width])

    hq = sec(0)
    lb = lb_ref[...]
    fg = lb + (1.0 - lb) * jax.nn.sigmoid(sec(1))
    hi = sec(2)
    g = jnp.log(fg)
    r_i = lax.broadcasted_iota(jnp.int32, (tb, tb), 0)
    c_i = lax.broadcasted_iota(jnp.int32, (tb, tb), 1)
    tri = jnp.where(((r_i >> 6) == (c_i >> 6)) & (c_i <= r_i), 1.0, 0.0).astype(_bf16)
    bcum = _dot_exact_lhs(tri, g)
    qh = _silu(hq)
    kk = 1.0 - fg
    for h in range(_HEADS):
        sl = slice(h * _DH, (h + 1) * _DH)
        q_s[h] = qh[:, sl]
        k_s[h] = kk[:, sl]
        v_s[h] = hi[:, sl]
        b_s[h] = bcum[:, sl]

    hgg = hgg_ref[...]
    row64 = lax.broadcasted_iota(jnp.int32, (_HEADS, _CH, _DH), 1)
    row16 = lax.broadcasted_iota(jnp.int32, (_HEADS * nsub, _SUB, _DH), 1)

    def chunk(c, carry):
        rows = pl.ds(pl.multiple_of(c * _CH, _CH), _CH)
        q = q_s[:, rows, :]
        k = k_s[:, rows, :]
        v = v_s[:, rows, :]
        b = b_s[:, rows, :]
        st = st_hg[...]
        vb = v.astype(_bf16)
        qe = (q * jnp.exp(b)).astype(_bf16)
        o = jnp.einsum('htd,hed->hte', qe, st.astype(_bf16), preferred_element_type=_f32)
        blocks = [jnp.zeros((_HEADS, _SUB, _CH), _f32)]
        for i in range(1, nsub):
            ref = b[:, _SUB * i - 1:_SUB * i, :]
            qp = q[:, _SUB * i:_SUB * (i + 1), :] * jnp.exp(b[:, _SUB * i:_SUB * (i + 1), :] - ref)
            kp = k * jnp.exp(jnp.where(row64 < _SUB * i, ref - b, _NEG))
            blocks.append(jnp.einsum('htd,hsd->hts', qp.astype(_bf16), kp.astype(_bf16),
                                     preferred_element_type=_f32))
        a = jnp.concatenate(blocks, axis=1)
        o = o + jnp.einsum('hts,hse->hte', a.astype(_bf16), vb, preferred_element_type=_f32)
        q3 = q.reshape(_HEADS * nsub, _SUB, _DH)
        k3 = k.reshape(_HEADS * nsub, _SUB, _DH)
        v3 = v.reshape(_HEADS * nsub, _SUB, _DH)
        b3 = b.reshape(_HEADS * nsub, _SUB, _DH)
        od = jnp.zeros((_HEADS * nsub, _SUB, _DH), _f32)
        for s in range(_SUB):
            e = jnp.exp(jnp.where(row16 >= s, b3 - b3[:, s:s + 1, :], _NEG))
            w = jnp.sum(q3 * e * k3[:, s:s + 1, :], axis=-1, keepdims=True)
            od = od + w * v3[:, s:s + 1, :]
        o = o + od.reshape(_HEADS, _CH, _DH)
        bl = b[:, _CH - 1:_CH, :]
        kd = (k * jnp.exp(bl - b)).astype(_bf16)
        dec = jnp.exp(bl)
        for h in range(_HEADS):
            st_hg[h] = st[h] * dec[h] + _dot(v[h].T.astype(_bf16), kd[h])
        o = o * lax.rsqrt(jnp.mean(o * o, axis=-1, keepdims=True) + _EPS) * hgg
        o_s[:, rows, :] = o
        return carry

    lax.fori_loop(0, nch, chunk, 0)
    hog = sec(3)
    cat_hg = jnp.concatenate([o_s[h] for h in range(_HEADS)], axis=1) * _silu(hog)

    rq = sec(4)
    rk = sec(5)
    rv = sec(6)
    cosf = cos_ref[...]
    sinf = sin_ref[...]
    t_i = lax.broadcasted_iota(jnp.int32, (tb, tb), 0)
    s_i = lax.broadcasted_iota(jnp.int32, (tb, tb), 1)
    rel = (t_i - s_i).astype(_f32)
    causal = t_i >= s_i
    tcol = lax.broadcasted_iota(jnp.int32, (tb, 1), 0).astype(_f32)
    rtg = rtg_ref[...]
    outs = []
    for h in range(_HEADS):
        sl = slice(h * _DH, (h + 1) * _DH)
        lg = math.log(1.0 - 2.0 ** (-5.0 - h))
        xq = rq[:, sl]
        xk = rk[:, sl]
        q = xq * cosf + pltpu.roll(xq, _DH // 2, 1) * sinf
        k = (xk * cosf + pltpu.roll(xk, _DH // 2, 1) * sinf) * (_DH ** -0.5)
        v = rv[:, sl]
        dmat = jnp.where(causal, jnp.exp(jnp.where(causal, rel, 0.0) * lg), 0.0)
        scores = _dot_nt(q.astype(_bf16), k.astype(_bf16)) * dmat
        o = _dot(scores.astype(_bf16), v.astype(_bf16))
        st = st_rt[h]
        qd = q * jnp.exp((tcol + 1.0) * lg)
        o = o + _dot_nt(qd.astype(_bf16), st.astype(_bf16))
        kd = k * jnp.exp((tb - 1.0 - tcol) * lg)
        st_rt[h] = st * math.exp(tb * lg) + _dot(v.T.astype(_bf16), kd.astype(_bf16))
        o = o * lax.rsqrt(jnp.mean(o * o, axis=-1, keepdims=True) + _EPS) * rtg[h]
        outs.append(o)
    rg = sec(7)
    cat_rt = jnp.concatenate(outs, axis=1) * _silu(rg)

    cat = jnp.concatenate([cat_hg, cat_rt], axis=1).astype(_bf16)
    mix = _dot(cat, wout_ref[...])
    x1 = x + g_m * mix
    _ffn_prologue(x1, mod, wr_ref, br_ref, x1_ref, hn2_ref, lg_ref)


def _mixer0(x, mod, w_in, lb, hg_gain, ret_gain, w_out, cosf, sinf, w_router, b_router):
    s = x.shape[0]
    tb = min(_TB, s)
    width = _HEADS * _DH
    full = lambda shape: pl.BlockSpec(shape, lambda i: (0,) * len(shape))
    rowblk = lambda w: pl.BlockSpec((tb, w), lambda i: (i, 0))
    return pl.pallas_call(
        _mixer0_kernel,
        grid=(s // tb,),
        in_specs=[
            rowblk(_D), full((6, _D)), full((_D, 8 * width)), full((1, width)),
            full((_HEADS, 1, _DH)), full((_HEADS, 1, _DH)), full((2 * width, _D)),
            rowblk(_DH), rowblk(_DH), full((_D, _E)), full((1, _E)),
        ],
        out_specs=[rowblk(_D), rowblk(_D), rowblk(_E)],
        out_shape=[jax.ShapeDtypeStruct((s, _D), _f32), jax.ShapeDtypeStruct((s, _D), _f32),
                   jax.ShapeDtypeStruct((s, _E), _f32)],
        scratch_shapes=[pltpu.VMEM((_HEADS, tb, _DH), _f32)] * 5
        + [pltpu.VMEM((_HEADS, _DH, _DH), _f32)] * 2,
        compiler_params=pltpu.CompilerParams(dimension_semantics=("arbitrary",),
                                             vmem_limit_bytes=_VMEM_LIMIT),
        name="mixer_hgrn2_retention",
    )(x, mod, w_in.astype(_bf16), lb.reshape(1, width), hg_gain.reshape(_HEADS, 1, _DH),
      ret_gain.reshape(_HEADS, 1, _DH), w_out.astype(_bf16), cosf, sinf, w_router,
      b_router.reshape(1, _E))


def _pool_kernel(x_ref, mod_ref, pw_ref, pb_ref, ps_ref, wr_ref, br_ref,
                 x1_ref, hn2_ref, lg_ref, halo):
    tb = x_ref.shape[0]
    i = pl.program_id(0)

    @pl.when(i == 0)
    def _():
        halo[...] = jnp.zeros_like(halo)

    x = x_ref[...]
    mod = mod_ref[...]
    sh_m, sc_m, g_m = mod[0:1, :], mod[1:2, :], mod[2:3, :]
    hn = _rms(x) * (1.0 + sc_m) + sh_m
    ext = jnp.concatenate([halo[...], hn], axis=0)
    halo[...] = hn[tb - _HALO:, :]
    t = (i * tb + lax.broadcasted_iota(jnp.int32, (tb, 1), 0)).astype(_f32)
    gw = _D // len(_POOL_WINDOWS)
    ys = []
    for gi, w in enumerate(_POOL_WINDOWS):
        acc = ext[:, gi * gw:(gi + 1) * gw]
        shift = 1
        while shift < w:
            acc = acc + pltpu.roll(acc, shift, 0)
            shift *= 2
        win = acc[_HALO:, :]
        cnt = jnp.minimum(t + 1.0, float(w))
        p = win / cnt - hn[:, gi * gw:(gi + 1) * gw]
        ys.append(_dot(p.astype(_bf16), pw_ref[gi]) + pb_ref[gi])
    mix = jnp.concatenate(ys, axis=1) * ps_ref[...]
    x1 = x + g_m * mix
    _ffn_prologue(x1, mod, wr_ref, br_ref, x1_ref, hn2_ref, lg_ref)


def _pool_mixer(x, mod, pool_w, pool_b, pool_scale, w_router, b_router):
    s = x.shape[0]
    tb = min(_TB, s)
    ng = len(_POOL_WINDOWS)
    gw = _D // ng
    full = lambda shape: pl.BlockSpec(shape, lambda i: (0,) * len(shape))
    rowblk = lambda w: pl.BlockSpec((tb, w), lambda i: (i, 0))
    return pl.pallas_call(
        _pool_kernel,
        grid=(s // tb,),
        in_specs=[rowblk(_D), full((6, _D)), full((ng, gw, gw)), full((ng, 1, gw)), full((1, _D)),
                  full((_D, _E)), full((1, _E))],
        out_specs=[rowblk(_D), rowblk(_D), rowblk(_E)],
        out_shape=[jax.ShapeDtypeStruct((s, _D), _f32), jax.ShapeDtypeStruct((s, _D), _f32),
                   jax.ShapeDtypeStruct((s, _E), _f32)],
        scratch_shapes=[pltpu.VMEM((_HALO, _D), _f32)],
        compiler_params=pltpu.CompilerParams(dimension_semantics=("arbitrary",),
                                             vmem_limit_bytes=_VMEM_LIMIT),
        name="mixer_pool",
    )(x, mod, pool_w.astype(_bf16), pool_b.reshape(ng, 1, gw), pool_scale.reshape(1, _D),
      w_router, b_router.reshape(1, _E))


def _route_kernel(lg_ref, route_ref, route_t_ref, cnt_ref):
    tt = lg_ref.shape[0]
    vals = lg_ref[...]
    lane = lax.broadcasted_iota(jnp.int32, (tt, _E), 1).astype(_f32)
    tops, idxs, hots = [], [], []
    for _ in range(_K):
        m = jnp.max(vals, axis=-1, keepdims=True)
        idx = jnp.min(jnp.where(vals == m, lane, float(_E)), axis=-1, keepdims=True)
        hot = lane == idx
        vals = jnp.where(hot, -jnp.inf, vals)
        tops.append(m)
        idxs.append(idx)
        hots.append(hot)
    ex = [jnp.exp(m - tops[0]) for m in tops]
    den = ex[0] + ex[1] + ex[2] + ex[3]
    member = jnp.zeros((tt, _E), _f32)
    for hot in hots:
        member = member + jnp.where(hot, 1.0, 0.0)
    r_i = lax.broadcasted_iota(jnp.int32, (tt, tt), 0)
    c_i = lax.broadcasted_iota(jnp.int32, (tt, tt), 1)
    below = jnp.where(c_i < r_i, 1.0, 0.0).astype(_bf16)
    cum = _dot(below, member.astype(_bf16))
    cnt_ref[0] = jnp.sum(member, axis=0, keepdims=True)
    out_lane = lax.broadcasted_iota(jnp.int32, (tt, 128), 1)
    out = jnp.zeros((tt, 128), _f32)
    for k in range(_K):
        rank = jnp.sum(jnp.where(hots[k], cum, 0.0), axis=-1, keepdims=True)
        out = jnp.where(out_lane == k, idxs[k], out)
        out = jnp.where(out_lane == _K + k, ex[k] / den, out)
        out = jnp.where(out_lane == 2 * _K + k, rank, out)
    route_ref[...] = out
    route_t_ref[...] = out.T[0:_RT_ROWS, :]


def _route(logits):
    s = logits.shape[0]
    tt = min(_TT, s)
    nt = s // tt
    return pl.pallas_call(
        _route_kernel,
        grid=(nt,),
        in_specs=[pl.BlockSpec((tt, _E), lambda i: (i, 0))],
        out_specs=[pl.BlockSpec((tt, 128), lambda i: (i, 0)),
                   pl.BlockSpec((_RT_ROWS, tt), lambda i: (0, i)),
                   pl.BlockSpec((1, 1, _E), lambda i: (i, 0, 0))],
        out_shape=[jax.ShapeDtypeStruct((s, 128), _f32),
                   jax.ShapeDtypeStruct((_RT_ROWS, s), _f32),
                   jax.ShapeDtypeStruct((nt, 1, _E), _f32)],
        compiler_params=pltpu.CompilerParams(dimension_semantics=("arbitrary",)),
        name="route_topk_rank",
    )(logits)


def _offsets_kernel(cnt_ref, lo_ref, lo_t_ref, goff_ref, nch_ref, tail_ref, blk_ref):
    cnt = cnt_ref[...]
    ntp = cnt.shape[0]
    nb = blk_ref.shape[0]
    r8 = jnp.floor((cnt + (_ALIGN - 1.0)) * (1.0 / _ALIGN)) * _ALIGN
    j_i = lax.broadcasted_iota(jnp.int32, (_E, _E), 0)
    e_i = lax.broadcasted_iota(jnp.int32, (_E, _E), 1)
    before = jnp.where(j_i < e_i, 1.0, 0.0).astype(_bf16)

    def times_before(a):
        a0, a1, a2 = _split3(a)
        return _dot(a0, before) + _dot(a1, before) + _dot(a2, before)

    lo = times_before(r8)
    tot = jnp.sum(r8, axis=0, keepdims=True)
    cap = jnp.floor((tot + (_BLK - 1.0)) * (1.0 / _BLK)) * _BLK
    start = times_before(jnp.broadcast_to(cap, (8, _E)))[0:1, :]
    a_i = lax.broadcasted_iota(jnp.int32, (ntp, ntp), 0)
    b_i = lax.broadcasted_iota(jnp.int32, (ntp, ntp), 1)
    earlier = jnp.where(b_i < a_i, 1.0, 0.0).astype(_bf16)
    goff = start + _dot_exact_lhs(earlier, r8)
    lo_ref[...] = lo.astype(jnp.int32)
    lo_t_ref[...] = lo.T
    goff_ref[...] = goff.astype(jnp.int32)
    nch_ref[...] = (r8 * (1.0 / _ALIGN)).astype(jnp.int32)
    trow = lax.broadcasted_iota(jnp.int32, (8, _E), 0)
    end = start + cap
    used_blocks = jnp.max(end, axis=-1, keepdims=True) * (1.0 / _BLK)
    tail = jnp.where(trow == 0, start + tot,
                     jnp.where(trow == 1, (cap - tot) * (1.0 / _ALIGN),
                               jnp.where(trow == 2, used_blocks, 0.0)))
    tail_ref[...] = tail.astype(jnp.int32)
    row0 = (lax.broadcasted_iota(jnp.int32, (nb, _E), 0) * _BLK).astype(_f32)
    blk_e = jnp.minimum(jnp.sum(jnp.where(end <= row0, 1.0, 0.0), axis=-1, keepdims=True), _E - 1.0)
    total = jnp.max(end, axis=-1, keepdims=True)
    used = jnp.where(row0[:, 0:1] < total, 1.0, 0.0)
    bl = lax.broadcasted_iota(jnp.int32, (nb, 128), 1)
    blk_ref[...] = jnp.where(bl == 0, blk_e, jnp.where(bl == 1, used, 0.0)).astype(jnp.int32)


def _offsets(cnt, nb):
    ntp = cnt.shape[0]
    i32 = jnp.int32
    return pl.pallas_call(
        _offsets_kernel,
        out_shape=[jax.ShapeDtypeStruct((ntp, _E), i32), jax.ShapeDtypeStruct((_E, ntp), _f32),
                   jax.ShapeDtypeStruct((ntp, _E), i32), jax.ShapeDtypeStruct((ntp, _E), i32),
                   jax.ShapeDtypeStruct((8, _E), i32), jax.ShapeDtypeStruct((nb, 128), i32)],
        name="route_offsets",
    )(cnt)


def _chunk_copy(src, src_row, dst, dst_row, sem):
    return pltpu.make_async_copy(src.at[pl.ds(src_row, _ALIGN)], dst.at[pl.ds(dst_row, _ALIGN)], sem)


def _for_each_chunk(nch_s, tile, fn):
    def per_expert(e, carry):
        def per_chunk(j, c):
            fn(e, j)
            return c
        return lax.fori_loop(0, nch_s[tile * _E + e], per_chunk, carry)
    lax.fori_loop(0, _E, per_expert, 0)


def _dispatch_kernel(lo_s, goff_s, nch_s, tail_s, hn_ref, route_t_ref, lo_t_ref, xs_ref,
                     buf, zrows, sem, *, nt):
    i = pl.program_id(0)
    tt = hn_ref.shape[0]
    cap = buf.shape[0]
    rt = route_t_ref[...]
    lo_t = lo_t_ref[...]
    tile_lane = lax.broadcasted_iota(jnp.int32, lo_t.shape, 1)
    lo_col = jnp.sum(jnp.where(tile_lane == i, lo_t, 0.0), axis=-1, keepdims=True)
    sub_e = lax.broadcasted_iota(jnp.int32, (_E, tt), 0).astype(_f32)
    r_iota = lax.broadcasted_iota(jnp.int32, (cap, tt), 0).astype(_f32)
    perm = jnp.zeros((cap, tt), _f32)
    for k in range(_K):
        idx = rt[k:k + 1, :]
        rank = rt[2 * _K + k:2 * _K + k + 1, :]
        pos = jnp.sum(jnp.where(sub_e == idx, lo_col, 0.0), axis=0, keepdims=True) + rank
        perm = jnp.where(r_iota == pos, 1.0, perm)
    sorted_rows = _dot(perm.astype(_bf16), hn_ref[...].astype(_bf16))

    def wait_one(e, j):
        _chunk_copy(buf, 0, xs_ref, 0, sem).wait()

    @pl.when(i > 0)
    def _():
        _for_each_chunk(nch_s, i - 1, wait_one)

    buf[...] = sorted_rows

    def start_one(e, j):
        src = pl.multiple_of(lo_s[i * _E + e] + j * _ALIGN, _ALIGN)
        dst = pl.multiple_of(goff_s[i * _E + e] + j * _ALIGN, _ALIGN)
        _chunk_copy(buf, src, xs_ref, dst, sem).start()

    _for_each_chunk(nch_s, i, start_one)

    @pl.when(i == nt - 1)
    def _():
        zrows[...] = jnp.zeros_like(zrows)

        def fill(e, carry):
            def one(j, c):
                dst = pl.multiple_of(tail_s[e] + j * _ALIGN, _ALIGN)
                _chunk_copy(zrows, 0, xs_ref, dst, sem).start()
                return c
            return lax.fori_loop(0, tail_s[_E + e], one, carry)

        lax.fori_loop(0, _E, fill, 0)

        def drain(e, carry):
            def one(j, c):
                _chunk_copy(zrows, 0, xs_ref, 0, sem).wait()
                return c
            return lax.fori_loop(0, tail_s[_E + e], one, carry)

        lax.fori_loop(0, _E, drain, 0)

        def block_copy(b):
            return pltpu.make_async_copy(zrows, xs_ref.at[pl.ds(pl.multiple_of(b * _BLK, _BLK), _BLK)], sem)

        def fill_block(b, carry):
            block_copy(b).start()
            return carry

        def drain_block(b, carry):
            block_copy(b).wait()
            return carry

        nb = xs_ref.shape[0] // _BLK
        lax.fori_loop(tail_s[2 * _E], nb, fill_block, 0)
        lax.fori_loop(tail_s[2 * _E], nb, drain_block, 0)
        _for_each_chunk(nch_s, i, wait_one)


def _dispatch(lo, goff, nch, tail, hn, route_t, lo_t, n_rows):
    s = hn.shape[0]
    tt = min(_TT, s)
    nt = s // tt
    cap = tt * _K + _E * _ALIGN
    ntp = lo_t.shape[1]
    grid_spec = pltpu.PrefetchScalarGridSpec(
        num_scalar_prefetch=4,
        grid=(nt,),
        in_specs=[
            pl.BlockSpec((tt, _D), lambda i, *_: (i, 0)),
            pl.BlockSpec((_RT_ROWS, tt), lambda i, *_: (0, i)),
            pl.BlockSpec((_E, ntp), lambda i, *_: (0, 0)),
        ],
        out_specs=pl.BlockSpec(memory_space=pl.ANY),
        scratch_shapes=[pltpu.VMEM((cap, _D), _f32), pltpu.VMEM((_BLK, _D), _f32),
                        pltpu.SemaphoreType.DMA(())],
    )
    return pl.pallas_call(
        functools.partial(_dispatch_kernel, nt=nt),
        grid_spec=grid_spec,
        out_shape=jax.ShapeDtypeStruct((n_rows, _D), _f32),
        compiler_params=pltpu.CompilerParams(dimension_semantics=("arbitrary",),
                                             vmem_limit_bytes=_VMEM_LIMIT),
        name="moe_dispatch",
    )(lo, goff, nch, tail, hn, route_t, lo_t)


def _expert_kernel(blk_e_ref, blk_ok_ref, last_ref, xs_ref, wgu_ref, bgu_ref, wd_ref, bd_ref, ys_ref,
                   wgu_c, wd_c):
    del last_ref
    i = pl.program_id(0)
    e = blk_e_ref[i]
    prev = blk_e_ref[jnp.maximum(i - 1, 0)]

    @pl.when((i == 0) | (e != prev))
    def _():
        step = 128
        for r in range(0, _D, step):
            wgu_c[r:r + step, :] = wgu_ref[0, 0, r:r + step, :].astype(_bf16)
        for r in range(0, _F, step):
            wd_c[r:r + step, :] = wd_ref[0, 0, r:r + step, :].astype(_bf16)

    @pl.when(blk_ok_ref[i] != 0)
    def _():
        x = xs_ref[...].astype(_bf16)
        gu = _dot(x, wgu_c[...]) + bgu_ref[0, 0]
        x_glu = jnp.minimum(gu[:, :_F], _SWIGLU_LIMIT)
        x_lin = jnp.clip(gu[:, _F:], -_SWIGLU_LIMIT, _SWIGLU_LIMIT)
        act = x_glu * jax.nn.sigmoid(_SWIGLU_ALPHA * x_glu) * (x_lin + 1.0)
        ys_ref[...] = _dot(act.astype(_bf16), wd_c[...]) + bd_ref[0, 0]

    @pl.when(blk_ok_ref[i] == 0)
    def _():
        ys_ref[...] = jnp.zeros_like(ys_ref)


def _experts(blk_e, blk_ok, last_blk, xs, w_gu, b_gu, w_down, b_down, layer):
    n_rows = xs.shape[0]
    nb = n_rows // _BLK
    row = lambda i, be, ok, last: (jnp.minimum(i, last[0]), 0)
    wsel = lambda i, be, ok, last: (layer, be[i], 0, 0)
    grid_spec = pltpu.PrefetchScalarGridSpec(
        num_scalar_prefetch=3,
        grid=(nb,),
        in_specs=[
            pl.BlockSpec((_BLK, _D), row),
            pl.BlockSpec((1, 1, _D, 2 * _F), wsel),
            pl.BlockSpec((1, 1, 1, 2 * _F), wsel),
            pl.BlockSpec((1, 1, _F, _D), wsel),
            pl.BlockSpec((1, 1, 1, _D), wsel),
        ],
        out_specs=pl.BlockSpec((_BLK, _D), lambda i, be, ok, last: (i, 0)),
        scratch_shapes=[pltpu.VMEM((_D, 2 * _F), _bf16), pltpu.VMEM((_F, _D), _bf16)],
    )
    depth = w_gu.shape[0]
    return pl.pallas_call(
        _expert_kernel,
        grid_spec=grid_spec,
        out_shape=jax.ShapeDtypeStruct((n_rows, _D), _f32),
        compiler_params=pltpu.CompilerParams(dimension_semantics=("arbitrary",),
                                             vmem_limit_bytes=_VMEM_LIMIT),
        name="moe_experts",
    )(blk_e, blk_ok, last_blk, xs, w_gu, b_gu.reshape(depth, _E, 1, 2 * _F), w_down,
      b_down.reshape(depth, _E, 1, _D))


def _combine_kernel(lo_s, goff_s, nch_s, route_ref, lo_ref, x1_ref, mod_ref, fin_ref, ys_ref,
                    out_ref, ybuf, sem, *, final):
    i = pl.program_id(0)
    tt = x1_ref.shape[0]
    cap = ybuf.shape[0]

    @pl.when(i == 0)
    def _():
        ybuf[...] = jnp.zeros_like(ybuf)

    def start_one(e, j):
        src = pl.multiple_of(goff_s[i * _E + e] + j * _ALIGN, _ALIGN)
        dst = pl.multiple_of(lo_s[i * _E + e] + j * _ALIGN, _ALIGN)
        _chunk_copy(ys_ref, src, ybuf, dst, sem).start()

    _for_each_chunk(nch_s, i, start_one)

    route = route_ref[...]
    lo_row = lo_ref[0]
    lane_e = lax.broadcasted_iota(jnp.int32, (tt, _E), 1).astype(_f32)
    lane_r = lax.broadcasted_iota(jnp.int32, (tt, cap), 1).astype(_f32)
    wmat = jnp.zeros((tt, cap), _f32)
    for k in range(_K):
        idx = route[:, k:k + 1]
        gate = route[:, _K + k:_K + k + 1]
        rank = route[:, 2 * _K + k:2 * _K + k + 1]
        pos = jnp.sum(jnp.where(lane_e == idx, lo_row, 0.0), axis=-1, keepdims=True) + rank
        wmat = jnp.where(lane_r == pos, gate, wmat)
    wmat = wmat.astype(_bf16)

    def wait_one(e, j):
        _chunk_copy(ys_ref, 0, ybuf, 0, sem).wait()

    _for_each_chunk(nch_s, i, wait_one)

    moe = _dot(wmat, ybuf[...].astype(_bf16))
    x2 = x1_ref[...] + mod_ref[5:6, :] * moe
    if final:
        x2 = _rms(x2) * fin_ref[...]
    out_ref[...] = x2


def _combine(lo, goff, nch, route, lo_f, x1, mod, final_norm, ys, final):
    s = x1.shape[0]
    tt = min(_TT, s)
    nt = s // tt
    cap = tt * _K + _E * _ALIGN
    grid_spec = pltpu.PrefetchScalarGridSpec(
        num_scalar_prefetch=3,
        grid=(nt,),
        in_specs=[
            pl.BlockSpec((tt, 128), lambda i, *_: (i, 0)),
            pl.BlockSpec((1, 1, _E), lambda i, *_: (i, 0, 0)),
            pl.BlockSpec((tt, _D), lambda i, *_: (i, 0)),
            pl.BlockSpec((6, _D), lambda i, *_: (0, 0)),
            pl.BlockSpec((1, _D), lambda i, *_: (0, 0)),
            pl.BlockSpec(memory_space=pl.ANY),
        ],
        out_specs=pl.BlockSpec((tt, _D), lambda i, *_: (i, 0)),
        scratch_shapes=[pltpu.VMEM((cap, _D), _f32), pltpu.SemaphoreType.DMA(())],
    )
    return pl.pallas_call(
        functools.partial(_combine_kernel, final=final),
        grid_spec=grid_spec,
        out_shape=jax.ShapeDtypeStruct((s, _D), _f32),
        compiler_params=pltpu.CompilerParams(dimension_semantics=("arbitrary",),
                                             vmem_limit_bytes=_VMEM_LIMIT),
        name="moe_combine",
    )(lo, goff, nch, route, lo_f, x1, mod, final_norm.reshape(1, _D), ys)


def _moe(x1, hn2, logits, mod, final_norm, w_gu, b_gu, w_down, b_down, layer, final):
    s = x1.shape[0]
    tt = min(_TT, s)
    nt = s // tt
    ntp = -(-nt // 8) * 8
    n_rows = -(-(s * _K + (_ALIGN - 1) * nt * _E) // _BLK) * _BLK + _E * _BLK
    nb = n_rows // _BLK
    route, route_t, cnt = _route(logits)
    cnt = jnp.pad(cnt.reshape(nt, _E), ((0, ntp - nt), (0, 0)))
    lo, lo_t, goff, nch, tail, blk = _offsets(cnt, nb)
    lo_s, goff_s, nch_s = (a[:nt].reshape(nt * _E) for a in (lo, goff, nch))
    xs = _dispatch(lo_s, goff_s, nch_s, tail[:3].reshape(3 * _E), hn2, route_t, lo_t, n_rows)
    blk_ok = blk[:, 1]
    last_blk = jnp.maximum(jnp.sum(blk_ok, keepdims=True) - 1, 0)
    ys = _experts(blk[:, 0], blk_ok, last_blk, xs, w_gu, b_gu, w_down, b_down, layer)
    lo_f = lo[:nt].astype(_f32).reshape(nt, 1, _E)
    return _combine(lo_s, goff_s, nch_s, route, lo_f, x1, mod, final_norm, ys, final)


def _rotary_tables(s):
    half = _DH // 2
    inv = 10000.0 ** (-jnp.arange(half, dtype=_f32) / half)
    ang = jnp.arange(s, dtype=_f32)[:, None] * inv[None, :]
    cos, sin = jnp.cos(ang), jnp.sin(ang)
    return jnp.concatenate([cos, cos], axis=1), jnp.concatenate([-sin, sin], axis=1)


def _lower_bounds_kernel(p_ref, o_ref):
    p = p_ref[...]
    n = p.shape[0]
    m = p[0:1, :]
    for j in range(1, n):
        m = jnp.maximum(m, p[j:j + 1, :])
    ex = [jnp.exp(p[j:j + 1, :] - m) for j in range(n)]
    den = ex[0]
    for j in range(1, n):
        den = den + ex[j]
    acc = jnp.zeros_like(den)
    rows = []
    for j in range(n):
        acc = acc + ex[j] / den
        rows.append(acc)
    o_ref[...] = jnp.concatenate(rows, axis=0)


def _lower_bounds(p):
    return pl.pallas_call(
        _lower_bounds_kernel,
        out_shape=jax.ShapeDtypeStruct(p.shape, _f32),
        name="hgrn2_lower_bounds",
    )(p)


def kernel(x, c, w_ada, b_ada, w_in, hg_lower_bounds, hg_norm, ret_norm, w_out, pool_w, pool_b,
           pool_scale, w_router, b_router, w_gu, b_gu, w_down, b_down, final_norm):
    b, s, d = x.shape
    assert b == 1 and d == _D
    depth = w_ada.shape[0]
    assert depth == 2
    xs = x.reshape(s, d)
    mod = _ada(c, w_ada, b_ada)
    lb_all = _lower_bounds(hg_lower_bounds)
    cosf, sinf = _rotary_tables(s)
    x1, hn2, logits = _mixer0(xs, mod[0], w_in[0], lb_all[0], hg_norm[0], ret_norm[0], w_out[0],
                              cosf, sinf, w_router[0], b_router[0])
    xs = _moe(x1, hn2, logits, mod[0], final_norm, w_gu, b_gu, w_down, b_down, 0, False)
    x1, hn2, logits = _pool_mixer(xs, mod[1], pool_w[0], pool_b[0], pool_scale[0], w_router[1],
                                  b_router[1])
    xs = _moe(x1, hn2, logits, mod[1], final_norm, w_gu, b_gu, w_down, b_down, 1, True)
    return xs.reshape(b, s, d)
```

```python
import functools
import math

import jax
import jax.numpy as jnp
from jax import lax
from jax.experimental import pallas as pl
from jax.experimental.pallas import tpu as pltpu

_D = 1024
_E = 32
_K = 4
_F = 1024
_HEADS = 4
_DH = 128
_CH = 64
_SUB = 16
_EPS = 1e-6
_BLK = 256
_TB = 256
_TT = 512
_ALIGN = 8
_RT_ROWS = 16
_NEG = -1e30
_SWIGLU_LIMIT = 7.0
_SWIGLU_ALPHA = 1.702
_POOL_WINDOWS = (2, 4, 8, 16)
_HALO = 16
_VMEM_LIMIT = 56 * 1024 * 1024

_f32 = jnp.float32
_bf16 = jnp.bfloat16


def _dot(a, b):
    return jnp.dot(a, b, preferred_element_type=_f32)


def _dot_nt(a, b):
    return lax.dot_general(a, b, (((1,), (1,)), ((), ())), preferred_element_type=_f32)


def _split3(a):
    a0 = a.astype(_bf16)
    r1 = a - a0.astype(_f32)
    a1 = r1.astype(_bf16)
    a2 = (r1 - a1.astype(_f32)).astype(_bf16)
    return a0, a1, a2


def _dot_exact_lhs(m, a):
    a0, a1, a2 = _split3(a)
    return _dot(m, a0) + _dot(m, a1) + _dot(m, a2)


def _dot_split(a, w):
    a0, a1, _ = _split3(a)
    w0, w1, _ = _split3(w)
    return _dot(a0, w0) + _dot(a1, w0) + _dot(a0, w1)


def _rms(x):
    return x * lax.rsqrt(jnp.mean(x * x, axis=-1, keepdims=True) + _EPS)


def _silu(x):
    return x * jax.nn.sigmoid(x)


def _ada_kernel(c_ref, w_ref, b_ref, o_ref):
    c = c_ref[...]
    cond = _silu(c)
    o_ref[0] = jnp.sum(w_ref[0] * cond, axis=0, keepdims=True) + b_ref[0]


def _ada(c, w_ada, b_ada):
    depth = w_ada.shape[0]
    n = depth * 6
    out = pl.pallas_call(
        _ada_kernel,
        grid=(n,),
        in_specs=[
            pl.BlockSpec((_D, 1), lambda j: (0, 0)),
            pl.BlockSpec((1, _D, _D), lambda j: (j // 6, 0, j % 6)),
            pl.BlockSpec((1, 1, _D), lambda j: (j, 0, 0)),
        ],
        out_specs=pl.BlockSpec((1, 1, _D), lambda j: (j, 0, 0)),
        out_shape=jax.ShapeDtypeStruct((n, 1, _D), _f32),
        name="ada_mod",
    )(c.reshape(_D, 1), w_ada, b_ada.reshape(n, 1, _D))
    return out.reshape(depth, 6, _D)


def _ffn_prologue(x1, mod, wr_ref, br_ref, x1_ref, hn2_ref, lg_ref):
    sh_f, sc_f = mod[3:4, :], mod[4:5, :]
    hn2 = _rms(x1) * (1.0 + sc_f) + sh_f
    x1_ref[...] = x1
    hn2_ref[...] = hn2.astype(_bf16)
    lg_ref[...] = _dot_split(hn2, wr_ref[...]) + br_ref[...]


def _mixer0_kernel(x_ref, mod_ref, win_ref, lb_ref, hgg_ref, rtg_ref, wout_ref, cos_ref, sin_ref,
                   wr_ref, br_ref, x1_ref, hn2_ref, lg_ref,
                   q_s, k_s, v_s, b_s, o_s, st_hg, st_rt):
    tb = x_ref.shape[0]
    nch = tb // _CH
    nsub = _CH // _SUB

    @pl.when(pl.program_id(0) == 0)
    def _():
        st_hg[...] = jnp.zeros_like(st_hg)
        st_rt[...] = jnp.zeros_like(st_rt)

    x = x_ref[...]
    mod = mod_ref[...]
    sh_m, sc_m, g_m = mod[0:1, :], mod[1:2, :], mod[2:3, :]
    hb = (_rms(x) * (1.0 + sc_m) + sh_m).astype(_bf16)
    width = _HEADS * _DH

    def sec(j):
        return _dot(hb, win_ref[:, j * width:(j + 1) * width])

    hq = sec(0)
    lb = lb_ref[...]
    fg = lb + (1.0 - lb) * jax.nn.sigmoid(sec(1))
    hi = sec(2)
    g = jnp.log(fg)
    r_i = lax.broadcasted_iota(jnp.int32, (tb, tb), 0)
    c_i = lax.broadcasted_iota(jnp.int32, (tb, tb), 1)
    tri = jnp.where(((r_i >> 6) == (c_i >> 6)) & (c_i <= r_i), 1.0, 0.0).astype(_bf16)
    bcum = _dot_exact_lhs(tri, g)
    qh = _silu(hq)
    kk = 1.0 - fg
    for h in range(_HEADS):
        sl = slice(h * _DH, (h + 1) * _DH)
        q_s[h] = qh[:, sl]
        k_s[h] = kk[:, sl]
        v_s[h] = hi[:, sl]
        b_s[h] = bcum[:, sl]

    hgg = hgg_ref[...]
    row64 = lax.broadcasted_iota(jnp.int32, (_HEADS, _CH, _DH), 1)
    row16 = lax.broadcasted_iota(jnp.int32, (_HEADS * nsub, _SUB, _DH), 1)

    def chunk(c, carry):
        rows = pl.ds(pl.multiple_of(c * _CH, _CH), _CH)
        q = q_s[:, rows, :]
        k = k_s[:, rows, :]
        v = v_s[:, rows, :]
        b = b_s[:, rows, :]
        st = st_hg[...]
        vb = v.astype(_bf16)
        qe = (q * jnp.exp(b)).astype(_bf16)
        o = jnp.einsum('htd,hed->hte', qe, st.astype(_bf16), preferred_element_type=_f32)
        blocks = [jnp.zeros((_HEADS, _SUB, _CH), _f32)]
        for i in range(1, nsub):
            ref = b[:, _SUB * i - 1:_SUB * i, :]
            qp = q[:, _SUB * i:_SUB * (i + 1), :] * jnp.exp(b[:, _SUB * i:_SUB * (i + 1), :] - ref)
            kp = k * jnp.exp(jnp.where(row64 < _SUB * i, ref - b, _NEG))
            blocks.append(jnp.einsum('htd,hsd->hts', qp.astype(_bf16), kp.astype(_bf16),
                                     preferred_element_type=_f32))
        a = jnp.concatenate(blocks, axis=1)
        o = o + jnp.einsum('hts,hse->hte', a.astype(_bf16), vb, preferred_element_type=_f32)
        q3 = q.reshape(_HEADS * nsub, _SUB, _DH)
        k3 = k.reshape(_HEADS * nsub, _SUB, _DH)
        v3 = v.reshape(_HEADS * nsub, _SUB, _DH)
        b3 = b.reshape(_HEADS * nsub, _SUB, _DH)
        od = jnp.zeros((_HEADS * nsub, _SUB, _DH), _f32)
        for s in range(_SUB):
            e = jnp.exp(jnp.where(row16 >= s, b3 - b3[:, s:s + 1, :], _NEG))
            w = jnp.sum(q3 * e * k3[:, s:s + 1, :], axis=-1, keepdims=True)
            od = od + w * v3[:, s:s + 1, :]
        o = o + od.reshape(_HEADS, _CH, _DH)
        bl = b[:, _CH - 1:_CH, :]
        kd = (k * jnp.exp(bl - b)).astype(_bf16)
        dec = jnp.exp(bl)
        for h in range(_HEADS):
            st_hg[h] = st[h] * dec[h] + _dot(v[h].T.astype(_bf16), kd[h])
        o = o * lax.rsqrt(jnp.mean(o * o, axis=-1, keepdims=True) + _EPS) * hgg
        o_s[:, rows, :] = o
        return carry

    lax.fori_loop(0, nch, chunk, 0)
    hog = sec(3)
    cat_hg = jnp.concatenate([o_s[h] for h in range(_HEADS)], axis=1) * _silu(hog)

    rq = sec(4)
    rk = sec(5)
    rv = sec(6)
    cosf = cos_ref[...]
    sinf = sin_ref[...]
    t_i = lax.broadcasted_iota(jnp.int32, (tb, tb), 0)
    s_i = lax.broadcasted_iota(jnp.int32, (tb, tb), 1)
    rel = (t_i - s_i).astype(_f32)
    causal = t_i >= s_i
    tcol = lax.broadcasted_iota(jnp.int32, (tb, 1), 0).astype(_f32)
    rtg = rtg_ref[...]
    outs = []
    for h in range(_HEADS):
        sl = slice(h * _DH, (h + 1) * _DH)
        lg = math.log(1.0 - 2.0 ** (-5.0 - h))
        xq = rq[:, sl]
        xk = rk[:, sl]
        q = xq * cosf + pltpu.roll(xq, _DH // 2, 1) * sinf
        k = (xk * cosf + pltpu.roll(xk, _DH // 2, 1) * sinf) * (_DH ** -0.5)
        v = rv[:, sl]
        dmat = jnp.where(causal, jnp.exp(jnp.where(causal, rel, 0.0) * lg), 0.0)
        scores = _dot_nt(q.astype(_bf16), k.astype(_bf16)) * dmat
        o = _dot(scores.astype(_bf16), v.astype(_bf16))
        st = st_rt[h]
        qd = q * jnp.exp((tcol + 1.0) * lg)
        o = o + _dot_nt(qd.astype(_bf16), st.astype(_bf16))
        kd = k * jnp.exp((tb - 1.0 - tcol) * lg)
        st_rt[h] = st * math.exp(tb * lg) + _dot(v.T.astype(_bf16), kd.astype(_bf16))
        o = o * lax.rsqrt(jnp.mean(o * o, axis=-1, keepdims=True) + _EPS) * rtg[h]
        outs.append(o)
    rg = sec(7)
    cat_rt = jnp.concatenate(outs, axis=1) * _silu(rg)

    cat = jnp.concatenate([cat_hg, cat_rt], axis=1).astype(_bf16)
    mix = _dot(cat, wout_ref[...])
    x1 = x + g_m * mix
    _ffn_prologue(x1, mod, wr_ref, br_ref, x1_ref, hn2_ref, lg_ref)


def _mixer0(x, mod, w_in, lb, hg_gain, ret_gain, w_out, cosf, sinf, w_router, b_router):
    s = x.shape[0]
    tb = min(_TB, s)
    width = _HEADS * _DH
    full = lambda shape: pl.BlockSpec(shape, lambda i: (0,) * len(shape))
    rowblk = lambda w: pl.BlockSpec((tb, w), lambda i: (i, 0))
    return pl.pallas_call(
        _mixer0_kernel,
        grid=(s // tb,),
        in_specs=[
            rowblk(_D), full((6, _D)), full((_D, 8 * width)), full((1, width)),
            full((_HEADS, 1, _DH)), full((_HEADS, 1, _DH)), full((2 * width, _D)),
            rowblk(_DH), rowblk(_DH), full((_D, _E)), full((1, _E)),
        ],
        out_specs=[rowblk(_D), rowblk(_D), rowblk(_E)],
        out_shape=[jax.ShapeDtypeStruct((s, _D), _f32), jax.ShapeDtypeStruct((s, _D), _bf16),
                   jax.ShapeDtypeStruct((s, _E), _f32)],
        scratch_shapes=[pltpu.VMEM((_HEADS, tb, _DH), _f32)] * 5
        + [pltpu.VMEM((_HEADS, _DH, _DH), _f32)] * 2,
        compiler_params=pltpu.CompilerParams(dimension_semantics=("arbitrary",),
                                             vmem_limit_bytes=_VMEM_LIMIT),
        name="mixer_hgrn2_retention",
    )(x, mod, w_in.astype(_bf16), lb.reshape(1, width), hg_gain.reshape(_HEADS, 1, _DH),
      ret_gain.reshape(_HEADS, 1, _DH), w_out.astype(_bf16), cosf, sinf, w_router,
      b_router.reshape(1, _E))


def _pool_kernel(x_ref, mod_ref, pw_ref, pb_ref, ps_ref, wr_ref, br_ref,
                 x1_ref, hn2_ref, lg_ref, halo):
    tb = x_ref.shape[0]
    i = pl.program_id(0)

    @pl.when(i == 0)
    def _():
        halo[...] = jnp.zeros_like(halo)

    x = x_ref[...]
    mod = mod_ref[...]
    sh_m, sc_m, g_m = mod[0:1, :], mod[1:2, :], mod[2:3, :]
    hn = _rms(x) * (1.0 + sc_m) + sh_m
    ext = jnp.concatenate([halo[...], hn], axis=0)
    halo[...] = hn[tb - _HALO:, :]
    t = (i * tb + lax.broadcasted_iota(jnp.int32, (tb, 1), 0)).astype(_f32)
    gw = _D // len(_POOL_WINDOWS)
    ys = []
    for gi, w in enumerate(_POOL_WINDOWS):
        acc = ext[:, gi * gw:(gi + 1) * gw]
        shift = 1
        while shift < w:
            acc = acc + pltpu.roll(acc, shift, 0)
            shift *= 2
        win = acc[_HALO:, :]
        cnt = jnp.minimum(t + 1.0, float(w))
        p = win / cnt - hn[:, gi * gw:(gi + 1) * gw]
        ys.append(_dot(p.astype(_bf16), pw_ref[gi]) + pb_ref[gi])
    mix = jnp.concatenate(ys, axis=1) * ps_ref[...]
    x1 = x + g_m * mix
    _ffn_prologue(x1, mod, wr_ref, br_ref, x1_ref, hn2_ref, lg_ref)


def _pool_mixer(x, mod, pool_w, pool_b, pool_scale, w_router, b_router):
    s = x.shape[0]
    tb = min(_TB, s)
    ng = len(_POOL_WINDOWS)
    gw = _D // ng
    full = lambda shape: pl.BlockSpec(shape, lambda i: (0,) * len(shape))
    rowblk = lambda w: pl.BlockSpec((tb, w), lambda i: (i, 0))
    return pl.pallas_call(
        _pool_kernel,
        grid=(s // tb,),
        in_specs=[rowblk(_D), full((6, _D)), full((ng, gw, gw)), full((ng, 1, gw)), full((1, _D)),
                  full((_D, _E)), full((1, _E))],
        out_specs=[rowblk(_D), rowblk(_D), rowblk(_E)],
        out_shape=[jax.ShapeDtypeStruct((s, _D), _f32), jax.ShapeDtypeStruct((s, _D), _bf16),
                   jax.ShapeDtypeStruct((s, _E), _f32)],
        scratch_shapes=[pltpu.VMEM((_HALO, _D), _f32)],
        compiler_params=pltpu.CompilerParams(dimension_semantics=("arbitrary",),
                                             vmem_limit_bytes=_VMEM_LIMIT),
        name="mixer_pool",
    )(x, mod, pool_w.astype(_bf16), pool_b.reshape(ng, 1, gw), pool_scale.reshape(1, _D),
      w_router, b_router.reshape(1, _E))


def _route_kernel(lg_ref, route_ref, route_t_ref, cnt_ref):
    tt = lg_ref.shape[0]
    vals = lg_ref[...]
    lane = lax.broadcasted_iota(jnp.int32, (tt, _E), 1).astype(_f32)
    tops, idxs, hots = [], [], []
    for _ in range(_K):
        m = jnp.max(vals, axis=-1, keepdims=True)
        idx = jnp.min(jnp.where(vals == m, lane, float(_E)), axis=-1, keepdims=True)
        hot = lane == idx
        vals = jnp.where(hot, -jnp.inf, vals)
        tops.append(m)
        idxs.append(idx)
        hots.append(hot)
    ex = [jnp.exp(m - tops[0]) for m in tops]
    den = ex[0] + ex[1] + ex[2] + ex[3]
    member = jnp.zeros((tt, _E), _f32)
    for hot in hots:
        member = member + jnp.where(hot, 1.0, 0.0)
    r_i = lax.broadcasted_iota(jnp.int32, (tt, tt), 0)
    c_i = lax.broadcasted_iota(jnp.int32, (tt, tt), 1)
    below = jnp.where(c_i < r_i, 1.0, 0.0).astype(_bf16)
    cum = _dot(below, member.astype(_bf16))
    cnt_ref[0] = jnp.sum(member, axis=0, keepdims=True)
    out_lane = lax.broadcasted_iota(jnp.int32, (tt, 128), 1)
    out = jnp.zeros((tt, 128), _f32)
    for k in range(_K):
        rank = jnp.sum(jnp.where(hots[k], cum, 0.0), axis=-1, keepdims=True)
        out = jnp.where(out_lane == k, idxs[k], out)
        out = jnp.where(out_lane == _K + k, ex[k] / den, out)
        out = jnp.where(out_lane == 2 * _K + k, rank, out)
    route_ref[...] = out
    route_t_ref[...] = out.T[0:_RT_ROWS, :]


def _route(logits):
    s = logits.shape[0]
    tt = min(_TT, s)
    nt = s // tt
    return pl.pallas_call(
        _route_kernel,
        grid=(nt,),
        in_specs=[pl.BlockSpec((tt, _E), lambda i: (i, 0))],
        out_specs=[pl.BlockSpec((tt, 128), lambda i: (i, 0)),
                   pl.BlockSpec((_RT_ROWS, tt), lambda i: (0, i)),
                   pl.BlockSpec((1, 1, _E), lambda i: (i, 0, 0))],
        out_shape=[jax.ShapeDtypeStruct((s, 128), _f32),
                   jax.ShapeDtypeStruct((_RT_ROWS, s), _f32),
                   jax.ShapeDtypeStruct((nt, 1, _E), _f32)],
        compiler_params=pltpu.CompilerParams(dimension_semantics=("arbitrary",)),
        name="route_topk_rank",
    )(logits)


def _offsets_kernel(cnt_ref, lo_ref, lo_t_ref, goff_ref, nch_ref, tail_ref):
    cnt = cnt_ref[...]
    ntp = cnt.shape[0]
    r8 = jnp.floor((cnt + (_ALIGN - 1.0)) * (1.0 / _ALIGN)) * _ALIGN
    j_i = lax.broadcasted_iota(jnp.int32, (_E, _E), 0)
    e_i = lax.broadcasted_iota(jnp.int32, (_E, _E), 1)
    before = jnp.where(j_i < e_i, 1.0, 0.0).astype(_bf16)

    def times_before(a):
        a0, a1, a2 = _split3(a)
        return _dot(a0, before) + _dot(a1, before) + _dot(a2, before)

    lo = times_before(r8)
    tot = jnp.sum(r8, axis=0, keepdims=True)
    cap = jnp.floor((tot + (_BLK - 1.0)) * (1.0 / _BLK)) * _BLK
    start = times_before(jnp.broadcast_to(cap, (8, _E)))[0:1, :]
    a_i = lax.broadcasted_iota(jnp.int32, (ntp, ntp), 0)
    b_i = lax.broadcasted_iota(jnp.int32, (ntp, ntp), 1)
    earlier = jnp.where(b_i < a_i, 1.0, 0.0).astype(_bf16)
    goff = start + _dot_exact_lhs(earlier, r8)
    lo_ref[...] = lo.astype(jnp.int32)
    lo_t_ref[...] = lo.T
    goff_ref[...] = goff.astype(jnp.int32)
    nch_ref[...] = (r8 * (1.0 / _ALIGN)).astype(jnp.int32)
    trow = lax.broadcasted_iota(jnp.int32, (8, _E), 0)
    end = start + cap
    used_blocks = jnp.max(end, axis=-1, keepdims=True) * (1.0 / _BLK)
    tail = jnp.where(trow == 0, start + tot,
                     jnp.where(trow == 1, (cap - tot) * (1.0 / _ALIGN),
                               jnp.where(trow == 2, used_blocks,
                                         jnp.where(trow == 3, start * (1.0 / _BLK),
                                                   jnp.where(trow == 4, cap * (1.0 / _BLK), 0.0)))))
    tail_ref[...] = tail.astype(jnp.int32)


def _offsets(cnt):
    ntp = cnt.shape[0]
    i32 = jnp.int32
    return pl.pallas_call(
        _offsets_kernel,
        out_shape=[jax.ShapeDtypeStruct((ntp, _E), i32), jax.ShapeDtypeStruct((_E, ntp), _f32),
                   jax.ShapeDtypeStruct((ntp, _E), i32), jax.ShapeDtypeStruct((ntp, _E), i32),
                   jax.ShapeDtypeStruct((8, _E), i32)],
        name="route_offsets",
    )(cnt)


_HI_MASK = -65536


def _pack_pairs(x):
    half = x.shape[1] // 2
    bits = lax.bitcast_convert_type(x, jnp.int32)
    return (bits[:, :half] & _HI_MASK) | lax.shift_right_logical(bits[:, half:], 16)


def _unpack_pairs(p):
    hi = lax.bitcast_convert_type(p & _HI_MASK, _f32).astype(_bf16)
    lo = lax.bitcast_convert_type(lax.shift_left(p, 16), _f32).astype(_bf16)
    return hi, lo


def _piece_sizes(tt):
    sizes = []
    size = tt
    while size >= _ALIGN:
        sizes.append(size)
        size //= 2
    return tuple(sizes)


def _for_each_part(nch_s, tile, tt, fn):
    def per_expert(e, carry):
        rows = nch_s[tile * _E + e] * _ALIGN
        for size in _piece_sizes(tt):
            @pl.when((rows & size) != 0)
            def _(size=size):
                fn(e, rows & (-2 * size), size)
        return carry
    lax.fori_loop(0, _E, per_expert, 0)


def _part_copy(src, src_row, dst, dst_row, size, sem):
    return pltpu.make_async_copy(src.at[pl.ds(pl.multiple_of(src_row, _ALIGN), size)],
                                 dst.at[pl.ds(pl.multiple_of(dst_row, _ALIGN), size)], sem)


def _zero_unused_blocks(zblk, dst_ref, first, sem):
    nb = dst_ref.shape[0] // _BLK

    def copy(b):
        return pltpu.make_async_copy(zblk, dst_ref.at[pl.ds(pl.multiple_of(b * _BLK, _BLK), _BLK)], sem)

    def fill(b, carry):
        copy(b).start()
        return carry

    def drain(b, carry):
        copy(b).wait()
        return carry

    lax.fori_loop(first, nb, fill, 0)
    lax.fori_loop(first, nb, drain, 0)


def _dispatch_kernel(lo_s, goff_s, nch_s, tail_s, hn_ref, route_t_ref, lo_t_ref, xs_ref,
                     buf, zblk, sem, *, nt):
    i = pl.program_id(0)
    tt = hn_ref.shape[0]
    cap = buf.shape[0]
    rt = route_t_ref[...]
    lo_t = lo_t_ref[...]
    tile_lane = lax.broadcasted_iota(jnp.int32, lo_t.shape, 1)
    lo_col = jnp.sum(jnp.where(tile_lane == i, lo_t, 0.0), axis=-1, keepdims=True)
    sub_e = lax.broadcasted_iota(jnp.int32, (_E, tt), 0).astype(_f32)
    r_iota = lax.broadcasted_iota(jnp.int32, (cap, tt), 0).astype(_f32)
    perm = jnp.zeros((cap, tt), _f32)
    for k in range(_K):
        idx = rt[k:k + 1, :]
        rank = rt[2 * _K + k:2 * _K + k + 1, :]
        pos = jnp.sum(jnp.where(sub_e == idx, lo_col, 0.0), axis=0, keepdims=True) + rank
        perm = jnp.where(r_iota == pos, 1.0, perm)
    packed = _pack_pairs(_dot(perm.astype(_bf16), hn_ref[...]))

    def wait_part(e, off, size):
        _part_copy(buf, 0, xs_ref, 0, size, sem).wait()

    @pl.when(i > 0)
    def _():
        _for_each_part(nch_s, i - 1, tt, wait_part)

    buf[...] = packed

    def start_part(e, off, size):
        _part_copy(buf, lo_s[i * _E + e] + off, xs_ref, goff_s[i * _E + e] + off, size, sem).start()

    _for_each_part(nch_s, i, tt, start_part)

    @pl.when(i == nt - 1)
    def _():
        zblk[...] = jnp.zeros_like(zblk)

        def tail_copy(dst_row):
            return _part_copy(zblk, 0, xs_ref, dst_row, _ALIGN, sem)

        def fill(e, carry):
            def one(j, c):
                tail_copy(tail_s[e] + j * _ALIGN).start()
                return c
            return lax.fori_loop(0, tail_s[_E + e], one, carry)

        def drain(e, carry):
            def one(j, c):
                tail_copy(0).wait()
                return c
            return lax.fori_loop(0, tail_s[_E + e], one, carry)

        lax.fori_loop(0, _E, fill, 0)
        lax.fori_loop(0, _E, drain, 0)
        _zero_unused_blocks(zblk, xs_ref, tail_s[2 * _E], sem)
        _for_each_part(nch_s, i, tt, wait_part)


def _dispatch(lo, goff, nch, tail, hn, route_t, lo_t, n_rows):
    s = hn.shape[0]
    tt = min(_TT, s)
    nt = s // tt
    cap = tt * _K + _E * _ALIGN
    ntp = lo_t.shape[1]
    grid_spec = pltpu.PrefetchScalarGridSpec(
        num_scalar_prefetch=4,
        grid=(nt,),
        in_specs=[
            pl.BlockSpec((tt, _D), lambda i, *_: (i, 0)),
            pl.BlockSpec((_RT_ROWS, tt), lambda i, *_: (0, i)),
            pl.BlockSpec((_E, ntp), lambda i, *_: (0, 0)),
        ],
        out_specs=pl.BlockSpec(memory_space=pl.ANY),
        scratch_shapes=[pltpu.VMEM((cap, _D // 2), jnp.int32), pltpu.VMEM((_BLK, _D // 2), jnp.int32),
                        pltpu.SemaphoreType.DMA(())],
    )
    return pl.pallas_call(
        functools.partial(_dispatch_kernel, nt=nt),
        grid_spec=grid_spec,
        out_shape=jax.ShapeDtypeStruct((n_rows, _D // 2), jnp.int32),
        compiler_params=pltpu.CompilerParams(dimension_semantics=("arbitrary",),
                                             vmem_limit_bytes=_VMEM_LIMIT),
        name="moe_dispatch",
    )(lo, goff, nch, tail, hn, route_t, lo_t)


def _expert_kernel(tail_s, wgu_ref, bgu_ref, wd_ref, bd_ref, xs_ref, ys_ref,
                   wgu_c, wd_c, xin, yout, zblk, sem_in, sem_out):
    e = pl.program_id(0)
    used = tail_s[2 * _E]
    first = tail_s[3 * _E + e]
    nblk = tail_s[4 * _E + e]
    half = _D // 2

    def rows(g):
        return pl.ds(pl.multiple_of(g * _BLK, _BLK), _BLK)

    def in_copy(g, slot):
        return pltpu.make_async_copy(xs_ref.at[rows(g)], xin.at[slot], sem_in.at[slot])

    def out_copy(g, slot):
        return pltpu.make_async_copy(yout.at[slot], ys_ref.at[rows(g)], sem_out.at[slot])

    @pl.when((e == 0) & (used > 0))
    def _():
        in_copy(0, 0).start()

    @pl.when(nblk > 0)
    def _():
        step = 128
        for r in range(0, _D, step):
            wgu_c[r:r + step, :] = wgu_ref[0, 0, r:r + step, :].astype(_bf16)
        for r in range(0, _F, step):
            wd_c[r:r + step, :] = wd_ref[0, 0, r:r + step, :].astype(_bf16)

    def block(j, carry):
        g = first + j
        slot = lax.rem(g, 2)
        in_copy(g, slot).wait()

        @pl.when(g + 1 < used)
        def _():
            in_copy(g + 1, 1 - slot).start()

        x_hi, x_lo = _unpack_pairs(xin[slot])
        gu = _dot(x_hi, wgu_c[:half, :]) + _dot(x_lo, wgu_c[half:, :]) + bgu_ref[0, 0]
        x_glu = jnp.minimum(gu[:, :_F], _SWIGLU_LIMIT)
        x_lin = jnp.clip(gu[:, _F:], -_SWIGLU_LIMIT, _SWIGLU_LIMIT)
        act = x_glu * jax.nn.sigmoid(_SWIGLU_ALPHA * x_glu) * (x_lin + 1.0)
        y = _dot(act.astype(_bf16), wd_c[...]) + bd_ref[0, 0]

        @pl.when(g >= 2)
        def _():
            out_copy(g - 2, slot).wait()

        yout[slot] = _pack_pairs(y.astype(_bf16).astype(_f32))
        out_copy(g, slot).start()
        return carry

    lax.fori_loop(0, nblk, block, 0)

    @pl.when(e == _E - 1)
    def _():
        @pl.when(used >= 1)
        def _():
            out_copy(used - 1, lax.rem(used - 1, 2)).wait()

        @pl.when(used >= 2)
        def _():
            out_copy(used - 2, lax.rem(used, 2)).wait()

        zblk[...] = jnp.zeros_like(zblk)
        _zero_unused_blocks(zblk, ys_ref, used, sem_out.at[0])


def _experts(tail, xs, w_gu, b_gu, w_down, b_down, layer):
    n_rows = xs.shape[0]
    wsel = lambda e, *_: (layer, e, 0, 0)
    grid_spec = pltpu.PrefetchScalarGridSpec(
        num_scalar_prefetch=1,
        grid=(_E,),
        in_specs=[
            pl.BlockSpec((1, 1, _D, 2 * _F), wsel),
            pl.BlockSpec((1, 1, 1, 2 * _F), wsel),
            pl.BlockSpec((1, 1, _F, _D), wsel),
            pl.BlockSpec((1, 1, 1, _D), wsel),
            pl.BlockSpec(memory_space=pl.ANY),
        ],
        out_specs=pl.BlockSpec(memory_space=pl.ANY),
        scratch_shapes=[pltpu.VMEM((_D, 2 * _F), _bf16), pltpu.VMEM((_F, _D), _bf16),
                        pltpu.VMEM((2, _BLK, _D // 2), jnp.int32),
                        pltpu.VMEM((2, _BLK, _D // 2), jnp.int32),
                        pltpu.VMEM((_BLK, _D // 2), jnp.int32),
                        pltpu.SemaphoreType.DMA((2,)), pltpu.SemaphoreType.DMA((2,))],
    )
    depth = w_gu.shape[0]
    return pl.pallas_call(
        _expert_kernel,
        grid_spec=grid_spec,
        out_shape=jax.ShapeDtypeStruct((n_rows, _D // 2), jnp.int32),
        compiler_params=pltpu.CompilerParams(dimension_semantics=("arbitrary",),
                                             vmem_limit_bytes=_VMEM_LIMIT),
        name="moe_experts",
    )(tail, w_gu, b_gu.reshape(depth, _E, 1, 2 * _F), w_down, b_down.reshape(depth, _E, 1, _D), xs)


def _combine_kernel(lo_s, goff_s, nch_s, route_ref, lo_ref, x1_ref, mod_ref, fin_ref, ys_ref,
                    out_ref, ybuf, sem, *, final):
    i = pl.program_id(0)
    tt = x1_ref.shape[0]
    cap = ybuf.shape[0]

    @pl.when(i == 0)
    def _():
        ybuf[...] = jnp.zeros_like(ybuf)

    def start_part(e, off, size):
        _part_copy(ys_ref, goff_s[i * _E + e] + off, ybuf, lo_s[i * _E + e] + off, size, sem).start()

    _for_each_part(nch_s, i, tt, start_part)

    route = route_ref[...]
    lo_row = lo_ref[0]
    lane_e = lax.broadcasted_iota(jnp.int32, (tt, _E), 1).astype(_f32)
    lane_r = lax.broadcasted_iota(jnp.int32, (tt, cap), 1).astype(_f32)
    wmat = jnp.zeros((tt, cap), _f32)
    for k in range(_K):
        idx = route[:, k:k + 1]
        gate = route[:, _K + k:_K + k + 1]
        rank = route[:, 2 * _K + k:2 * _K + k + 1]
        pos = jnp.sum(jnp.where(lane_e == idx, lo_row, 0.0), axis=-1, keepdims=True) + rank
        wmat = jnp.where(lane_r == pos, gate, wmat)
    wmat = wmat.astype(_bf16)

    def wait_part(e, off, size):
        _part_copy(ys_ref, 0, ybuf, 0, size, sem).wait()

    _for_each_part(nch_s, i, tt, wait_part)

    y_hi, y_lo = _unpack_pairs(ybuf[...])
    moe = jnp.concatenate([_dot(wmat, y_hi), _dot(wmat, y_lo)], axis=1)
    x2 = x1_ref[...] + mod_ref[5:6, :] * moe
    if final:
        x2 = _rms(x2) * fin_ref[...]
    out_ref[...] = x2


def _combine(lo, goff, nch, route, lo_f, x1, mod, final_norm, ys, final):
    s = x1.shape[0]
    tt = min(_TT, s)
    nt = s // tt
    cap = tt * _K + _E * _ALIGN
    grid_spec = pltpu.PrefetchScalarGridSpec(
        num_scalar_prefetch=3,
        grid=(nt,),
        in_specs=[
            pl.BlockSpec((tt, 128), lambda i, *_: (i, 0)),
            pl.BlockSpec((1, 1, _E), lambda i, *_: (i, 0, 0)),
            pl.BlockSpec((tt, _D), lambda i, *_: (i, 0)),
            pl.BlockSpec((6, _D), lambda i, *_: (0, 0)),
            pl.BlockSpec((1, _D), lambda i, *_: (0, 0)),
            pl.BlockSpec(memory_space=pl.ANY),
        ],
        out_specs=pl.BlockSpec((tt, _D), lambda i, *_: (i, 0)),
        scratch_shapes=[pltpu.VMEM((cap, _D // 2), jnp.int32), pltpu.SemaphoreType.DMA(())],
    )
    return pl.pallas_call(
        functools.partial(_combine_kernel, final=final),
        grid_spec=grid_spec,
        out_shape=jax.ShapeDtypeStruct((s, _D), _f32),
        compiler_params=pltpu.CompilerParams(dimension_semantics=("arbitrary",),
                                             vmem_limit_bytes=_VMEM_LIMIT),
        name="moe_combine",
    )(lo, goff, nch, route, lo_f, x1, mod, final_norm.reshape(1, _D), ys)


def _moe(x1, hn2, logits, mod, final_norm, w_gu, b_gu, w_down, b_down, layer, final):
    s = x1.shape[0]
    tt = min(_TT, s)
    nt = s // tt
    ntp = -(-nt // 8) * 8
    n_rows = -(-(s * _K + (_ALIGN - 1) * nt * _E) // _BLK) * _BLK + _E * _BLK
    route, route_t, cnt = _route(logits)
    cnt = jnp.pad(cnt.reshape(nt, _E), ((0, ntp - nt), (0, 0)))
    lo, lo_t, goff, nch, tail = _offsets(cnt)
    lo_s, goff_s, nch_s = (a[:nt].reshape(nt * _E) for a in (lo, goff, nch))
    tail_s = tail[:5].reshape(5 * _E)
    xs = _dispatch(lo_s, goff_s, nch_s, tail_s, hn2, route_t, lo_t, n_rows)
    ys = _experts(tail_s, xs, w_gu, b_gu, w_down, b_down, layer)
    lo_f = lo[:nt].astype(_f32).reshape(nt, 1, _E)
    return _combine(lo_s, goff_s, nch_s, route, lo_f, x1, mod, final_norm, ys, final)


def _rotary_tables(s):
    half = _DH // 2
    inv = 10000.0 ** (-jnp.arange(half, dtype=_f32) / half)
    ang = jnp.arange(s, dtype=_f32)[:, None] * inv[None, :]
    cos, sin = jnp.cos(ang), jnp.sin(ang)
    return jnp.concatenate([cos, cos], axis=1), jnp.concatenate([-sin, sin], axis=1)


def _lower_bounds_kernel(p_ref, o_ref):
    p = p_ref[...]
    n = p.shape[0]
    m = p[0:1, :]
    for j in range(1, n):
        m = jnp.maximum(m, p[j:j + 1, :])
    ex = [jnp.exp(p[j:j + 1, :] - m) for j in range(n)]
    den = ex[0]
    for j in range(1, n):
        den = den + ex[j]
    acc = jnp.zeros_like(den)
    rows = []
    for j in range(n):
        acc = acc + ex[j] / den
        rows.append(acc)
    o_ref[...] = jnp.concatenate(rows, axis=0)


def _lower_bounds(p):
    return pl.pallas_call(
        _lower_bounds_kernel,
        out_shape=jax.ShapeDtypeStruct(p.shape, _f32),
        name="hgrn2_lower_bounds",
    )(p)


def kernel(x, c, w_ada, b_ada, w_in, hg_lower_bounds, hg_norm, ret_norm, w_out, pool_w, pool_b,
           pool_scale, w_router, b_router, w_gu, b_gu, w_down, b_down, final_norm):
    b, s, d = x.shape
    assert b == 1 and d == _D
    depth = w_ada.shape[0]
    assert depth == 2
    xs = x.reshape(s, d)
    mod = _ada(c, w_ada, b_ada)
    lb_all = _lower_bounds(hg_lower_bounds)
    cosf, sinf = _rotary_tables(s)
    x1, hn2, logits = _mixer0(xs, mod[0], w_in[0], lb_all[0], hg_norm[0], ret_norm[0], w_out[0],
                              cosf, sinf, w_router[0], b_router[0])
    xs = _moe(x1, hn2, logits, mod[0], final_norm, w_gu, b_gu, w_down, b_down, 0, False)
    x1, hn2, logits = _pool_mixer(xs, mod[1], pool_w[0], pool_b[0], pool_scale[0], w_router[1],
                                  b_router[1])
    xs = _moe(x1, hn2, logits, mod[1], final_norm, w_gu, b_gu, w_down, b_down, 1, True)
    return xs.reshape(b, s, d)
```

```python
import functools
import math

import jax
import jax.numpy as jnp
from jax import lax
from jax.experimental import pallas as pl
from jax.experimental.pallas import tpu as pltpu

_D = 1024
_E = 32
_K = 4
_F = 1024
_HEADS = 4
_DH = 128
_CH = 64
_SUB = 16
_EPS = 1e-6
_BLK = 256
_TB = 256
_TT = 512
_ALIGN = 8
_RT_ROWS = 16
_RING = 4
_NEG = -1e30
_SWIGLU_LIMIT = 7.0
_SWIGLU_ALPHA = 1.702
_POOL_WINDOWS = (2, 4, 8, 16)
_HALO = 16
_VMEM_LIMIT = 56 * 1024 * 1024

_f32 = jnp.float32
_bf16 = jnp.bfloat16


def _dot(a, b):
    return jnp.dot(a, b, preferred_element_type=_f32)


def _dot_nt(a, b):
    return lax.dot_general(a, b, (((1,), (1,)), ((), ())), preferred_element_type=_f32)


def _split3(a):
    a0 = a.astype(_bf16)
    r1 = a - a0.astype(_f32)
    a1 = r1.astype(_bf16)
    a2 = (r1 - a1.astype(_f32)).astype(_bf16)
    return a0, a1, a2


def _dot_exact_lhs(m, a):
    a0, a1, a2 = _split3(a)
    return _dot(m, a0) + _dot(m, a1) + _dot(m, a2)


def _dot_split(a, w):
    a0, a1, _ = _split3(a)
    w0, w1, _ = _split3(w)
    return _dot(a0, w0) + _dot(a1, w0) + _dot(a0, w1)


def _rms(x):
    return x * lax.rsqrt(jnp.mean(x * x, axis=-1, keepdims=True) + _EPS)


def _silu(x):
    return x * jax.nn.sigmoid(x)


def _ada_kernel(c_ref, w_ref, b_ref, o_ref):
    c = c_ref[...]
    cond = _silu(c)
    o_ref[0] = jnp.sum(w_ref[0] * cond, axis=0, keepdims=True) + b_ref[0]


def _ada(c, w_ada, b_ada):
    depth = w_ada.shape[0]
    n = depth * 6
    out = pl.pallas_call(
        _ada_kernel,
        grid=(n,),
        in_specs=[
            pl.BlockSpec((_D, 1), lambda j: (0, 0)),
            pl.BlockSpec((1, _D, _D), lambda j: (j // 6, 0, j % 6)),
            pl.BlockSpec((1, 1, _D), lambda j: (j, 0, 0)),
        ],
        out_specs=pl.BlockSpec((1, 1, _D), lambda j: (j, 0, 0)),
        out_shape=jax.ShapeDtypeStruct((n, 1, _D), _f32),
        name="ada_mod",
    )(c.reshape(_D, 1), w_ada, b_ada.reshape(n, 1, _D))
    return out.reshape(depth, 6, _D)


def _ffn_prologue(x1, mod, wr_ref, br_ref, x1_ref, hn2_ref, lg_ref):
    sh_f, sc_f = mod[3:4, :], mod[4:5, :]
    hn2 = _rms(x1) * (1.0 + sc_f) + sh_f
    x1_ref[...] = x1
    hn2_ref[...] = hn2.astype(_bf16)
    lg_ref[...] = _dot_split(hn2, wr_ref[...]) + br_ref[...]


def _mixer0_kernel(x_ref, mod_ref, win_ref, lb_ref, hgg_ref, rtg_ref, wout_ref, cos_ref, sin_ref,
                   wr_ref, br_ref, x1_ref, hn2_ref, lg_ref,
                   q_s, k_s, v_s, b_s, o_s, st_hg, st_rt):
    tb = x_ref.shape[0]
    nch = tb // _CH
    nsub = _CH // _SUB

    @pl.when(pl.program_id(0) == 0)
    def _():
        st_hg[...] = jnp.zeros_like(st_hg)
        st_rt[...] = jnp.zeros_like(st_rt)

    x = x_ref[...]
    mod = mod_ref[...]
    sh_m, sc_m, g_m = mod[0:1, :], mod[1:2, :], mod[2:3, :]
    hb = (_rms(x) * (1.0 + sc_m) + sh_m).astype(_bf16)
    width = _HEADS * _DH

    def sec(j):
        return _dot(hb, win_ref[:, j * width:(j + 1) * width])

    hq = sec(0)
    lb = lb_ref[...]
    fg = lb + (1.0 - lb) * jax.nn.sigmoid(sec(1))
    hi = sec(2)
    g = jnp.log(fg)
    r_i = lax.broadcasted_iota(jnp.int32, (tb, tb), 0)
    c_i = lax.broadcasted_iota(jnp.int32, (tb, tb), 1)
    tri = jnp.where(((r_i >> 6) == (c_i >> 6)) & (c_i <= r_i), 1.0, 0.0).astype(_bf16)
    bcum = _dot_exact_lhs(tri, g)
    qh = _silu(hq)
    kk = 1.0 - fg
    for h in range(_HEADS):
        sl = slice(h * _DH, (h + 1) * _DH)
        q_s[h] = qh[:, sl]
        k_s[h] = kk[:, sl]
        v_s[h] = hi[:, sl]
        b_s[h] = bcum[:, sl]

    hgg = hgg_ref[...]
    row64 = lax.broadcasted_iota(jnp.int32, (_HEADS, _CH, _DH), 1)
    row16 = lax.broadcasted_iota(jnp.int32, (_HEADS * nsub, _SUB, 1), 1)

    def chunk(c, carry):
        rows = pl.ds(pl.multiple_of(c * _CH, _CH), _CH)
        q = q_s[:, rows, :]
        k = k_s[:, rows, :]
        v = v_s[:, rows, :]
        b = b_s[:, rows, :]
        st = st_hg[...]
        vb = v.astype(_bf16)
        qe = (q * jnp.exp(b)).astype(_bf16)
        o = jnp.einsum('htd,hed->hte', qe, st.astype(_bf16), preferred_element_type=_f32)
        blocks = [jnp.zeros((_HEADS, _SUB, _CH), _f32)]
        for i in range(1, nsub):
            ref = b[:, _SUB * i - 1:_SUB * i, :]
            qp = q[:, _SUB * i:_SUB * (i + 1), :] * jnp.exp(b[:, _SUB * i:_SUB * (i + 1), :] - ref)
            kp = k * jnp.exp(jnp.where(row64 < _SUB * i, ref - b, _NEG))
            blocks.append(jnp.einsum('htd,hsd->hts', qp.astype(_bf16), kp.astype(_bf16),
                                     preferred_element_type=_f32))
        a = jnp.concatenate(blocks, axis=1)
        o = o + jnp.einsum('hts,hse->hte', a.astype(_bf16), vb, preferred_element_type=_f32)
        q3 = q.reshape(_HEADS * nsub, _SUB, _DH)
        k3 = k.reshape(_HEADS * nsub, _SUB, _DH)
        v3 = v.reshape(_HEADS * nsub, _SUB, _DH)
        b3 = b.reshape(_HEADS * nsub, _SUB, _DH)
        od = jnp.zeros((_HEADS * nsub, _SUB, _DH), _f32)
        for s in range(_SUB):
            e = jnp.exp(b3 - b3[:, s:s + 1, :])
            w = jnp.sum(q3 * e * k3[:, s:s + 1, :], axis=-1, keepdims=True)
            od = od + jnp.where(row16 >= s, w, 0.0) * v3[:, s:s + 1, :]
        o = o + od.reshape(_HEADS, _CH, _DH)
        bl = b[:, _CH - 1:_CH, :]
        kd = (k * jnp.exp(bl - b)).astype(_bf16)
        dec = jnp.exp(bl)
        for h in range(_HEADS):
            st_hg[h] = st[h] * dec[h] + _dot(v[h].T.astype(_bf16), kd[h])
        o = o * lax.rsqrt(jnp.mean(o * o, axis=-1, keepdims=True) + _EPS) * hgg
        o_s[:, rows, :] = o
        return carry

    lax.fori_loop(0, nch, chunk, 0)
    hog = sec(3)
    cat_hg = jnp.concatenate([o_s[h] for h in range(_HEADS)], axis=1) * _silu(hog)

    rq = sec(4)
    rk = sec(5)
    rv = sec(6)
    cosf = cos_ref[...]
    sinf = sin_ref[...]
    t_i = lax.broadcasted_iota(jnp.int32, (tb, tb), 0)
    s_i = lax.broadcasted_iota(jnp.int32, (tb, tb), 1)
    rel = (t_i - s_i).astype(_f32)
    causal = t_i >= s_i
    tcol = lax.broadcasted_iota(jnp.int32, (tb, 1), 0).astype(_f32)
    rtg = rtg_ref[...]
    outs = []
    for h in range(_HEADS):
        sl = slice(h * _DH, (h + 1) * _DH)
        lg = math.log(1.0 - 2.0 ** (-5.0 - h))
        xq = rq[:, sl]
        xk = rk[:, sl]
        q = xq * cosf + pltpu.roll(xq, _DH // 2, 1) * sinf
        k = (xk * cosf + pltpu.roll(xk, _DH // 2, 1) * sinf) * (_DH ** -0.5)
        v = rv[:, sl]
        dmat = jnp.where(causal, jnp.exp(jnp.where(causal, rel, 0.0) * lg), 0.0)
        scores = _dot_nt(q.astype(_bf16), k.astype(_bf16)) * dmat
        o = _dot(scores.astype(_bf16), v.astype(_bf16))
        st = st_rt[h]
        qd = q * jnp.exp((tcol + 1.0) * lg)
        o = o + _dot_nt(qd.astype(_bf16), st.astype(_bf16))
        kd = k * jnp.exp((tb - 1.0 - tcol) * lg)
        st_rt[h] = st * math.exp(tb * lg) + _dot(v.T.astype(_bf16), kd.astype(_bf16))
        o = o * lax.rsqrt(jnp.mean(o * o, axis=-1, keepdims=True) + _EPS) * rtg[h]
        outs.append(o)
    rg = sec(7)
    cat_rt = jnp.concatenate(outs, axis=1) * _silu(rg)

    cat = jnp.concatenate([cat_hg, cat_rt], axis=1).astype(_bf16)
    mix = _dot(cat, wout_ref[...])
    x1 = x + g_m * mix
    _ffn_prologue(x1, mod, wr_ref, br_ref, x1_ref, hn2_ref, lg_ref)


def _mixer0(x, mod, w_in, lb, hg_gain, ret_gain, w_out, cosf, sinf, w_router, b_router):
    s = x.shape[0]
    tb = min(_TB, s)
    width = _HEADS * _DH
    full = lambda shape: pl.BlockSpec(shape, lambda i: (0,) * len(shape))
    rowblk = lambda w: pl.BlockSpec((tb, w), lambda i: (i, 0))
    return pl.pallas_call(
        _mixer0_kernel,
        grid=(s // tb,),
        in_specs=[
            rowblk(_D), full((6, _D)), full((_D, 8 * width)), full((1, width)),
            full((_HEADS, 1, _DH)), full((_HEADS, 1, _DH)), full((2 * width, _D)),
            rowblk(_DH), rowblk(_DH), full((_D, _E)), full((1, _E)),
        ],
        out_specs=[rowblk(_D), rowblk(_D), rowblk(_E)],
        out_shape=[jax.ShapeDtypeStruct((s, _D), _f32), jax.ShapeDtypeStruct((s, _D), _bf16),
                   jax.ShapeDtypeStruct((s, _E), _f32)],
        scratch_shapes=[pltpu.VMEM((_HEADS, tb, _DH), _f32)] * 5
        + [pltpu.VMEM((_HEADS, _DH, _DH), _f32)] * 2,
        compiler_params=pltpu.CompilerParams(dimension_semantics=("arbitrary",),
                                             vmem_limit_bytes=_VMEM_LIMIT),
        name="mixer_hgrn2_retention",
    )(x, mod, w_in.astype(_bf16), lb.reshape(1, width), hg_gain.reshape(_HEADS, 1, _DH),
      ret_gain.reshape(_HEADS, 1, _DH), w_out.astype(_bf16), cosf, sinf, w_router,
      b_router.reshape(1, _E))


def _pool_kernel(x_ref, mod_ref, pw_ref, pb_ref, ps_ref, wr_ref, br_ref,
                 x1_ref, hn2_ref, lg_ref, halo):
    tb = x_ref.shape[0]
    i = pl.program_id(0)

    @pl.when(i == 0)
    def _():
        halo[...] = jnp.zeros_like(halo)

    x = x_ref[...]
    mod = mod_ref[...]
    sh_m, sc_m, g_m = mod[0:1, :], mod[1:2, :], mod[2:3, :]
    hn = _rms(x) * (1.0 + sc_m) + sh_m
    ext = jnp.concatenate([halo[...], hn], axis=0)
    halo[...] = hn[tb - _HALO:, :]
    t = (i * tb + lax.broadcasted_iota(jnp.int32, (tb, 1), 0)).astype(_f32)
    gw = _D // len(_POOL_WINDOWS)
    ys = []
    for gi, w in enumerate(_POOL_WINDOWS):
        acc = ext[:, gi * gw:(gi + 1) * gw]
        shift = 1
        while shift < w:
            acc = acc + pltpu.roll(acc, shift, 0)
            shift *= 2
        win = acc[_HALO:, :]
        cnt = jnp.minimum(t + 1.0, float(w))
        p = win / cnt - hn[:, gi * gw:(gi + 1) * gw]
        ys.append(_dot(p.astype(_bf16), pw_ref[gi]) + pb_ref[gi])
    mix = jnp.concatenate(ys, axis=1) * ps_ref[...]
    x1 = x + g_m * mix
    _ffn_prologue(x1, mod, wr_ref, br_ref, x1_ref, hn2_ref, lg_ref)


def _pool_mixer(x, mod, pool_w, pool_b, pool_scale, w_router, b_router):
    s = x.shape[0]
    tb = min(_TB, s)
    ng = len(_POOL_WINDOWS)
    gw = _D // ng
    full = lambda shape: pl.BlockSpec(shape, lambda i: (0,) * len(shape))
    rowblk = lambda w: pl.BlockSpec((tb, w), lambda i: (i, 0))
    return pl.pallas_call(
        _pool_kernel,
        grid=(s // tb,),
        in_specs=[rowblk(_D), full((6, _D)), full((ng, gw, gw)), full((ng, 1, gw)), full((1, _D)),
                  full((_D, _E)), full((1, _E))],
        out_specs=[rowblk(_D), rowblk(_D), rowblk(_E)],
        out_shape=[jax.ShapeDtypeStruct((s, _D), _f32), jax.ShapeDtypeStruct((s, _D), _bf16),
                   jax.ShapeDtypeStruct((s, _E), _f32)],
        scratch_shapes=[pltpu.VMEM((_HALO, _D), _f32)],
        compiler_params=pltpu.CompilerParams(dimension_semantics=("arbitrary",),
                                             vmem_limit_bytes=_VMEM_LIMIT),
        name="mixer_pool",
    )(x, mod, pool_w.astype(_bf16), pool_b.reshape(ng, 1, gw), pool_scale.reshape(1, _D),
      w_router, b_router.reshape(1, _E))


def _route_kernel(lg_ref, route_ref, route_t_ref, cnt_ref):
    tt = lg_ref.shape[0]
    vals = lg_ref[...]
    lane = lax.broadcasted_iota(jnp.int32, (tt, _E), 1).astype(_f32)
    tops, idxs, hots = [], [], []
    for _ in range(_K):
        m = jnp.max(vals, axis=-1, keepdims=True)
        idx = jnp.min(jnp.where(vals == m, lane, float(_E)), axis=-1, keepdims=True)
        hot = lane == idx
        vals = jnp.where(hot, -jnp.inf, vals)
        tops.append(m)
        idxs.append(idx)
        hots.append(hot)
    ex = [jnp.exp(m - tops[0]) for m in tops]
    den = ex[0] + ex[1] + ex[2] + ex[3]
    member = jnp.zeros((tt, _E), _f32)
    for hot in hots:
        member = member + jnp.where(hot, 1.0, 0.0)
    r_i = lax.broadcasted_iota(jnp.int32, (tt, tt), 0)
    c_i = lax.broadcasted_iota(jnp.int32, (tt, tt), 1)
    below = jnp.where(c_i < r_i, 1.0, 0.0).astype(_bf16)
    cum = _dot(below, member.astype(_bf16))
    cnt_ref[0] = jnp.sum(member, axis=0, keepdims=True)
    out_lane = lax.broadcasted_iota(jnp.int32, (tt, 128), 1)
    out = jnp.zeros((tt, 128), _f32)
    for k in range(_K):
        rank = jnp.sum(jnp.where(hots[k], cum, 0.0), axis=-1, keepdims=True)
        out = jnp.where(out_lane == k, idxs[k], out)
        out = jnp.where(out_lane == _K + k, ex[k] / den, out)
        out = jnp.where(out_lane == 2 * _K + k, rank, out)
    route_ref[...] = out
    route_t_ref[...] = out.T[0:_RT_ROWS, :]


def _route(logits):
    s = logits.shape[0]
    tt = min(_TT, s)
    nt = s // tt
    return pl.pallas_call(
        _route_kernel,
        grid=(nt,),
        in_specs=[pl.BlockSpec((tt, _E), lambda i: (i, 0))],
        out_specs=[pl.BlockSpec((tt, 128), lambda i: (i, 0)),
                   pl.BlockSpec((_RT_ROWS, tt), lambda i: (0, i)),
                   pl.BlockSpec((1, 1, _E), lambda i: (i, 0, 0))],
        out_shape=[jax.ShapeDtypeStruct((s, 128), _f32),
                   jax.ShapeDtypeStruct((_RT_ROWS, s), _f32),
                   jax.ShapeDtypeStruct((nt, 1, _E), _f32)],
        compiler_params=pltpu.CompilerParams(dimension_semantics=("arbitrary",)),
        name="route_topk_rank",
    )(logits)


def _offsets_kernel(cnt_ref, lo_ref, lo_t_ref, goff_ref, nch_ref, tail_ref):
    cnt = cnt_ref[...]
    ntp = cnt.shape[0]
    r8 = jnp.floor((cnt + (_ALIGN - 1.0)) * (1.0 / _ALIGN)) * _ALIGN
    j_i = lax.broadcasted_iota(jnp.int32, (_E, _E), 0)
    e_i = lax.broadcasted_iota(jnp.int32, (_E, _E), 1)
    before = jnp.where(j_i < e_i, 1.0, 0.0).astype(_bf16)

    def times_before(a):
        a0, a1, a2 = _split3(a)
        return _dot(a0, before) + _dot(a1, before) + _dot(a2, before)

    lo = times_before(r8)
    tot = jnp.sum(r8, axis=0, keepdims=True)
    cap = jnp.floor((tot + (_BLK - 1.0)) * (1.0 / _BLK)) * _BLK
    start = times_before(jnp.broadcast_to(cap, (8, _E)))[0:1, :]
    a_i = lax.broadcasted_iota(jnp.int32, (ntp, ntp), 0)
    b_i = lax.broadcasted_iota(jnp.int32, (ntp, ntp), 1)
    earlier = jnp.where(b_i < a_i, 1.0, 0.0).astype(_bf16)
    goff = start + _dot_exact_lhs(earlier, r8)
    lo_ref[...] = lo.astype(jnp.int32)
    lo_t_ref[...] = lo.T
    goff_ref[...] = goff.astype(jnp.int32)
    nch_ref[...] = (r8 * (1.0 / _ALIGN)).astype(jnp.int32)
    trow = lax.broadcasted_iota(jnp.int32, (8, _E), 0)
    end = start + cap
    used_blocks = jnp.max(end, axis=-1, keepdims=True) * (1.0 / _BLK)
    tail = jnp.where(trow == 0, start + tot,
                     jnp.where(trow == 1, (cap - tot) * (1.0 / _ALIGN),
                               jnp.where(trow == 2, used_blocks,
                                         jnp.where(trow == 3, start * (1.0 / _BLK),
                                                   jnp.where(trow == 4, cap * (1.0 / _BLK), 0.0)))))
    tail_ref[...] = tail.astype(jnp.int32)


def _offsets(cnt):
    ntp = cnt.shape[0]
    i32 = jnp.int32
    return pl.pallas_call(
        _offsets_kernel,
        out_shape=[jax.ShapeDtypeStruct((ntp, _E), i32), jax.ShapeDtypeStruct((_E, ntp), _f32),
                   jax.ShapeDtypeStruct((ntp, _E), i32), jax.ShapeDtypeStruct((ntp, _E), i32),
                   jax.ShapeDtypeStruct((8, _E), i32)],
        name="route_offsets",
    )(cnt)


_HI_MASK = -65536


def _pack_pairs(x):
    half = x.shape[1] // 2
    bits = lax.bitcast_convert_type(x, jnp.int32)
    return (bits[:, :half] & _HI_MASK) | lax.shift_right_logical(bits[:, half:], 16)


def _unpack_pairs(p):
    hi = lax.bitcast_convert_type(p & _HI_MASK, _f32).astype(_bf16)
    lo = lax.bitcast_convert_type(lax.shift_left(p, 16), _f32).astype(_bf16)
    return hi, lo


def _piece_sizes(tt):
    sizes = []
    size = tt
    while size >= _ALIGN:
        sizes.append(size)
        size //= 2
    return tuple(sizes)


def _for_each_part(nch_s, tile, tt, fn):
    def per_expert(e, carry):
        rows = nch_s[tile * _E + e] * _ALIGN
        for size in _piece_sizes(tt):
            @pl.when((rows & size) != 0)
            def _(size=size):
                fn(e, rows & (-2 * size), size)
        return carry
    lax.fori_loop(0, _E, per_expert, 0)


def _part_copy(src, src_row, dst, dst_row, size, sem):
    return pltpu.make_async_copy(src.at[pl.ds(pl.multiple_of(src_row, _ALIGN), size)],
                                 dst.at[pl.ds(pl.multiple_of(dst_row, _ALIGN), size)], sem)


def _zero_unused_blocks(zblk, dst_ref, first, sem):
    nb = dst_ref.shape[0] // _BLK

    def copy(b):
        return pltpu.make_async_copy(zblk, dst_ref.at[pl.ds(pl.multiple_of(b * _BLK, _BLK), _BLK)], sem)

    def fill(b, carry):
        copy(b).start()
        return carry

    def drain(b, carry):
        copy(b).wait()
        return carry

    lax.fori_loop(first, nb, fill, 0)
    lax.fori_loop(first, nb, drain, 0)


def _dispatch_kernel(lo_s, goff_s, nch_s, tail_s, hn_ref, route_t_ref, lo_t_ref, xs_ref,
                     buf, zblk, sem, *, nt):
    i = pl.program_id(0)
    tt = hn_ref.shape[0]
    cap = buf.shape[0]
    rt = route_t_ref[...]
    lo_t = lo_t_ref[...]
    tile_lane = lax.broadcasted_iota(jnp.int32, lo_t.shape, 1)
    lo_col = jnp.sum(jnp.where(tile_lane == i, lo_t, 0.0), axis=-1, keepdims=True)
    sub_e = lax.broadcasted_iota(jnp.int32, (_E, tt), 0).astype(_f32)
    r_iota = lax.broadcasted_iota(jnp.int32, (cap, tt), 0).astype(_f32)
    perm = jnp.zeros((cap, tt), _f32)
    for k in range(_K):
        idx = rt[k:k + 1, :]
        rank = rt[2 * _K + k:2 * _K + k + 1, :]
        pos = jnp.sum(jnp.where(sub_e == idx, lo_col, 0.0), axis=0, keepdims=True) + rank
        perm = jnp.where(r_iota == pos, 1.0, perm)
    packed = _pack_pairs(_dot(perm.astype(_bf16), hn_ref[...]))

    def wait_part(e, off, size):
        _part_copy(buf, 0, xs_ref, 0, size, sem).wait()

    @pl.when(i > 0)
    def _():
        _for_each_part(nch_s, i - 1, tt, wait_part)

    buf[...] = packed

    def start_part(e, off, size):
        _part_copy(buf, lo_s[i * _E + e] + off, xs_ref, goff_s[i * _E + e] + off, size, sem).start()

    _for_each_part(nch_s, i, tt, start_part)

    @pl.when(i == nt - 1)
    def _():
        zblk[...] = jnp.zeros_like(zblk)

        def tail_copy(dst_row):
            return _part_copy(zblk, 0, xs_ref, dst_row, _ALIGN, sem)

        def fill(e, carry):
            def one(j, c):
                tail_copy(tail_s[e] + j * _ALIGN).start()
                return c
            return lax.fori_loop(0, tail_s[_E + e], one, carry)

        def drain(e, carry):
            def one(j, c):
                tail_copy(0).wait()
                return c
            return lax.fori_loop(0, tail_s[_E + e], one, carry)

        lax.fori_loop(0, _E, fill, 0)
        lax.fori_loop(0, _E, drain, 0)
        _zero_unused_blocks(zblk, xs_ref, tail_s[2 * _E], sem)
        _for_each_part(nch_s, i, tt, wait_part)


def _dispatch(lo, goff, nch, tail, hn, route_t, lo_t, n_rows):
    s = hn.shape[0]
    tt = min(_TT, s)
    nt = s // tt
    cap = tt * _K + _E * _ALIGN
    ntp = lo_t.shape[1]
    grid_spec = pltpu.PrefetchScalarGridSpec(
        num_scalar_prefetch=4,
        grid=(nt,),
        in_specs=[
            pl.BlockSpec((tt, _D), lambda i, *_: (i, 0)),
            pl.BlockSpec((_RT_ROWS, tt), lambda i, *_: (0, i)),
            pl.BlockSpec((_E, ntp), lambda i, *_: (0, 0)),
        ],
        out_specs=pl.BlockSpec(memory_space=pl.ANY),
        scratch_shapes=[pltpu.VMEM((cap, _D // 2), jnp.int32), pltpu.VMEM((_BLK, _D // 2), jnp.int32),
                        pltpu.SemaphoreType.DMA(())],
    )
    return pl.pallas_call(
        functools.partial(_dispatch_kernel, nt=nt),
        grid_spec=grid_spec,
        out_shape=jax.ShapeDtypeStruct((n_rows, _D // 2), jnp.int32),
        compiler_params=pltpu.CompilerParams(dimension_semantics=("arbitrary",),
                                             vmem_limit_bytes=_VMEM_LIMIT),
        name="moe_dispatch",
    )(lo, goff, nch, tail, hn, route_t, lo_t)


def _expert_kernel(tail_s, wgu_ref, bgu_ref, wd_ref, bd_ref, xs_ref, ys_ref,
                   wgu_c, wd_c, xin, yout, zblk, sem_in, sem_out):
    e = pl.program_id(0)
    used = tail_s[2 * _E]
    first = tail_s[3 * _E + e]
    nblk = tail_s[4 * _E + e]
    half = _D // 2

    def rows(g):
        return pl.ds(pl.multiple_of(g * _BLK, _BLK), _BLK)

    def in_copy(g, slot):
        return pltpu.make_async_copy(xs_ref.at[rows(g)], xin.at[slot], sem_in.at[slot])

    def out_copy(g, slot):
        return pltpu.make_async_copy(yout.at[slot], ys_ref.at[rows(g)], sem_out.at[slot])

    def slot_of(g):
        return lax.rem(g, _RING)

    @pl.when(e == 0)
    def _():
        for g in range(2):
            @pl.when(g < used)
            def _(g=g):
                in_copy(g, g).start()

    @pl.when(nblk > 0)
    def _():
        step = 128
        for r in range(0, _D, step):
            wgu_c[r:r + step, :] = wgu_ref[0, 0, r:r + step, :].astype(_bf16)
        for r in range(0, _F, step):
            wd_c[r:r + step, :] = wd_ref[0, 0, r:r + step, :].astype(_bf16)

    def unit(g, n):
        for d in range(n):
            in_copy(g + d, slot_of(g + d)).wait()
        for d in range(2, n + 2):
            @pl.when(g + d < used)
            def _(d=d):
                in_copy(g + d, slot_of(g + d)).start()
        parts = [_unpack_pairs(xin[slot_of(g + d)]) for d in range(n)]
        x_hi = jnp.concatenate([p[0] for p in parts], axis=0)
        x_lo = jnp.concatenate([p[1] for p in parts], axis=0)
        gu = _dot(x_hi, wgu_c[:half, :]) + _dot(x_lo, wgu_c[half:, :]) + bgu_ref[0, 0]
        x_glu = jnp.minimum(gu[:, :_F], _SWIGLU_LIMIT)
        x_lin = jnp.clip(gu[:, _F:], -_SWIGLU_LIMIT, _SWIGLU_LIMIT)
        act = x_glu * jax.nn.sigmoid(_SWIGLU_ALPHA * x_glu) * (x_lin + 1.0)
        y = _dot(act.astype(_bf16), wd_c[...]) + bd_ref[0, 0]
        packed = _pack_pairs(y.astype(_bf16).astype(_f32))
        for d in range(n):
            @pl.when(g + d >= _RING)
            def _(d=d):
                out_copy(g + d - _RING, slot_of(g + d)).wait()
            yout[slot_of(g + d)] = packed[d * _BLK:(d + 1) * _BLK, :]
            out_copy(g + d, slot_of(g + d)).start()

    def pair(j, carry):
        unit(first + 2 * j, 2)
        return carry

    lax.fori_loop(0, nblk // 2, pair, 0)

    @pl.when(lax.rem(nblk, 2) == 1)
    def _():
        unit(first + nblk - 1, 1)

    @pl.when(e == _E - 1)
    def _():
        for d in range(1, _RING + 1):
            @pl.when(used >= d)
            def _(d=d):
                out_copy(used - d, slot_of(used - d)).wait()

        zblk[...] = jnp.zeros_like(zblk)
        _zero_unused_blocks(zblk, ys_ref, used, sem_out.at[0])


def _experts(tail, xs, w_gu, b_gu, w_down, b_down, layer):
    n_rows = xs.shape[0]
    wsel = lambda e, *_: (layer, e, 0, 0)
    grid_spec = pltpu.PrefetchScalarGridSpec(
        num_scalar_prefetch=1,
        grid=(_E,),
        in_specs=[
            pl.BlockSpec((1, 1, _D, 2 * _F), wsel),
            pl.BlockSpec((1, 1, 1, 2 * _F), wsel),
            pl.BlockSpec((1, 1, _F, _D), wsel),
            pl.BlockSpec((1, 1, 1, _D), wsel),
            pl.BlockSpec(memory_space=pl.ANY),
        ],
        out_specs=pl.BlockSpec(memory_space=pl.ANY),
        scratch_shapes=[pltpu.VMEM((_D, 2 * _F), _bf16), pltpu.VMEM((_F, _D), _bf16),
                        pltpu.VMEM((_RING, _BLK, _D // 2), jnp.int32),
                        pltpu.VMEM((_RING, _BLK, _D // 2), jnp.int32),
                        pltpu.VMEM((_BLK, _D // 2), jnp.int32),
                        pltpu.SemaphoreType.DMA((_RING,)), pltpu.SemaphoreType.DMA((_RING,))],
    )
    depth = w_gu.shape[0]
    return pl.pallas_call(
        _expert_kernel,
        grid_spec=grid_spec,
        out_shape=jax.ShapeDtypeStruct((n_rows, _D // 2), jnp.int32),
        compiler_params=pltpu.CompilerParams(dimension_semantics=("arbitrary",),
                                             vmem_limit_bytes=_VMEM_LIMIT),
        name="moe_experts",
    )(tail, w_gu, b_gu.reshape(depth, _E, 1, 2 * _F), w_down, b_down.reshape(depth, _E, 1, _D), xs)


def _combine_kernel(lo_s, goff_s, nch_s, route_ref, lo_ref, x1_ref, mod_ref, fin_ref, ys_ref,
                    out_ref, ybuf, sem, *, final):
    i = pl.program_id(0)
    tt = x1_ref.shape[0]
    cap = ybuf.shape[0]

    @pl.when(i == 0)
    def _():
        ybuf[...] = jnp.zeros_like(ybuf)

    def start_part(e, off, size):
        _part_copy(ys_ref, goff_s[i * _E + e] + off, ybuf, lo_s[i * _E + e] + off, size, sem).start()

    _for_each_part(nch_s, i, tt, start_part)

    route = route_ref[...]
    lo_row = lo_ref[0]
    lane_e = lax.broadcasted_iota(jnp.int32, (tt, _E), 1).astype(_f32)
    lane_r = lax.broadcasted_iota(jnp.int32, (tt, cap), 1).astype(_f32)
    wmat = jnp.zeros((tt, cap), _f32)
    for k in range(_K):
        idx = route[:, k:k + 1]
        gate = route[:, _K + k:_K + k + 1]
        rank = route[:, 2 * _K + k:2 * _K + k + 1]
        pos = jnp.sum(jnp.where(lane_e == idx, lo_row, 0.0), axis=-1, keepdims=True) + rank
        wmat = jnp.where(lane_r == pos, gate, wmat)
    wmat = wmat.astype(_bf16)

    def wait_part(e, off, size):
        _part_copy(ys_ref, 0, ybuf, 0, size, sem).wait()

    _for_each_part(nch_s, i, tt, wait_part)

    y_hi, y_lo = _unpack_pairs(ybuf[...])
    moe = jnp.concatenate([_dot(wmat, y_hi), _dot(wmat, y_lo)], axis=1)
    x2 = x1_ref[...] + mod_ref[5:6, :] * moe
    if final:
        x2 = _rms(x2) * fin_ref[...]
    out_ref[...] = x2


def _combine(lo, goff, nch, route, lo_f, x1, mod, final_norm, ys, final):
    s = x1.shape[0]
    tt = min(_TT, s)
    nt = s // tt
    cap = tt * _K + _E * _ALIGN
    grid_spec = pltpu.PrefetchScalarGridSpec(
        num_scalar_prefetch=3,
        grid=(nt,),
        in_specs=[
            pl.BlockSpec((tt, 128), lambda i, *_: (i, 0)),
            pl.BlockSpec((1, 1, _E), lambda i, *_: (i, 0, 0)),
            pl.BlockSpec((tt, _D), lambda i, *_: (i, 0)),
            pl.BlockSpec((6, _D), lambda i, *_: (0, 0)),
            pl.BlockSpec((1, _D), lambda i, *_: (0, 0)),
            pl.BlockSpec(memory_space=pl.ANY),
        ],
        out_specs=pl.BlockSpec((tt, _D), lambda i, *_: (i, 0)),
        scratch_shapes=[pltpu.VMEM((cap, _D // 2), jnp.int32), pltpu.SemaphoreType.DMA(())],
    )
    return pl.pallas_call(
        functools.partial(_combine_kernel, final=final),
        grid_spec=grid_spec,
        out_shape=jax.ShapeDtypeStruct((s, _D), _f32),
        compiler_params=pltpu.CompilerParams(dimension_semantics=("arbitrary",),
                                             vmem_limit_bytes=_VMEM_LIMIT),
        name="moe_combine",
    )(lo, goff, nch, route, lo_f, x1, mod, final_norm.reshape(1, _D), ys)


def _moe(x1, hn2, logits, mod, final_norm, w_gu, b_gu, w_down, b_down, layer, final):
    s = x1.shape[0]
    tt = min(_TT, s)
    nt = s // tt
    ntp = -(-nt // 8) * 8
    n_rows = -(-(s * _K + (_ALIGN - 1) * nt * _E) // _BLK) * _BLK + _E * _BLK
    route, route_t, cnt = _route(logits)
    cnt = jnp.pad(cnt.reshape(nt, _E), ((0, ntp - nt), (0, 0)))
    lo, lo_t, goff, nch, tail = _offsets(cnt)
    lo_s, goff_s, nch_s = (a[:nt].reshape(nt * _E) for a in (lo, goff, nch))
    tail_s = tail[:5].reshape(5 * _E)
    xs = _dispatch(lo_s, goff_s, nch_s, tail_s, hn2, route_t, lo_t, n_rows)
    ys = _experts(tail_s, xs, w_gu, b_gu, w_down, b_down, layer)
    lo_f = lo[:nt].astype(_f32).reshape(nt, 1, _E)
    return _combine(lo_s, goff_s, nch_s, route, lo_f, x1, mod, final_norm, ys, final)


def _rotary_tables(s):
    half = _DH // 2
    inv = 10000.0 ** (-jnp.arange(half, dtype=_f32) / half)
    ang = jnp.arange(s, dtype=_f32)[:, None] * inv[None, :]
    cos, sin = jnp.cos(ang), jnp.sin(ang)
    return jnp.concatenate([cos, cos], axis=1), jnp.concatenate([-sin, sin], axis=1)


def _lower_bounds_kernel(p_ref, o_ref):
    p = p_ref[...]
    n = p.shape[0]
    m = p[0:1, :]
    for j in range(1, n):
        m = jnp.maximum(m, p[j:j + 1, :])
    ex = [jnp.exp(p[j:j + 1, :] - m) for j in range(n)]
    den = ex[0]
    for j in range(1, n):
        den = den + ex[j]
    acc = jnp.zeros_like(den)
    rows = []
    for j in range(n):
        acc = acc + ex[j] / den
        rows.append(acc)
    o_ref[...] = jnp.concatenate(rows, axis=0)


def _lower_bounds(p):
    return pl.pallas_call(
        _lower_bounds_kernel,
        out_shape=jax.ShapeDtypeStruct(p.shape, _f32),
        name="hgrn2_lower_bounds",
    )(p)


def kernel(x, c, w_ada, b_ada, w_in, hg_lower_bounds, hg_norm, ret_norm, w_out, pool_w, pool_b,
           pool_scale, w_router, b_router, w_gu, b_gu, w_down, b_down, final_norm):
    b, s, d = x.shape
    assert b == 1 and d == _D
    depth = w_ada.shape[0]
    assert depth == 2
    xs = x.reshape(s, d)
    mod = _ada(c, w_ada, b_ada)
    lb_all = _lower_bounds(hg_lower_bounds)
    cosf, sinf = _rotary_tables(s)
    x1, hn2, logits = _mixer0(xs, mod[0], w_in[0], lb_all[0], hg_norm[0], ret_norm[0], w_out[0],
                              cosf, sinf, w_router[0], b_router[0])
    xs = _moe(x1, hn2, logits, mod[0], final_norm, w_gu, b_gu, w_down, b_down, 0, False)
    x1, hn2, logits = _pool_mixer(xs, mod[1], pool_w[0], pool_b[0], pool_scale[0], w_router[1],
                                  b_router[1])
    xs = _moe(x1, hn2, logits, mod[1], final_norm, w_gu, b_gu, w_down, b_down, 1, True)
    return xs.reshape(b, s, d)
```

```python
import functools
import math

import jax
import jax.numpy as jnp
from jax import lax
from jax.experimental import pallas as pl
from jax.experimental.pallas import tpu as pltpu

_D = 1024
_E = 32
_K = 4
_F = 1024
_HEADS = 4
_DH = 128
_CH = 64
_SUB = 16
_EPS = 1e-6
_BLK = 256
_TB = 256
_TT = 512
_ALIGN = 8
_RT_ROWS = 16
_BULK = 64
_SORT_CHUNKS = 3
_AHEAD = 4
_RING = 2 * _AHEAD
_NEG = -1e30
_SWIGLU_LIMIT = 7.0
_SWIGLU_ALPHA = 1.702
_POOL_WINDOWS = (2, 4, 8, 16)
_HALO = 16
_VMEM_LIMIT = 56 * 1024 * 1024

_f32 = jnp.float32
_bf16 = jnp.bfloat16


def _dot(a, b):
    return jnp.dot(a, b, preferred_element_type=_f32)


def _dot_nt(a, b):
    return lax.dot_general(a, b, (((1,), (1,)), ((), ())), preferred_element_type=_f32)


def _split3(a):
    a0 = a.astype(_bf16)
    r1 = a - a0.astype(_f32)
    a1 = r1.astype(_bf16)
    a2 = (r1 - a1.astype(_f32)).astype(_bf16)
    return a0, a1, a2


def _dot_exact_lhs(m, a):
    a0, a1, a2 = _split3(a)
    return _dot(m, a0) + _dot(m, a1) + _dot(m, a2)


def _dot_split(a, w):
    a0, a1, _ = _split3(a)
    w0, w1, _ = _split3(w)
    return _dot(a0, w0) + _dot(a1, w0) + _dot(a0, w1)


def _rms(x):
    return x * lax.rsqrt(jnp.mean(x * x, axis=-1, keepdims=True) + _EPS)


def _silu(x):
    return x * jax.nn.sigmoid(x)


def _ada_kernel(c_ref, w_ref, b_ref, o_ref):
    c = c_ref[...]
    cond = _silu(c)
    o_ref[0] = jnp.sum(w_ref[0] * cond, axis=0, keepdims=True) + b_ref[0]


def _ada(c, w_ada, b_ada):
    depth = w_ada.shape[0]
    n = depth * 6
    out = pl.pallas_call(
        _ada_kernel,
        grid=(n,),
        in_specs=[
            pl.BlockSpec((_D, 1), lambda j: (0, 0)),
            pl.BlockSpec((1, _D, _D), lambda j: (j // 6, 0, j % 6)),
            pl.BlockSpec((1, 1, _D), lambda j: (j, 0, 0)),
        ],
        out_specs=pl.BlockSpec((1, 1, _D), lambda j: (j, 0, 0)),
        out_shape=jax.ShapeDtypeStruct((n, 1, _D), _f32),
        name="ada_mod",
    )(c.reshape(_D, 1), w_ada, b_ada.reshape(n, 1, _D))
    return out.reshape(depth, 6, _D)


def _ffn_prologue(x1, mod, wr_ref, br_ref, x1_ref, hn2_ref, lg_ref):
    sh_f, sc_f = mod[3:4, :], mod[4:5, :]
    hn2 = _rms(x1) * (1.0 + sc_f) + sh_f
    x1_ref[...] = x1
    hn2_ref[...] = hn2.astype(_bf16)
    lg_ref[...] = _dot_split(hn2, wr_ref[...]) + br_ref[...]


def _mixer0_kernel(x_ref, mod_ref, win_ref, lb_ref, hgg_ref, rtg_ref, wout_ref, cos_ref, sin_ref,
                   wr_ref, br_ref, x1_ref, hn2_ref, lg_ref,
                   q_s, k_s, v_s, b_s, o_s, st_hg, st_rt):
    tb = x_ref.shape[0]
    nch = tb // _CH
    nsub = _CH // _SUB

    @pl.when(pl.program_id(0) == 0)
    def _():
        st_hg[...] = jnp.zeros_like(st_hg)
        st_rt[...] = jnp.zeros_like(st_rt)

    x = x_ref[...]
    mod = mod_ref[...]
    sh_m, sc_m, g_m = mod[0:1, :], mod[1:2, :], mod[2:3, :]
    hb = (_rms(x) * (1.0 + sc_m) + sh_m).astype(_bf16)
    width = _HEADS * _DH

    def sec(j):
        return _dot(hb, win_ref[:, j * width:(j + 1) * width])

    hq = sec(0)
    lb = lb_ref[...]
    fg = lb + (1.0 - lb) * jax.nn.sigmoid(sec(1))
    hi = sec(2)
    g = jnp.log(fg)
    r_i = lax.broadcasted_iota(jnp.int32, (tb, tb), 0)
    c_i = lax.broadcasted_iota(jnp.int32, (tb, tb), 1)
    tri = jnp.where(((r_i >> 6) == (c_i >> 6)) & (c_i <= r_i), 1.0, 0.0).astype(_bf16)
    bcum = _dot_exact_lhs(tri, g)
    qh = _silu(hq)
    kk = 1.0 - fg
    for h in range(_HEADS):
        sl = slice(h * _DH, (h + 1) * _DH)
        q_s[h] = qh[:, sl]
        k_s[h] = kk[:, sl]
        v_s[h] = hi[:, sl]
        b_s[h] = bcum[:, sl]

    hgg = hgg_ref[...]
    row64 = lax.broadcasted_iota(jnp.int32, (_HEADS, _CH, _DH), 1)
    row16 = lax.broadcasted_iota(jnp.int32, (_HEADS * nsub, _SUB, 1), 1)

    def chunk(c, carry):
        rows = pl.ds(pl.multiple_of(c * _CH, _CH), _CH)
        q = q_s[:, rows, :]
        k = k_s[:, rows, :]
        v = v_s[:, rows, :]
        b = b_s[:, rows, :]
        st = st_hg[...]
        vb = v.astype(_bf16)
        qe = (q * jnp.exp(b)).astype(_bf16)
        o = jnp.einsum('htd,hed->hte', qe, st.astype(_bf16), preferred_element_type=_f32)
        blocks = [jnp.zeros((_HEADS, _SUB, _CH), _f32)]
        for i in range(1, nsub):
            ref = b[:, _SUB * i - 1:_SUB * i, :]
            qp = q[:, _SUB * i:_SUB * (i + 1), :] * jnp.exp(b[:, _SUB * i:_SUB * (i + 1), :] - ref)
            kp = k * jnp.exp(jnp.where(row64 < _SUB * i, ref - b, _NEG))
            blocks.append(jnp.einsum('htd,hsd->hts', qp.astype(_bf16), kp.astype(_bf16),
                                     preferred_element_type=_f32))
        a = jnp.concatenate(blocks, axis=1)
        o = o + jnp.einsum('hts,hse->hte', a.astype(_bf16), vb, preferred_element_type=_f32)
        q3 = q.reshape(_HEADS * nsub, _SUB, _DH)
        k3 = k.reshape(_HEADS * nsub, _SUB, _DH)
        v3 = v.reshape(_HEADS * nsub, _SUB, _DH)
        b3 = b.reshape(_HEADS * nsub, _SUB, _DH)
        od = jnp.zeros((_HEADS * nsub, _SUB, _DH), _f32)
        for s in range(_SUB):
            e = jnp.exp(b3 - b3[:, s:s + 1, :])
            w = jnp.sum(q3 * e * k3[:, s:s + 1, :], axis=-1, keepdims=True)
            od = od + jnp.where(row16 >= s, w, 0.0) * v3[:, s:s + 1, :]
        o = o + od.reshape(_HEADS, _CH, _DH)
        bl = b[:, _CH - 1:_CH, :]
        kd = (k * jnp.exp(bl - b)).astype(_bf16)
        dec = jnp.exp(bl)
        for h in range(_HEADS):
            st_hg[h] = st[h] * dec[h] + _dot(v[h].T.astype(_bf16), kd[h])
        o = o * lax.rsqrt(jnp.mean(o * o, axis=-1, keepdims=True) + _EPS) * hgg
        o_s[:, rows, :] = o
        return carry

    lax.fori_loop(0, nch, chunk, 0)
    hog = sec(3)
    cat_hg = jnp.concatenate([o_s[h] for h in range(_HEADS)], axis=1) * _silu(hog)

    rq = sec(4)
    rk = sec(5)
    rv = sec(6)
    cosf = cos_ref[...]
    sinf = sin_ref[...]
    t_i = lax.broadcasted_iota(jnp.int32, (tb, tb), 0)
    s_i = lax.broadcasted_iota(jnp.int32, (tb, tb), 1)
    rel = (t_i - s_i).astype(_f32)
    causal = t_i >= s_i
    tcol = lax.broadcasted_iota(jnp.int32, (tb, 1), 0).astype(_f32)
    rtg = rtg_ref[...]
    outs = []
    for h in range(_HEADS):
        sl = slice(h * _DH, (h + 1) * _DH)
        lg = math.log(1.0 - 2.0 ** (-5.0 - h))
        xq = rq[:, sl]
        xk = rk[:, sl]
        q = xq * cosf + pltpu.roll(xq, _DH // 2, 1) * sinf
        k = (xk * cosf + pltpu.roll(xk, _DH // 2, 1) * sinf) * (_DH ** -0.5)
        v = rv[:, sl]
        dmat = jnp.where(causal, jnp.exp(jnp.where(causal, rel, 0.0) * lg), 0.0)
        scores = _dot_nt(q.astype(_bf16), k.astype(_bf16)) * dmat
        o = _dot(scores.astype(_bf16), v.astype(_bf16))
        st = st_rt[h]
        qd = q * jnp.exp((tcol + 1.0) * lg)
        o = o + _dot_nt(qd.astype(_bf16), st.astype(_bf16))
        kd = k * jnp.exp((tb - 1.0 - tcol) * lg)
        st_rt[h] = st * math.exp(tb * lg) + _dot(v.T.astype(_bf16), kd.astype(_bf16))
        o = o * lax.rsqrt(jnp.mean(o * o, axis=-1, keepdims=True) + _EPS) * rtg[h]
        outs.append(o)
    rg = sec(7)
    cat_rt = jnp.concatenate(outs, axis=1) * _silu(rg)

    cat = jnp.concatenate([cat_hg, cat_rt], axis=1).astype(_bf16)
    mix = _dot(cat, wout_ref[...])
    x1 = x + g_m * mix
    _ffn_prologue(x1, mod, wr_ref, br_ref, x1_ref, hn2_ref, lg_ref)


def _mixer0(x, mod, w_in, lb, hg_gain, ret_gain, w_out, cosf, sinf, w_router, b_router):
    s = x.shape[0]
    tb = min(_TB, s)
    width = _HEADS * _DH
    full = lambda shape: pl.BlockSpec(shape, lambda i: (0,) * len(shape))
    rowblk = lambda w: pl.BlockSpec((tb, w), lambda i: (i, 0))
    return pl.pallas_call(
        _mixer0_kernel,
        grid=(s // tb,),
        in_specs=[
            rowblk(_D), full((6, _D)), full((_D, 8 * width)), full((1, width)),
            full((_HEADS, 1, _DH)), full((_HEADS, 1, _DH)), full((2 * width, _D)),
            rowblk(_DH), rowblk(_DH), full((_D, _E)), full((1, _E)),
        ],
        out_specs=[rowblk(_D), rowblk(_D), rowblk(_E)],
        out_shape=[jax.ShapeDtypeStruct((s, _D), _f32), jax.ShapeDtypeStruct((s, _D), _bf16),
                   jax.ShapeDtypeStruct((s, _E), _f32)],
        scratch_shapes=[pltpu.VMEM((_HEADS, tb, _DH), _f32)] * 5
        + [pltpu.VMEM((_HEADS, _DH, _DH), _f32)] * 2,
        compiler_params=pltpu.CompilerParams(dimension_semantics=("arbitrary",),
                                             vmem_limit_bytes=_VMEM_LIMIT),
        name="mixer_hgrn2_retention",
    )(x, mod, w_in.astype(_bf16), lb.reshape(1, width), hg_gain.reshape(_HEADS, 1, _DH),
      ret_gain.reshape(_HEADS, 1, _DH), w_out.astype(_bf16), cosf, sinf, w_router,
      b_router.reshape(1, _E))


def _pool_kernel(x_ref, mod_ref, pw_ref, pb_ref, ps_ref, wr_ref, br_ref,
                 x1_ref, hn2_ref, lg_ref, halo):
    tb = x_ref.shape[0]
    i = pl.program_id(0)

    @pl.when(i == 0)
    def _():
        halo[...] = jnp.zeros_like(halo)

    x = x_ref[...]
    mod = mod_ref[...]
    sh_m, sc_m, g_m = mod[0:1, :], mod[1:2, :], mod[2:3, :]
    hn = _rms(x) * (1.0 + sc_m) + sh_m
    ext = jnp.concatenate([halo[...], hn], axis=0)
    halo[...] = hn[tb - _HALO:, :]
    t = (i * tb + lax.broadcasted_iota(jnp.int32, (tb, 1), 0)).astype(_f32)
    gw = _D // len(_POOL_WINDOWS)
    ys = []
    for gi, w in enumerate(_POOL_WINDOWS):
        acc = ext[:, gi * gw:(gi + 1) * gw]
        shift = 1
        while shift < w:
            acc = acc + pltpu.roll(acc, shift, 0)
            shift *= 2
        win = acc[_HALO:, :]
        cnt = jnp.minimum(t + 1.0, float(w))
        p = win / cnt - hn[:, gi * gw:(gi + 1) * gw]
        ys.append(_dot(p.astype(_bf16), pw_ref[gi]) + pb_ref[gi])
    mix = jnp.concatenate(ys, axis=1) * ps_ref[...]
    x1 = x + g_m * mix
    _ffn_prologue(x1, mod, wr_ref, br_ref, x1_ref, hn2_ref, lg_ref)


def _pool_mixer(x, mod, pool_w, pool_b, pool_scale, w_router, b_router):
    s = x.shape[0]
    tb = min(_TB, s)
    ng = len(_POOL_WINDOWS)
    gw = _D // ng
    full = lambda shape: pl.BlockSpec(shape, lambda i: (0,) * len(shape))
    rowblk = lambda w: pl.BlockSpec((tb, w), lambda i: (i, 0))
    return pl.pallas_call(
        _pool_kernel,
        grid=(s // tb,),
        in_specs=[rowblk(_D), full((6, _D)), full((ng, gw, gw)), full((ng, 1, gw)), full((1, _D)),
                  full((_D, _E)), full((1, _E))],
        out_specs=[rowblk(_D), rowblk(_D), rowblk(_E)],
        out_shape=[jax.ShapeDtypeStruct((s, _D), _f32), jax.ShapeDtypeStruct((s, _D), _bf16),
                   jax.ShapeDtypeStruct((s, _E), _f32)],
        scratch_shapes=[pltpu.VMEM((_HALO, _D), _f32)],
        compiler_params=pltpu.CompilerParams(dimension_semantics=("arbitrary",),
                                             vmem_limit_bytes=_VMEM_LIMIT),
        name="mixer_pool",
    )(x, mod, pool_w.astype(_bf16), pool_b.reshape(ng, 1, gw), pool_scale.reshape(1, _D),
      w_router, b_router.reshape(1, _E))


def _route_kernel(lg_ref, route_ref, route_t_ref, cnt_ref):
    tt = lg_ref.shape[0]
    vals = lg_ref[...]
    lane = lax.broadcasted_iota(jnp.int32, (tt, _E), 1).astype(_f32)
    tops, idxs, hots = [], [], []
    for _ in range(_K):
        m = jnp.max(vals, axis=-1, keepdims=True)
        idx = jnp.min(jnp.where(vals == m, lane, float(_E)), axis=-1, keepdims=True)
        hot = lane == idx
        vals = jnp.where(hot, -jnp.inf, vals)
        tops.append(m)
        idxs.append(idx)
        hots.append(hot)
    ex = [jnp.exp(m - tops[0]) for m in tops]
    den = ex[0] + ex[1] + ex[2] + ex[3]
    member = jnp.zeros((tt, _E), _f32)
    for hot in hots:
        member = member + jnp.where(hot, 1.0, 0.0)
    r_i = lax.broadcasted_iota(jnp.int32, (tt, tt), 0)
    c_i = lax.broadcasted_iota(jnp.int32, (tt, tt), 1)
    below = jnp.where(c_i < r_i, 1.0, 0.0).astype(_bf16)
    cum = _dot(below, member.astype(_bf16))
    cnt_ref[0] = jnp.sum(member, axis=0, keepdims=True)
    out_lane = lax.broadcasted_iota(jnp.int32, (tt, 128), 1)
    out = jnp.zeros((tt, 128), _f32)
    for k in range(_K):
        rank = jnp.sum(jnp.where(hots[k], cum, 0.0), axis=-1, keepdims=True)
        out = jnp.where(out_lane == k, idxs[k], out)
        out = jnp.where(out_lane == _K + k, ex[k] / den, out)
        out = jnp.where(out_lane == 2 * _K + k, rank, out)
    route_ref[...] = out
    route_t_ref[...] = out.T[0:_RT_ROWS, :]


def _route(logits):
    s = logits.shape[0]
    tt = min(_TT, s)
    nt = s // tt
    return pl.pallas_call(
        _route_kernel,
        grid=(nt,),
        in_specs=[pl.BlockSpec((tt, _E), lambda i: (i, 0))],
        out_specs=[pl.BlockSpec((tt, 128), lambda i: (i, 0)),
                   pl.BlockSpec((_RT_ROWS, tt), lambda i: (0, i)),
                   pl.BlockSpec((1, 1, _E), lambda i: (i, 0, 0))],
        out_shape=[jax.ShapeDtypeStruct((s, 128), _f32),
                   jax.ShapeDtypeStruct((_RT_ROWS, s), _f32),
                   jax.ShapeDtypeStruct((nt, 1, _E), _f32)],
        compiler_params=pltpu.CompilerParams(dimension_semantics=("arbitrary",)),
        name="route_topk_rank",
    )(logits)


def _offsets_kernel(cnt_ref, lo_ref, lo_t_ref, goff_ref, nch_ref, tail_ref, tsum_ref):
    cnt = cnt_ref[...]
    ntp = cnt.shape[0]
    r8 = jnp.floor((cnt + (_ALIGN - 1.0)) * (1.0 / _ALIGN)) * _ALIGN
    j_i = lax.broadcasted_iota(jnp.int32, (_E, _E), 0)
    e_i = lax.broadcasted_iota(jnp.int32, (_E, _E), 1)
    before = jnp.where(j_i < e_i, 1.0, 0.0).astype(_bf16)

    def times_before(a):
        a0, a1, a2 = _split3(a)
        return _dot(a0, before) + _dot(a1, before) + _dot(a2, before)

    lo = times_before(r8)
    tot = jnp.sum(r8, axis=0, keepdims=True)
    cap = jnp.floor((tot + (_BLK - 1.0)) * (1.0 / _BLK)) * _BLK
    start = times_before(jnp.broadcast_to(cap, (8, _E)))[0:1, :]
    a_i = lax.broadcasted_iota(jnp.int32, (ntp, ntp), 0)
    b_i = lax.broadcasted_iota(jnp.int32, (ntp, ntp), 1)
    earlier = jnp.where(b_i < a_i, 1.0, 0.0).astype(_bf16)
    goff = start + _dot_exact_lhs(earlier, r8)
    lo_ref[...] = lo.astype(jnp.int32)
    lo_t_ref[...] = lo.T
    goff_ref[...] = goff.astype(jnp.int32)
    nch_ref[...] = (r8 * (1.0 / _ALIGN)).astype(jnp.int32)
    trow = lax.broadcasted_iota(jnp.int32, (8, _E), 0)
    end = start + cap
    used_blocks = jnp.max(end, axis=-1, keepdims=True) * (1.0 / _BLK)
    tail = jnp.where(trow == 0, start + tot,
                     jnp.where(trow == 1, (cap - tot) * (1.0 / _ALIGN),
                               jnp.where(trow == 2, used_blocks,
                                         jnp.where(trow == 3, start * (1.0 / _BLK),
                                                   jnp.where(trow == 4, cap * (1.0 / _BLK), 0.0)))))
    tail_ref[...] = tail.astype(jnp.int32)
    tsum = jnp.sum(r8, axis=-1, keepdims=True)
    tsum_ref[...] = jnp.broadcast_to(tsum, (ntp, _E)).astype(jnp.int32)


def _offsets(cnt):
    ntp = cnt.shape[0]
    i32 = jnp.int32
    return pl.pallas_call(
        _offsets_kernel,
        out_shape=[jax.ShapeDtypeStruct((ntp, _E), i32), jax.ShapeDtypeStruct((_E, ntp), _f32),
                   jax.ShapeDtypeStruct((ntp, _E), i32), jax.ShapeDtypeStruct((ntp, _E), i32),
                   jax.ShapeDtypeStruct((8, _E), i32), jax.ShapeDtypeStruct((ntp, _E), i32)],
        name="route_offsets",
    )(cnt)


_HI_MASK = -65536


def _pack_pairs(x):
    half = x.shape[1] // 2
    bits = lax.bitcast_convert_type(x, jnp.int32)
    return (bits[:, :half] & _HI_MASK) | lax.shift_right_logical(bits[:, half:], 16)


def _unpack_pairs(p):
    hi = lax.bitcast_convert_type(p & _HI_MASK, _f32).astype(_bf16)
    lo = lax.bitcast_convert_type(lax.shift_left(p, 16), _f32).astype(_bf16)
    return hi, lo


def _for_each_part(nch_s, tile, fn):
    def per_expert(e, carry):
        rows = nch_s[tile * _E + e] * _ALIGN

        def bulk(j, c):
            fn(e, j * _BULK, _BULK)
            return c

        lax.fori_loop(0, lax.shift_right_logical(rows, _BULK.bit_length() - 1), bulk, 0)
        size = _BULK // 2
        while size >= _ALIGN:
            @pl.when((rows & size) != 0)
            def _(size=size):
                fn(e, rows & (-2 * size), size)
            size //= 2
        return carry
    lax.fori_loop(0, _E, per_expert, 0)


def _part_copy(src, src_row, dst, dst_row, size, sem):
    return pltpu.make_async_copy(src.at[pl.ds(pl.multiple_of(src_row, _ALIGN), size)],
                                 dst.at[pl.ds(pl.multiple_of(dst_row, _ALIGN), size)], sem)


def _wait_rows(total, limit, copy_of_size):
    size = _ALIGN
    while size * 2 <= limit:
        size *= 2
    while size >= _ALIGN:
        @pl.when((total & size) != 0)
        def _(size=size):
            copy_of_size(size).wait()
        size //= 2


def _zero_unused_blocks(zblk, dst_ref, first, sem):
    nb = dst_ref.shape[0] // _BLK

    def copy(b):
        return pltpu.make_async_copy(zblk, dst_ref.at[pl.ds(pl.multiple_of(b * _BLK, _BLK), _BLK)], sem)

    def fill(b, carry):
        copy(b).start()
        return carry

    def drain(b, carry):
        copy(b).wait()
        return carry

    lax.fori_loop(first, nb, fill, 0)
    lax.fori_loop(first, nb, drain, 0)


def _dispatch_kernel(lo_s, goff_s, nch_s, tail_s, tsum_s, hn_ref, route_t_ref, lo_t_ref, xs_ref,
                     buf, zblk, sem, *, nt):
    i = pl.program_id(0)
    tt = hn_ref.shape[0]
    cap = buf.shape[1]
    rt = route_t_ref[...]
    lo_t = lo_t_ref[...]
    tile_lane = lax.broadcasted_iota(jnp.int32, lo_t.shape, 1)
    lo_col = jnp.sum(jnp.where(tile_lane == i, lo_t, 0.0), axis=-1, keepdims=True)
    sub_e = lax.broadcasted_iota(jnp.int32, (_E, tt), 0).astype(_f32)
    pos = []
    for k in range(_K):
        idx = rt[k:k + 1, :]
        rank = rt[2 * _K + k:2 * _K + k + 1, :]
        pos.append(jnp.sum(jnp.where(sub_e == idx, lo_col, 0.0), axis=0, keepdims=True) + rank)

    slot = lax.rem(i, 2)
    tile_buf = buf.at[slot]
    fill_sem = sem.at[2]

    def wait_tile(tile, sl):
        _wait_rows(tsum_s[tile], cap,
                   lambda size: _part_copy(buf.at[sl], 0, xs_ref, 0, size, sem.at[sl]))

    @pl.when(i > 1)
    def _():
        wait_tile(i - 2, slot)

    rows = cap // _SORT_CHUNKS
    hn = hn_ref[...]
    for c in range(_SORT_CHUNKS):
        r_iota = (lax.broadcasted_iota(jnp.int32, (rows, tt), 0) + c * rows).astype(_f32)
        perm = jnp.zeros((rows, tt), _f32)
        for k in range(_K):
            perm = jnp.where(r_iota == pos[k], 1.0, perm)
        buf[slot, c * rows:(c + 1) * rows, :] = _pack_pairs(_dot(perm.astype(_bf16), hn))

    def start_part(e, off, size):
        _part_copy(tile_buf, lo_s[i * _E + e] + off, xs_ref, goff_s[i * _E + e] + off, size,
                   sem.at[slot]).start()

    _for_each_part(nch_s, i, start_part)

    @pl.when(i == nt - 1)
    def _():
        zblk[...] = jnp.zeros_like(zblk)

        def tail_copy(dst_row):
            return _part_copy(zblk, 0, xs_ref, dst_row, _ALIGN, fill_sem)

        def fill(e, carry):
            def one(j, c):
                tail_copy(tail_s[e] + j * _ALIGN).start()
                return c
            return lax.fori_loop(0, tail_s[_E + e], one, carry)

        def drain(e, carry):
            def one(j, c):
                tail_copy(0).wait()
                return c
            return lax.fori_loop(0, tail_s[_E + e], one, carry)

        lax.fori_loop(0, _E, fill, 0)
        lax.fori_loop(0, _E, drain, 0)
        _zero_unused_blocks(zblk, xs_ref, tail_s[2 * _E], fill_sem)
        if nt > 1:
            wait_tile(i - 1, 1 - slot)
        wait_tile(i, slot)


def _dispatch(lo, goff, nch, tail, tsum, hn, route_t, lo_t, n_rows):
    s = hn.shape[0]
    tt = min(_TT, s)
    nt = s // tt
    cap = tt * _K + _E * _ALIGN
    ntp = lo_t.shape[1]
    grid_spec = pltpu.PrefetchScalarGridSpec(
        num_scalar_prefetch=5,
        grid=(nt,),
        in_specs=[
            pl.BlockSpec((tt, _D), lambda i, *_: (i, 0)),
            pl.BlockSpec((_RT_ROWS, tt), lambda i, *_: (0, i)),
            pl.BlockSpec((_E, ntp), lambda i, *_: (0, 0)),
        ],
        out_specs=pl.BlockSpec(memory_space=pl.ANY),
        scratch_shapes=[pltpu.VMEM((2, cap, _D // 2), jnp.int32),
                        pltpu.VMEM((_BLK, _D // 2), jnp.int32), pltpu.SemaphoreType.DMA((3,))],
    )
    return pl.pallas_call(
        functools.partial(_dispatch_kernel, nt=nt),
        grid_spec=grid_spec,
        out_shape=jax.ShapeDtypeStruct((n_rows, _D // 2), jnp.int32),
        compiler_params=pltpu.CompilerParams(dimension_semantics=("arbitrary",),
                                             vmem_limit_bytes=_VMEM_LIMIT),
        name="moe_dispatch",
    )(lo, goff, nch, tail, tsum, hn, route_t, lo_t)


def _expert_kernel(tail_s, wgu_ref, bgu_ref, wd_ref, bd_ref, xs_ref, ys_ref,
                   wgu_c, wd_c, xin, yout, zblk, sem_in, sem_out):
    e = pl.program_id(0)
    used = tail_s[2 * _E]
    first = tail_s[3 * _E + e]
    nblk = tail_s[4 * _E + e]
    half = _D // 2

    def rows(g):
        return pl.ds(pl.multiple_of(g * _BLK, _BLK), _BLK)

    def in_copy(g, slot):
        return pltpu.make_async_copy(xs_ref.at[rows(g)], xin.at[slot], sem_in.at[slot])

    def out_copy(g, slot):
        return pltpu.make_async_copy(yout.at[slot], ys_ref.at[rows(g)], sem_out.at[slot])

    def slot_of(g):
        return lax.rem(g, _RING)

    @pl.when(e == 0)
    def _():
        for g in range(_AHEAD):
            @pl.when(g < used)
            def _(g=g):
                in_copy(g, g).start()

    @pl.when(nblk > 0)
    def _():
        step = 128
        for r in range(0, _D, step):
            wgu_c[r:r + step, :] = wgu_ref[0, 0, r:r + step, :].astype(_bf16)
        for r in range(0, _F, step):
            wd_c[r:r + step, :] = wd_ref[0, 0, r:r + step, :].astype(_bf16)

    def unit(g, n):
        for d in range(n):
            in_copy(g + d, slot_of(g + d)).wait()
        for d in range(_AHEAD, _AHEAD + n):
            @pl.when(g + d < used)
            def _(d=d):
                in_copy(g + d, slot_of(g + d)).start()
        parts = [_unpack_pairs(xin[slot_of(g + d)]) for d in range(n)]
        x_hi = jnp.concatenate([p[0] for p in parts], axis=0)
        x_lo = jnp.concatenate([p[1] for p in parts], axis=0)
        gu = _dot(x_hi, wgu_c[:half, :]) + _dot(x_lo, wgu_c[half:, :]) + bgu_ref[0, 0]
        x_glu = jnp.minimum(gu[:, :_F], _SWIGLU_LIMIT)
        x_lin = jnp.clip(gu[:, _F:], -_SWIGLU_LIMIT, _SWIGLU_LIMIT)
        act = x_glu * jax.nn.sigmoid(_SWIGLU_ALPHA * x_glu) * (x_lin + 1.0)
        y = _dot(act.astype(_bf16), wd_c[...]) + bd_ref[0, 0]
        packed = _pack_pairs(y.astype(_bf16).astype(_f32))
        for d in range(n):
            @pl.when(g + d >= _RING)
            def _(d=d):
                out_copy(g + d - _RING, slot_of(g + d)).wait()
            yout[slot_of(g + d)] = packed[d * _BLK:(d + 1) * _BLK, :]
            out_copy(g + d, slot_of(g + d)).start()

    def widest(j, carry):
        unit(first + _AHEAD * j, _AHEAD)
        return carry

    lax.fori_loop(0, nblk // _AHEAD, widest, 0)
    n = _AHEAD // 2
    while n >= 1:
        @pl.when((nblk & n) != 0)
        def _(n=n):
            unit(first + (nblk & (-2 * n)), n)
        n //= 2

    @pl.when(e == _E - 1)
    def _():
        for d in range(1, _RING + 1):
            @pl.when(used >= d)
            def _(d=d):
                out_copy(used - d, slot_of(used - d)).wait()

        zblk[...] = jnp.zeros_like(zblk)
        _zero_unused_blocks(zblk, ys_ref, used, sem_out.at[0])


def _experts(tail, xs, w_gu, b_gu, w_down, b_down, layer):
    n_rows = xs.shape[0]
    wsel = lambda e, *_: (layer, e, 0, 0)
    grid_spec = pltpu.PrefetchScalarGridSpec(
        num_scalar_prefetch=1,
        grid=(_E,),
        in_specs=[
            pl.BlockSpec((1, 1, _D, 2 * _F), wsel),
            pl.BlockSpec((1, 1, 1, 2 * _F), wsel),
            pl.BlockSpec((1, 1, _F, _D), wsel),
            pl.BlockSpec((1, 1, 1, _D), wsel),
            pl.BlockSpec(memory_space=pl.ANY),
        ],
        out_specs=pl.BlockSpec(memory_space=pl.ANY),
        scratch_shapes=[pltpu.VMEM((_D, 2 * _F), _bf16), pltpu.VMEM((_F, _D), _bf16),
                        pltpu.VMEM((_RING, _BLK, _D // 2), jnp.int32),
                        pltpu.VMEM((_RING, _BLK, _D // 2), jnp.int32),
                        pltpu.VMEM((_BLK, _D // 2), jnp.int32),
                        pltpu.SemaphoreType.DMA((_RING,)), pltpu.SemaphoreType.DMA((_RING,))],
    )
    depth = w_gu.shape[0]
    return pl.pallas_call(
        _expert_kernel,
        grid_spec=grid_spec,
        out_shape=jax.ShapeDtypeStruct((n_rows, _D // 2), jnp.int32),
        compiler_params=pltpu.CompilerParams(dimension_semantics=("arbitrary",),
                                             vmem_limit_bytes=_VMEM_LIMIT),
        name="moe_experts",
    )(tail, w_gu, b_gu.reshape(depth, _E, 1, 2 * _F), w_down, b_down.reshape(depth, _E, 1, _D), xs)


def _combine_kernel(lo_s, goff_s, nch_s, tsum_s, route_ref, lo_ref, x1_ref, mod_ref, fin_ref, ys_ref,
                    out_ref, ybuf, sem, *, final, nt):
    i = pl.program_id(0)
    tt = x1_ref.shape[0]
    cap = ybuf.shape[1]
    slot = lax.rem(i, 2)

    def fetch(tile, dst):
        def start_part(e, off, size):
            _part_copy(ys_ref, goff_s[tile * _E + e] + off, dst, lo_s[tile * _E + e] + off, size,
                       sem).start()
        _for_each_part(nch_s, tile, start_part)

    @pl.when(i == 0)
    def _():
        ybuf[...] = jnp.zeros_like(ybuf)
        fetch(0, ybuf.at[0])

    _wait_rows(tsum_s[i], cap, lambda size: _part_copy(ys_ref, 0, ybuf.at[slot], 0, size, sem))

    @pl.when(i + 1 < nt)
    def _():
        fetch(i + 1, ybuf.at[1 - slot])

    route = route_ref[...]
    lo_row = lo_ref[0]
    lane_e = lax.broadcasted_iota(jnp.int32, (tt, _E), 1).astype(_f32)
    pos, gate = [], []
    for k in range(_K):
        idx = route[:, k:k + 1]
        rank = route[:, 2 * _K + k:2 * _K + k + 1]
        pos.append(jnp.sum(jnp.where(lane_e == idx, lo_row, 0.0), axis=-1, keepdims=True) + rank)
        gate.append(route[:, _K + k:_K + k + 1])

    cols = cap // _SORT_CHUNKS
    moe_hi = jnp.zeros((tt, _D // 2), _f32)
    moe_lo = jnp.zeros((tt, _D // 2), _f32)
    for c in range(_SORT_CHUNKS):
        lane_r = (lax.broadcasted_iota(jnp.int32, (tt, cols), 1) + c * cols).astype(_f32)
        wmat = jnp.zeros((tt, cols), _f32)
        for k in range(_K):
            wmat = jnp.where(lane_r == pos[k], gate[k], wmat)
        wmat = wmat.astype(_bf16)
        y_hi, y_lo = _unpack_pairs(ybuf[slot, c * cols:(c + 1) * cols, :])
        moe_hi = moe_hi + _dot(wmat, y_hi)
        moe_lo = moe_lo + _dot(wmat, y_lo)
    moe = jnp.concatenate([moe_hi, moe_lo], axis=1)
    x2 = x1_ref[...] + mod_ref[5:6, :] * moe
    if final:
        x2 = _rms(x2) * fin_ref[...]
    out_ref[...] = x2


def _combine(lo, goff, nch, tsum, route, lo_f, x1, mod, final_norm, ys, final):
    s = x1.shape[0]
    tt = min(_TT, s)
    nt = s // tt
    cap = tt * _K + _E * _ALIGN
    grid_spec = pltpu.PrefetchScalarGridSpec(
        num_scalar_prefetch=4,
        grid=(nt,),
        in_specs=[
            pl.BlockSpec((tt, 128), lambda i, *_: (i, 0)),
            pl.BlockSpec((1, 1, _E), lambda i, *_: (i, 0, 0)),
            pl.BlockSpec((tt, _D), lambda i, *_: (i, 0)),
            pl.BlockSpec((6, _D), lambda i, *_: (0, 0)),
            pl.BlockSpec((1, _D), lambda i, *_: (0, 0)),
            pl.BlockSpec(memory_space=pl.ANY),
        ],
        out_specs=pl.BlockSpec((tt, _D), lambda i, *_: (i, 0)),
        scratch_shapes=[pltpu.VMEM((2, cap, _D // 2), jnp.int32), pltpu.SemaphoreType.DMA(())],
    )
    return pl.pallas_call(
        functools.partial(_combine_kernel, final=final, nt=nt),
        grid_spec=grid_spec,
        out_shape=jax.ShapeDtypeStruct((s, _D), _f32),
        compiler_params=pltpu.CompilerParams(dimension_semantics=("arbitrary",),
                                             vmem_limit_bytes=_VMEM_LIMIT),
        name="moe_combine",
    )(lo, goff, nch, tsum, route, lo_f, x1, mod, final_norm.reshape(1, _D), ys)


def _moe(x1, hn2, logits, mod, final_norm, w_gu, b_gu, w_down, b_down, layer, final):
    s = x1.shape[0]
    tt = min(_TT, s)
    nt = s // tt
    ntp = -(-nt // 8) * 8
    n_rows = -(-(s * _K + (_ALIGN - 1) * nt * _E) // _BLK) * _BLK + _E * _BLK
    route, route_t, cnt = _route(logits)
    cnt = jnp.pad(cnt.reshape(nt, _E), ((0, ntp - nt), (0, 0)))
    lo, lo_t, goff, nch, tail, tsum = _offsets(cnt)
    lo_s, goff_s, nch_s = (a[:nt].reshape(nt * _E) for a in (lo, goff, nch))
    tail_s = tail[:5].reshape(5 * _E)
    tsum_s = tsum[:nt, 0]
    xs = _dispatch(lo_s, goff_s, nch_s, tail_s, tsum_s, hn2, route_t, lo_t, n_rows)
    ys = _experts(tail_s, xs, w_gu, b_gu, w_down, b_down, layer)
    lo_f = lo[:nt].astype(_f32).reshape(nt, 1, _E)
    return _combine(lo_s, goff_s, nch_s, tsum_s, route, lo_f, x1, mod, final_norm, ys, final)


def _rotary_tables(s):
    half = _DH // 2
    inv = 10000.0 ** (-jnp.arange(half, dtype=_f32) / half)
    ang = jnp.arange(s, dtype=_f32)[:, None] * inv[None, :]
    cos, sin = jnp.cos(ang), jnp.sin(ang)
    return jnp.concatenate([cos, cos], axis=1), jnp.concatenate([-sin, sin], axis=1)


def _lower_bounds_kernel(p_ref, o_ref):
    p = p_ref[...]
    n = p.shape[0]
    m = p[0:1, :]
    for j in range(1, n):
        m = jnp.maximum(m, p[j:j + 1, :])
    ex = [jnp.exp(p[j:j + 1, :] - m) for j in range(n)]
    den = ex[0]
    for j in range(1, n):
        den = den + ex[j]
    acc = jnp.zeros_like(den)
    rows = []
    for j in range(n):
        acc = acc + ex[j] / den
        rows.append(acc)
    o_ref[...] = jnp.concatenate(rows, axis=0)


def _lower_bounds(p):
    return pl.pallas_call(
        _lower_bounds_kernel,
        out_shape=jax.ShapeDtypeStruct(p.shape, _f32),
        name="hgrn2_lower_bounds",
    )(p)


def kernel(x, c, w_ada, b_ada, w_in, hg_lower_bounds, hg_norm, ret_norm, w_out, pool_w, pool_b,
           pool_scale, w_router, b_router, w_gu, b_gu, w_down, b_down, final_norm):
    b, s, d = x.shape
    assert b == 1 and d == _D
    depth = w_ada.shape[0]
    assert depth == 2
    xs = x.reshape(s, d)
    mod = _ada(c, w_ada, b_ada)
    lb_all = _lower_bounds(hg_lower_bounds)
    cosf, sinf = _rotary_tables(s)
    x1, hn2, logits = _mixer0(xs, mod[0], w_in[0], lb_all[0], hg_norm[0], ret_norm[0], w_out[0],
                              cosf, sinf, w_router[0], b_router[0])
    xs = _moe(x1, hn2, logits, mod[0], final_norm, w_gu, b_gu, w_down, b_down, 0, False)
    x1, hn2, logits = _pool_mixer(xs, mod[1], pool_w[0], pool_b[0], pool_scale[0], w_router[1],
                                  b_router[1])
    xs = _moe(x1, hn2, logits, mod[1], final_norm, w_gu, b_gu, w_down, b_down, 1, True)
    return xs.reshape(b, s, d)
```

```python
import functools
import math

import jax
import jax.numpy as jnp
import numpy as np
from jax import lax
from jax.experimental import pallas as pl
from jax.experimental.pallas import tpu as pltpu

_D = 1024
_E = 32
_K = 4
_F = 1024
_HEADS = 4
_DH = 128
_CH = 64
_SUB = 16
_EPS = 1e-6
_BLK = 256
_TB = 256
_TT = 512
_ALIGN = 8
_RT_ROWS = 16
_BULK = 64
_SORT_CHUNKS = 3
_AHEAD = 4
_RING = 2 * _AHEAD
_NEG = -1e30
_LOG2E = 1.4426950408889634
_SWIGLU_LIMIT = 7.0
_SWIGLU_ALPHA = 1.702
_POOL_WINDOWS = (2, 4, 8, 16)
_HALO = 16
_VMEM_LIMIT = 56 * 1024 * 1024

_f32 = jnp.float32
_bf16 = jnp.bfloat16


def _dot(a, b):
    return jnp.dot(a, b, preferred_element_type=_f32)


def _dot_nt(a, b):
    return lax.dot_general(a, b, (((1,), (1,)), ((), ())), preferred_element_type=_f32)


def _split3(a):
    a0 = a.astype(_bf16)
    r1 = a - a0.astype(_f32)
    a1 = r1.astype(_bf16)
    a2 = (r1 - a1.astype(_f32)).astype(_bf16)
    return a0, a1, a2


def _dot_exact_lhs(m, a):
    a0, a1, a2 = _split3(a)
    return _dot(m, a0) + _dot(m, a1) + _dot(m, a2)


def _rms(x):
    return x * lax.rsqrt(jnp.mean(x * x, axis=-1, keepdims=True) + _EPS)


def _silu(x):
    return x * jax.nn.sigmoid(x)


def _ada_kernel(c_ref, w_ref, b_ref, o_ref):
    c = c_ref[...]
    cond = _silu(c)
    o_ref[0] = jnp.sum(w_ref[0] * cond, axis=0, keepdims=True) + b_ref[0]


def _ada(c, w_ada, b_ada):
    depth = w_ada.shape[0]
    n = depth * 6
    out = pl.pallas_call(
        _ada_kernel,
        grid=(n,),
        in_specs=[
            pl.BlockSpec((_D, 1), lambda j: (0, 0)),
            pl.BlockSpec((1, _D, _D), lambda j: (j // 6, 0, j % 6)),
            pl.BlockSpec((1, 1, _D), lambda j: (j, 0, 0)),
        ],
        out_specs=pl.BlockSpec((1, 1, _D), lambda j: (j, 0, 0)),
        out_shape=jax.ShapeDtypeStruct((n, 1, _D), _f32),
        name="ada_mod",
    )(c.reshape(_D, 1), w_ada, b_ada.reshape(n, 1, _D))
    return out.reshape(depth, 6, _D)


def _ffn_prologue(x1, mod, wr_ref, br_ref, x1_ref, hn2_ref, lg_ref):
    sh_f, sc_f = mod[3:4, :], mod[4:5, :]
    hn2 = _rms(x1) * (1.0 + sc_f) + sh_f
    x1_ref[...] = x1
    a0, a1, _ = _split3(hn2)
    hn2_ref[...] = a0
    w0, w1, _ = _split3(wr_ref[...])
    both = _dot(a0, jnp.concatenate([w0, w1], axis=1))
    lg_ref[...] = both[:, :_E] + both[:, _E:] + _dot(a1, w0) + br_ref[...]


def _mixer0_kernel(x_ref, mod_ref, win_ref, lb_ref, hgg_ref, rtg_ref, wout_ref, rot_a_ref, rot_b_ref,
                   wr_ref, br_ref, x1_ref, hn2_ref, lg_ref,
                   q_s, k_s, v_s, b_s, o_s, st_hg, st_rt):
    tb = x_ref.shape[0]
    nch = tb // _CH
    nsub = _CH // _SUB

    @pl.when(pl.program_id(0) == 0)
    def _():
        st_hg[...] = jnp.zeros_like(st_hg)
        st_rt[...] = jnp.zeros_like(st_rt)

    x = x_ref[...]
    mod = mod_ref[...]
    sh_m, sc_m, g_m = mod[0:1, :], mod[1:2, :], mod[2:3, :]
    hb = (_rms(x) * (1.0 + sc_m) + sh_m).astype(_bf16)
    width = _HEADS * _DH

    def sec(j):
        return _dot(hb, win_ref[:, j * width:(j + 1) * width])

    hq = sec(0)
    lb = lb_ref[...]
    fg = lb + (1.0 - lb) * jax.nn.sigmoid(sec(1))
    hi = sec(2)
    g = jnp.log(fg) * _LOG2E
    r_i = lax.broadcasted_iota(jnp.int32, (tb, tb), 0)
    c_i = lax.broadcasted_iota(jnp.int32, (tb, tb), 1)
    tri = jnp.where(((r_i >> 6) == (c_i >> 6)) & (c_i <= r_i), 1.0, 0.0).astype(_bf16)
    bcum = _dot_exact_lhs(tri, g)
    qh = _silu(hq)
    kk = 1.0 - fg
    for h in range(_HEADS):
        sl = slice(h * _DH, (h + 1) * _DH)
        q_s[h] = qh[:, sl]
        k_s[h] = kk[:, sl]
        v_s[h] = hi[:, sl]
        b_s[h] = bcum[:, sl]

    hgg = hgg_ref[...]
    row64 = lax.broadcasted_iota(jnp.int32, (_HEADS, _CH, _DH), 1)
    row16 = lax.broadcasted_iota(jnp.int32, (_HEADS * nsub, _SUB, 1), 1)
    lead16 = lax.broadcasted_iota(jnp.int32, (_HEADS * nsub, _SUB, _CH), 0)
    col16 = lax.broadcasted_iota(jnp.int32, (_HEADS * nsub, _SUB, _CH), 2) - _SUB * (lead16 & (nsub - 1))

    def chunk(c, carry):
        rows = pl.ds(pl.multiple_of(c * _CH, _CH), _CH)
        q = q_s[:, rows, :]
        k = k_s[:, rows, :]
        v = v_s[:, rows, :]
        b = b_s[:, rows, :]
        st = st_hg[...]
        vb = v.astype(_bf16)
        qe = (q * jnp.exp2(b)).astype(_bf16)
        o = jnp.einsum('htd,hed->hte', qe, st.astype(_bf16), preferred_element_type=_f32)
        blocks = [jnp.zeros((_HEADS, _SUB, _CH), _f32)]
        for i in range(1, nsub):
            ref = b[:, _SUB * i - 1:_SUB * i, :]
            qp = q[:, _SUB * i:_SUB * (i + 1), :] * jnp.exp2(b[:, _SUB * i:_SUB * (i + 1), :] - ref)
            kp = k * jnp.exp2(jnp.where(row64 < _SUB * i, ref - b, _NEG))
            blocks.append(jnp.einsum('htd,hsd->hts', qp.astype(_bf16), kp.astype(_bf16),
                                     preferred_element_type=_f32))
        a = jnp.concatenate(blocks, axis=1)
        q3 = q.reshape(_HEADS * nsub, _SUB, _DH)
        k3 = k.reshape(_HEADS * nsub, _SUB, _DH)
        b3 = b.reshape(_HEADS * nsub, _SUB, _DH)
        ad = jnp.zeros((_HEADS * nsub, _SUB, _CH), _f32)
        for s in range(_SUB):
            e = jnp.exp2(b3 - b3[:, s:s + 1, :])
            w = jnp.sum(q3 * e * k3[:, s:s + 1, :], axis=-1, keepdims=True)
            ad = jnp.where((col16 == s) & (row16 >= s), w, ad)
        a = a + ad.reshape(_HEADS, _CH, _CH)
        o = o + jnp.einsum('hts,hse->hte', a.astype(_bf16), vb, preferred_element_type=_f32)
        bl = b[:, _CH - 1:_CH, :]
        kd = (k * jnp.exp2(bl - b)).astype(_bf16)
        dec = jnp.exp2(bl)
        for h in range(_HEADS):
            st_hg[h] = st[h] * dec[h] + _dot(v[h].T.astype(_bf16), kd[h])
        o = o * lax.rsqrt(jnp.mean(o * o, axis=-1, keepdims=True) + _EPS) * hgg
        o_s[:, rows, :] = o
        return carry

    lax.fori_loop(0, nch, chunk, 0)
    hog = sec(3)
    cat_hg = jnp.concatenate([o_s[h] for h in range(_HEADS)], axis=1) * _silu(hog)

    rq = sec(4)
    rk = sec(5)
    rv = sec(6)
    cos_a, sin_a = rot_a_ref[0, 0:1, :], rot_a_ref[0, 1:2, :]
    cos_b, sin_b = rot_b_ref[0], rot_b_ref[1]
    first_half = lax.broadcasted_iota(jnp.int32, (1, _DH), 1) < _DH // 2
    cosf = cos_a * cos_b - sin_a * sin_b
    sinf = (sin_a * cos_b + cos_a * sin_b) * jnp.where(first_half, -1.0, 1.0)
    t_i = lax.broadcasted_iota(jnp.int32, (tb, tb), 0)
    s_i = lax.broadcasted_iota(jnp.int32, (tb, tb), 1)
    rel = (t_i - s_i).astype(_f32)
    causal = t_i >= s_i
    tcol = lax.broadcasted_iota(jnp.int32, (tb, 1), 0).astype(_f32)
    rtg = rtg_ref[...]
    outs = []
    for h in range(_HEADS):
        sl = slice(h * _DH, (h + 1) * _DH)
        lg = math.log(1.0 - 2.0 ** (-5.0 - h))
        xq = rq[:, sl]
        xk = rk[:, sl]
        q = xq * cosf + pltpu.roll(xq, _DH // 2, 1) * sinf
        k = (xk * cosf + pltpu.roll(xk, _DH // 2, 1) * sinf) * (_DH ** -0.5)
        v = rv[:, sl]
        dmat = jnp.where(causal, jnp.exp(jnp.where(causal, rel, 0.0) * lg), 0.0)
        scores = _dot_nt(q.astype(_bf16), k.astype(_bf16)) * dmat
        o = _dot(scores.astype(_bf16), v.astype(_bf16))
        st = st_rt[h]
        qd = q * jnp.exp((tcol + 1.0) * lg)
        o = o + _dot_nt(qd.astype(_bf16), st.astype(_bf16))
        kd = k * jnp.exp((tb - 1.0 - tcol) * lg)
        st_rt[h] = st * math.exp(tb * lg) + _dot(v.T.astype(_bf16), kd.astype(_bf16))
        o = o * lax.rsqrt(jnp.mean(o * o, axis=-1, keepdims=True) + _EPS) * rtg[h]
        outs.append(o)
    rg = sec(7)
    cat_rt = jnp.concatenate(outs, axis=1) * _silu(rg)

    cat = jnp.concatenate([cat_hg, cat_rt], axis=1).astype(_bf16)
    mix = _dot(cat, wout_ref[...])
    x1 = x + g_m * mix
    _ffn_prologue(x1, mod, wr_ref, br_ref, x1_ref, hn2_ref, lg_ref)


def _mixer0(x, mod, w_in, lb, hg_gain, ret_gain, w_out, w_router, b_router):
    s = x.shape[0]
    tb = min(_TB, s)
    rot_a, rot_b = _rotary_tables(s, tb)
    width = _HEADS * _DH
    full = lambda shape: pl.BlockSpec(shape, lambda i: (0,) * len(shape))
    rowblk = lambda w: pl.BlockSpec((tb, w), lambda i: (i, 0))
    return pl.pallas_call(
        _mixer0_kernel,
        grid=(s // tb,),
        in_specs=[
            rowblk(_D), full((6, _D)), full((_D, 8 * width)), full((1, width)),
            full((_HEADS, 1, _DH)), full((_HEADS, 1, _DH)), full((2 * width, _D)),
            pl.BlockSpec((1, 2, _DH), lambda i: (i, 0, 0)), full((2, tb, _DH)),
            full((_D, _E)), full((1, _E)),
        ],
        out_specs=[rowblk(_D), rowblk(_D), rowblk(_E)],
        out_shape=[jax.ShapeDtypeStruct((s, _D), _f32), jax.ShapeDtypeStruct((s, _D), _bf16),
                   jax.ShapeDtypeStruct((s, _E), _f32)],
        scratch_shapes=[pltpu.VMEM((_HEADS, tb, _DH), _f32)] * 5
        + [pltpu.VMEM((_HEADS, _DH, _DH), _f32)] * 2,
        compiler_params=pltpu.CompilerParams(dimension_semantics=("arbitrary",),
                                             vmem_limit_bytes=_VMEM_LIMIT),
        name="mixer_hgrn2_retention",
    )(x, mod, w_in.astype(_bf16), lb.reshape(1, width), hg_gain.reshape(_HEADS, 1, _DH),
      ret_gain.reshape(_HEADS, 1, _DH), w_out.astype(_bf16), rot_a, rot_b, w_router,
      b_router.reshape(1, _E))


def _pool_kernel(x_ref, mod_ref, pw_ref, pb_ref, ps_ref, wr_ref, br_ref,
                 x1_ref, hn2_ref, lg_ref, halo):
    tb = x_ref.shape[0]
    i = pl.program_id(0)

    @pl.when(i == 0)
    def _():
        halo[...] = jnp.zeros_like(halo)

    x = x_ref[...]
    mod = mod_ref[...]
    sh_m, sc_m, g_m = mod[0:1, :], mod[1:2, :], mod[2:3, :]
    hn = _rms(x) * (1.0 + sc_m) + sh_m
    ext = jnp.concatenate([halo[...], hn], axis=0)
    halo[...] = hn[tb - _HALO:, :]
    t = (i * tb + lax.broadcasted_iota(jnp.int32, (tb, 1), 0)).astype(_f32)
    gw = _D // len(_POOL_WINDOWS)
    ys = []
    for gi, w in enumerate(_POOL_WINDOWS):
        acc = ext[:, gi * gw:(gi + 1) * gw]
        shift = 1
        while shift < w:
            acc = acc + pltpu.roll(acc, shift, 0)
            shift *= 2
        win = acc[_HALO:, :]
        cnt = jnp.minimum(t + 1.0, float(w))
        p = win / cnt - hn[:, gi * gw:(gi + 1) * gw]
        ys.append(_dot(p.astype(_bf16), pw_ref[gi]) + pb_ref[gi])
    mix = jnp.concatenate(ys, axis=1) * ps_ref[...]
    x1 = x + g_m * mix
    _ffn_prologue(x1, mod, wr_ref, br_ref, x1_ref, hn2_ref, lg_ref)


def _pool_mixer(x, mod, pool_w, pool_b, pool_scale, w_router, b_router):
    s = x.shape[0]
    tb = min(_TB, s)
    ng = len(_POOL_WINDOWS)
    gw = _D // ng
    full = lambda shape: pl.BlockSpec(shape, lambda i: (0,) * len(shape))
    rowblk = lambda w: pl.BlockSpec((tb, w), lambda i: (i, 0))
    return pl.pallas_call(
        _pool_kernel,
        grid=(s // tb,),
        in_specs=[rowblk(_D), full((6, _D)), full((ng, gw, gw)), full((ng, 1, gw)), full((1, _D)),
                  full((_D, _E)), full((1, _E))],
        out_specs=[rowblk(_D), rowblk(_D), rowblk(_E)],
        out_shape=[jax.ShapeDtypeStruct((s, _D), _f32), jax.ShapeDtypeStruct((s, _D), _bf16),
                   jax.ShapeDtypeStruct((s, _E), _f32)],
        scratch_shapes=[pltpu.VMEM((_HALO, _D), _f32)],
        compiler_params=pltpu.CompilerParams(dimension_semantics=("arbitrary",),
                                             vmem_limit_bytes=_VMEM_LIMIT),
        name="mixer_pool",
    )(x, mod, pool_w.astype(_bf16), pool_b.reshape(ng, 1, gw), pool_scale.reshape(1, _D),
      w_router, b_router.reshape(1, _E))


def _route_kernel(lg_ref, route_ref, route_t_ref, cnt_ref):
    tt = lg_ref.shape[0]
    vals = lg_ref[...]
    lane = lax.broadcasted_iota(jnp.int32, (tt, _E), 1).astype(_f32)
    tops, idxs, hots = [], [], []
    for _ in range(_K):
        m = jnp.max(vals, axis=-1, keepdims=True)
        idx = jnp.min(jnp.where(vals == m, lane, float(_E)), axis=-1, keepdims=True)
        hot = lane == idx
        vals = jnp.where(hot, -jnp.inf, vals)
        tops.append(m)
        idxs.append(idx)
        hots.append(hot)
    ex = [jnp.exp(m - tops[0]) for m in tops]
    den = ex[0] + ex[1] + ex[2] + ex[3]
    member = jnp.zeros((tt, _E), _f32)
    for hot in hots:
        member = member + jnp.where(hot, 1.0, 0.0)
    r_i = lax.broadcasted_iota(jnp.int32, (tt, tt), 0)
    c_i = lax.broadcasted_iota(jnp.int32, (tt, tt), 1)
    below = jnp.where(c_i < r_i, 1.0, 0.0).astype(_bf16)
    cum = _dot(below, member.astype(_bf16))
    cnt = jnp.sum(member, axis=0, keepdims=True)
    cnt_ref[0] = cnt
    r8 = jnp.floor((cnt + (_ALIGN - 1.0)) * (1.0 / _ALIGN)) * _ALIGN
    j_i = lax.broadcasted_iota(jnp.int32, (_E, _E), 0)
    e_i = lax.broadcasted_iota(jnp.int32, (_E, _E), 1)
    before = jnp.where(j_i < e_i, 1.0, 0.0).astype(_bf16)
    lo = _dot(jnp.broadcast_to(r8, (8, _E)).astype(_bf16), before)[0:1, :]
    row = cum + lo
    out_lane = lax.broadcasted_iota(jnp.int32, (tt, 128), 1)
    out = jnp.zeros((tt, 128), _f32)
    for k in range(_K):
        pos = jnp.sum(jnp.where(hots[k], row, 0.0), axis=-1, keepdims=True)
        out = jnp.where(out_lane == k, idxs[k], out)
        out = jnp.where(out_lane == _K + k, ex[k] / den, out)
        out = jnp.where(out_lane == 2 * _K + k, pos, out)
    route_ref[...] = out
    route_t_ref[...] = out.T[0:_RT_ROWS, :]


def _route(logits):
    s = logits.shape[0]
    tt = min(_TT, s)
    nt = s // tt
    return pl.pallas_call(
        _route_kernel,
        grid=(nt,),
        in_specs=[pl.BlockSpec((tt, _E), lambda i: (i, 0))],
        out_specs=[pl.BlockSpec((tt, 128), lambda i: (i, 0)),
                   pl.BlockSpec((_RT_ROWS, tt), lambda i: (0, i)),
                   pl.BlockSpec((1, 1, _E), lambda i: (i, 0, 0))],
        out_shape=[jax.ShapeDtypeStruct((s, 128), _f32),
                   jax.ShapeDtypeStruct((_RT_ROWS, s), _f32),
                   jax.ShapeDtypeStruct((nt, 1, _E), _f32)],
        compiler_params=pltpu.CompilerParams(dimension_semantics=("arbitrary",)),
        name="route_topk_rank",
    )(logits)


def _offsets_kernel(cnt_ref, lo_ref, goff_ref, nch_ref, tail_ref, tsum_ref):
    cnt = cnt_ref[...]
    ntp = cnt.shape[0]
    r8 = jnp.floor((cnt + (_ALIGN - 1.0)) * (1.0 / _ALIGN)) * _ALIGN
    j_i = lax.broadcasted_iota(jnp.int32, (_E, _E), 0)
    e_i = lax.broadcasted_iota(jnp.int32, (_E, _E), 1)
    before = jnp.where(j_i < e_i, 1.0, 0.0).astype(_bf16)

    def times_before(a):
        a0, a1, a2 = _split3(a)
        return _dot(a0, before) + _dot(a1, before) + _dot(a2, before)

    lo = times_before(r8)
    tot = jnp.sum(r8, axis=0, keepdims=True)
    cap = jnp.floor((tot + (_BLK - 1.0)) * (1.0 / _BLK)) * _BLK
    start = times_before(jnp.broadcast_to(cap, (8, _E)))[0:1, :]
    a_i = lax.broadcasted_iota(jnp.int32, (ntp, ntp), 0)
    b_i = lax.broadcasted_iota(jnp.int32, (ntp, ntp), 1)
    earlier = jnp.where(b_i < a_i, 1.0, 0.0).astype(_bf16)
    goff = start + _dot_exact_lhs(earlier, r8)
    lo_ref[...] = lo.astype(jnp.int32)
    goff_ref[...] = goff.astype(jnp.int32)
    nch_ref[...] = (r8 * (1.0 / _ALIGN)).astype(jnp.int32)
    trow = lax.broadcasted_iota(jnp.int32, (8, _E), 0)
    end = start + cap
    used_blocks = jnp.max(end, axis=-1, keepdims=True) * (1.0 / _BLK)
    tail = jnp.where(trow == 0, start + tot,
                     jnp.where(trow == 1, (cap - tot) * (1.0 / _ALIGN),
                               jnp.where(trow == 2, used_blocks,
                                         jnp.where(trow == 3, start * (1.0 / _BLK),
                                                   jnp.where(trow == 4, cap * (1.0 / _BLK), 0.0)))))
    tail_ref[...] = tail.astype(jnp.int32)
    tsum = jnp.sum(r8, axis=-1, keepdims=True)
    tsum_ref[...] = jnp.broadcast_to(tsum, (ntp, _E)).astype(jnp.int32)


def _offsets(cnt):
    ntp = cnt.shape[0]
    i32 = jnp.int32
    return pl.pallas_call(
        _offsets_kernel,
        out_shape=[jax.ShapeDtypeStruct((ntp, _E), i32),
                   jax.ShapeDtypeStruct((ntp, _E), i32), jax.ShapeDtypeStruct((ntp, _E), i32),
                   jax.ShapeDtypeStruct((8, _E), i32), jax.ShapeDtypeStruct((ntp, _E), i32)],
        name="route_offsets",
    )(cnt)


_HI_MASK = -65536


def _pack_pairs(x):
    half = x.shape[1] // 2
    bits = lax.bitcast_convert_type(x, jnp.int32)
    return (bits[:, :half] & _HI_MASK) | lax.shift_right_logical(bits[:, half:], 16)


def _unpack_pairs(p):
    hi = lax.bitcast_convert_type(p & _HI_MASK, _f32).astype(_bf16)
    lo = lax.bitcast_convert_type(lax.shift_left(p, 16), _f32).astype(_bf16)
    return hi, lo


def _for_each_part(nch_s, tile, fn):
    def per_expert(e, carry):
        rows = nch_s[tile * _E + e] * _ALIGN

        def bulk(j, c):
            fn(e, j * _BULK, _BULK)
            return c

        lax.fori_loop(0, lax.shift_right_logical(rows, _BULK.bit_length() - 1), bulk, 0)
        size = _BULK // 2
        while size >= _ALIGN:
            @pl.when((rows & size) != 0)
            def _(size=size):
                fn(e, rows & (-2 * size), size)
            size //= 2
        return carry
    lax.fori_loop(0, _E, per_expert, 0)


def _part_copy(src, src_row, dst, dst_row, size, sem):
    return pltpu.make_async_copy(src.at[pl.ds(pl.multiple_of(src_row, _ALIGN), size)],
                                 dst.at[pl.ds(pl.multiple_of(dst_row, _ALIGN), size)], sem)


def _wait_rows(total, limit, copy_of_size):
    size = _ALIGN
    while size * 2 <= limit:
        size *= 2
    while size >= _ALIGN:
        @pl.when((total & size) != 0)
        def _(size=size):
            copy_of_size(size).wait()
        size //= 2


def _zero_unused_blocks(zblk, dst_ref, first, sem):
    nb = dst_ref.shape[0] // _BLK

    def copy(b):
        return pltpu.make_async_copy(zblk, dst_ref.at[pl.ds(pl.multiple_of(b * _BLK, _BLK), _BLK)], sem)

    def fill(b, carry):
        copy(b).start()
        return carry

    def drain(b, carry):
        copy(b).wait()
        return carry

    lax.fori_loop(first, nb, fill, 0)
    lax.fori_loop(first, nb, drain, 0)


def _dispatch_kernel(lo_s, goff_s, nch_s, tail_s, tsum_s, hn_ref, route_t_ref, xs_ref,
                     buf, zblk, sem, *, nt):
    i = pl.program_id(0)
    tt = hn_ref.shape[0]
    cap = buf.shape[1]
    rt = route_t_ref[...]
    pos = [rt[2 * _K + k:2 * _K + k + 1, :] for k in range(_K)]

    slot = lax.rem(i, 2)
    tile_buf = buf.at[slot]
    fill_sem = sem.at[2]

    def wait_tile(tile, sl):
        _wait_rows(tsum_s[tile], cap,
                   lambda size: _part_copy(buf.at[sl], 0, xs_ref, 0, size, sem.at[sl]))

    @pl.when(i > 1)
    def _():
        wait_tile(i - 2, slot)

    rows = cap // _SORT_CHUNKS
    hn = hn_ref[...]
    for c in range(_SORT_CHUNKS):
        r_iota = (lax.broadcasted_iota(jnp.int32, (rows, tt), 0) + c * rows).astype(_f32)
        perm = jnp.zeros((rows, tt), _f32)
        for k in range(_K):
            perm = jnp.where(r_iota == pos[k], 1.0, perm)
        buf[slot, c * rows:(c + 1) * rows, :] = _pack_pairs(_dot(perm.astype(_bf16), hn))

    def start_part(e, off, size):
        _part_copy(tile_buf, lo_s[i * _E + e] + off, xs_ref, goff_s[i * _E + e] + off, size,
                   sem.at[slot]).start()

    _for_each_part(nch_s, i, start_part)

    @pl.when(i == nt - 1)
    def _():
        zblk[...] = jnp.zeros_like(zblk)

        def tail_copy(dst_row):
            return _part_copy(zblk, 0, xs_ref, dst_row, _ALIGN, fill_sem)

        def fill(e, carry):
            def one(j, c):
                tail_copy(tail_s[e] + j * _ALIGN).start()
                return c
            return lax.fori_loop(0, tail_s[_E + e], one, carry)

        def drain(e, carry):
            def one(j, c):
                tail_copy(0).wait()
                return c
            return lax.fori_loop(0, tail_s[_E + e], one, carry)

        lax.fori_loop(0, _E, fill, 0)
        lax.fori_loop(0, _E, drain, 0)
        _zero_unused_blocks(zblk, xs_ref, tail_s[2 * _E], fill_sem)
        if nt > 1:
            wait_tile(i - 1, 1 - slot)
        wait_tile(i, slot)


def _dispatch(lo, goff, nch, tail, tsum, hn, route_t, n_rows):
    s = hn.shape[0]
    tt = min(_TT, s)
    nt = s // tt
    cap = tt * _K + _E * _ALIGN
    grid_spec = pltpu.PrefetchScalarGridSpec(
        num_scalar_prefetch=5,
        grid=(nt,),
        in_specs=[
            pl.BlockSpec((tt, _D), lambda i, *_: (i, 0)),
            pl.BlockSpec((_RT_ROWS, tt), lambda i, *_: (0, i)),
        ],
        out_specs=pl.BlockSpec(memory_space=pl.ANY),
        scratch_shapes=[pltpu.VMEM((2, cap, _D // 2), jnp.int32),
                        pltpu.VMEM((_BLK, _D // 2), jnp.int32), pltpu.SemaphoreType.DMA((3,))],
    )
    return pl.pallas_call(
        functools.partial(_dispatch_kernel, nt=nt),
        grid_spec=grid_spec,
        out_shape=jax.ShapeDtypeStruct((n_rows, _D // 2), jnp.int32),
        compiler_params=pltpu.CompilerParams(dimension_semantics=("arbitrary",),
                                             vmem_limit_bytes=_VMEM_LIMIT),
        name="moe_dispatch",
    )(lo, goff, nch, tail, tsum, hn, route_t)


def _expert_kernel(tail_s, wgu_ref, bgu_ref, wd_ref, bd_ref, xs_ref, ys_ref,
                   wgu_c, wd_c, xin, yout, zblk, sem_in, sem_out):
    e = pl.program_id(0)
    used = tail_s[2 * _E]
    first = tail_s[3 * _E + e]
    nblk = tail_s[4 * _E + e]
    half = _D // 2

    def rows(g):
        return pl.ds(pl.multiple_of(g * _BLK, _BLK), _BLK)

    def in_copy(g, slot):
        return pltpu.make_async_copy(xs_ref.at[rows(g)], xin.at[slot], sem_in.at[slot])

    def out_copy(g, slot):
        return pltpu.make_async_copy(yout.at[slot], ys_ref.at[rows(g)], sem_out.at[slot])

    def slot_of(g):
        return lax.rem(g, _RING)

    @pl.when(e == 0)
    def _():
        for g in range(_AHEAD):
            @pl.when(g < used)
            def _(g=g):
                in_copy(g, g).start()

    @pl.when(nblk > 0)
    def _():
        step = 128
        for r in range(0, _D, step):
            wgu_c[r:r + step, :] = wgu_ref[0, 0, r:r + step, :].astype(_bf16)
        for r in range(0, _F, step):
            wd_c[r:r + step, :] = wd_ref[0, 0, r:r + step, :].astype(_bf16)

    def unit(g, n):
        for d in range(n):
            in_copy(g + d, slot_of(g + d)).wait()
        for d in range(_AHEAD, _AHEAD + n):
            @pl.when(g + d < used)
            def _(d=d):
                in_copy(g + d, slot_of(g + d)).start()
        parts = [_unpack_pairs(xin[slot_of(g + d)]) for d in range(n)]
        x_hi = jnp.concatenate([p[0] for p in parts], axis=0)
        x_lo = jnp.concatenate([p[1] for p in parts], axis=0)
        gu = _dot(x_hi, wgu_c[:half, :]) + _dot(x_lo, wgu_c[half:, :]) + bgu_ref[0, 0]
        x_glu = jnp.minimum(gu[:, :_F], _SWIGLU_LIMIT)
        x_lin = jnp.clip(gu[:, _F:], -_SWIGLU_LIMIT, _SWIGLU_LIMIT)
        act = x_glu * jax.nn.sigmoid(_SWIGLU_ALPHA * x_glu) * (x_lin + 1.0)
        y = _dot(act.astype(_bf16), wd_c[...]) + bd_ref[0, 0]
        packed = _pack_pairs(y.astype(_bf16).astype(_f32))
        for d in range(n):
            @pl.when(g + d >= _RING)
            def _(d=d):
                out_copy(g + d - _RING, slot_of(g + d)).wait()
            yout[slot_of(g + d)] = packed[d * _BLK:(d + 1) * _BLK, :]
            out_copy(g + d, slot_of(g + d)).start()

    def widest(j, carry):
        unit(first + _AHEAD * j, _AHEAD)
        return carry

    lax.fori_loop(0, nblk // _AHEAD, widest, 0)
    n = _AHEAD // 2
    while n >= 1:
        @pl.when((nblk & n) != 0)
        def _(n=n):
            unit(first + (nblk & (-2 * n)), n)
        n //= 2

    @pl.when(e == _E - 1)
    def _():
        for d in range(1, _RING + 1):
            @pl.when(used >= d)
            def _(d=d):
                out_copy(used - d, slot_of(used - d)).wait()

        zblk[...] = jnp.zeros_like(zblk)
        _zero_unused_blocks(zblk, ys_ref, used, sem_out.at[0])


def _experts(tail, xs, w_gu, b_gu, w_down, b_down, layer):
    n_rows = xs.shape[0]
    wsel = lambda e, *_: (layer, e, 0, 0)
    grid_spec = pltpu.PrefetchScalarGridSpec(
        num_scalar_prefetch=1,
        grid=(_E,),
        in_specs=[
            pl.BlockSpec((1, 1, _D, 2 * _F), wsel),
            pl.BlockSpec((1, 1, 1, 2 * _F), wsel),
            pl.BlockSpec((1, 1, _F, _D), wsel),
            pl.BlockSpec((1, 1, 1, _D), wsel),
            pl.BlockSpec(memory_space=pl.ANY),
        ],
        out_specs=pl.BlockSpec(memory_space=pl.ANY),
        scratch_shapes=[pltpu.VMEM((_D, 2 * _F), _bf16), pltpu.VMEM((_F, _D), _bf16),
                        pltpu.VMEM((_RING, _BLK, _D // 2), jnp.int32),
                        pltpu.VMEM((_RING, _BLK, _D // 2), jnp.int32),
                        pltpu.VMEM((_BLK, _D // 2), jnp.int32),
                        pltpu.SemaphoreType.DMA((_RING,)), pltpu.SemaphoreType.DMA((_RING,))],
    )
    depth = w_gu.shape[0]
    return pl.pallas_call(
        _expert_kernel,
        grid_spec=grid_spec,
        out_shape=jax.ShapeDtypeStruct((n_rows, _D // 2), jnp.int32),
        compiler_params=pltpu.CompilerParams(dimension_semantics=("arbitrary",),
                                             vmem_limit_bytes=_VMEM_LIMIT),
        name="moe_experts",
    )(tail, w_gu, b_gu.reshape(depth, _E, 1, 2 * _F), w_down, b_down.reshape(depth, _E, 1, _D), xs)


def _combine_kernel(lo_s, goff_s, nch_s, tsum_s, route_ref, x1_ref, mod_ref, fin_ref, ys_ref,
                    out_ref, ybuf, sem, *, final, nt):
    i = pl.program_id(0)
    tt = x1_ref.shape[0]
    cap = ybuf.shape[1]
    slot = lax.rem(i, 2)

    def fetch(tile, dst):
        def start_part(e, off, size):
            _part_copy(ys_ref, goff_s[tile * _E + e] + off, dst, lo_s[tile * _E + e] + off, size,
                       sem).start()
        _for_each_part(nch_s, tile, start_part)

    @pl.when(i == 0)
    def _():
        ybuf[...] = jnp.zeros_like(ybuf)
        fetch(0, ybuf.at[0])

    _wait_rows(tsum_s[i], cap, lambda size: _part_copy(ys_ref, 0, ybuf.at[slot], 0, size, sem))

    @pl.when(i + 1 < nt)
    def _():
        fetch(i + 1, ybuf.at[1 - slot])

    route = route_ref[...]
    pos = [route[:, 2 * _K + k:2 * _K + k + 1] for k in range(_K)]
    gate = [route[:, _K + k:_K + k + 1] for k in range(_K)]

    cols = cap // _SORT_CHUNKS
    moe_hi = jnp.zeros((tt, _D // 2), _f32)
    moe_lo = jnp.zeros((tt, _D // 2), _f32)
    for c in range(_SORT_CHUNKS):
        lane_r = (lax.broadcasted_iota(jnp.int32, (tt, cols), 1) + c * cols).astype(_f32)
        wmat = jnp.zeros((tt, cols), _f32)
        for k in range(_K):
            wmat = jnp.where(lane_r == pos[k], gate[k], wmat)
        wmat = wmat.astype(_bf16)
        y_hi, y_lo = _unpack_pairs(ybuf[slot, c * cols:(c + 1) * cols, :])
        moe_hi = moe_hi + _dot(wmat, y_hi)
        moe_lo = moe_lo + _dot(wmat, y_lo)
    moe = jnp.concatenate([moe_hi, moe_lo], axis=1)
    x2 = x1_ref[...] + mod_ref[5:6, :] * moe
    if final:
        x2 = _rms(x2) * fin_ref[...]
    out_ref[...] = x2


def _combine(lo, goff, nch, tsum, route, x1, mod, final_norm, ys, final):
    s = x1.shape[0]
    tt = min(_TT, s)
    nt = s // tt
    cap = tt * _K + _E * _ALIGN
    grid_spec = pltpu.PrefetchScalarGridSpec(
        num_scalar_prefetch=4,
        grid=(nt,),
        in_specs=[
            pl.BlockSpec((tt, 128), lambda i, *_: (i, 0)),
            pl.BlockSpec((tt, _D), lambda i, *_: (i, 0)),
            pl.BlockSpec((6, _D), lambda i, *_: (0, 0)),
            pl.BlockSpec((1, _D), lambda i, *_: (0, 0)),
            pl.BlockSpec(memory_space=pl.ANY),
        ],
        out_specs=pl.BlockSpec((tt, _D), lambda i, *_: (i, 0)),
        scratch_shapes=[pltpu.VMEM((2, cap, _D // 2), jnp.int32), pltpu.SemaphoreType.DMA(())],
    )
    return pl.pallas_call(
        functools.partial(_combine_kernel, final=final, nt=nt),
        grid_spec=grid_spec,
        out_shape=jax.ShapeDtypeStruct((s, _D), _f32),
        compiler_params=pltpu.CompilerParams(dimension_semantics=("arbitrary",),
                                             vmem_limit_bytes=_VMEM_LIMIT),
        name="moe_combine",
    )(lo, goff, nch, tsum, route, x1, mod, final_norm.reshape(1, _D), ys)


def _moe(x1, hn2, logits, mod, final_norm, w_gu, b_gu, w_down, b_down, layer, final):
    s = x1.shape[0]
    tt = min(_TT, s)
    nt = s // tt
    ntp = -(-nt // 8) * 8
    n_rows = -(-(s * _K + (_ALIGN - 1) * nt * _E) // _BLK) * _BLK + _E * _BLK
    route, route_t, cnt = _route(logits)
    cnt = jnp.pad(cnt.reshape(nt, _E), ((0, ntp - nt), (0, 0)))
    lo, goff, nch, tail, tsum = _offsets(cnt)
    lo_s, goff_s, nch_s = (a[:nt].reshape(nt * _E) for a in (lo, goff, nch))
    tail_s = tail[:5].reshape(5 * _E)
    tsum_s = tsum[:nt, 0]
    xs = _dispatch(lo_s, goff_s, nch_s, tail_s, tsum_s, hn2, route_t, n_rows)
    ys = _experts(tail_s, xs, w_gu, b_gu, w_down, b_down, layer)
    return _combine(lo_s, goff_s, nch_s, tsum_s, route, x1, mod, final_norm, ys, final)


def _rotary_tables(s, tb):
    half = _DH // 2
    inv = (np.float32(10000.0) ** (-np.arange(half, dtype=np.float32) / np.float32(half)))
    inv = inv.astype(np.float64)

    def cos_sin(positions):
        ang = positions.astype(np.float64)[:, None] * inv[None, :]
        cos = np.concatenate([np.cos(ang)] * 2, axis=1)
        sin = np.concatenate([np.sin(ang)] * 2, axis=1)
        return cos.astype(np.float32), sin.astype(np.float32)

    cos_a, sin_a = cos_sin(np.arange(s // tb) * tb)
    cos_b, sin_b = cos_sin(np.arange(tb))
    return (jnp.asarray(np.stack([cos_a, sin_a], axis=1)), jnp.asarray(np.stack([cos_b, sin_b], axis=0)))


def _lower_bounds_kernel(p_ref, o_ref):
    p = p_ref[...]
    n = p.shape[0]
    m = p[0:1, :]
    for j in range(1, n):
        m = jnp.maximum(m, p[j:j + 1, :])
    ex = [jnp.exp(p[j:j + 1, :] - m) for j in range(n)]
    den = ex[0]
    for j in range(1, n):
        den = den + ex[j]
    acc = jnp.zeros_like(den)
    rows = []
    for j in range(n):
        acc = acc + ex[j] / den
        rows.append(acc)
    o_ref[...] = jnp.concatenate(rows, axis=0)


def _lower_bounds(p):
    return pl.pallas_call(
        _lower_bounds_kernel,
        out_shape=jax.ShapeDtypeStruct(p.shape, _f32),
        name="hgrn2_lower_bounds",
    )(p)


def kernel(x, c, w_ada, b_ada, w_in, hg_lower_bounds, hg_norm, ret_norm, w_out, pool_w, pool_b,
           pool_scale, w_router, b_router, w_gu, b_gu, w_down, b_down, final_norm):
    b, s, d = x.shape
    assert b == 1 and d == _D
    depth = w_ada.shape[0]
    assert depth == 2
    xs = x.reshape(s, d)
    mod = _ada(c, w_ada, b_ada)
    lb_all = _lower_bounds(hg_lower_bounds)
    x1, hn2, logits = _mixer0(xs, mod[0], w_in[0], lb_all[0], hg_norm[0], ret_norm[0], w_out[0],
                              w_router[0], b_router[0])
    xs = _moe(x1, hn2, logits, mod[0], final_norm, w_gu, b_gu, w_down, b_down, 0, False)
    x1, hn2, logits = _pool_mixer(xs, mod[1], pool_w[0], pool_b[0], pool_scale[0], w_router[1],
                                  b_router[1])
    xs = _moe(x1, hn2, logits, mod[1], final_norm, w_gu, b_gu, w_down, b_down, 1, True)
    return xs.reshape(b, s, d)
```

```python
import functools
import math

import jax
import jax.numpy as jnp
import numpy as np
from jax import lax
from jax.experimental import pallas as pl
from jax.experimental.pallas import tpu as pltpu

_D = 1024
_E = 32
_K = 4
_F = 1024
_HEADS = 4
_DH = 128
_CH = 64
_SUB = 16
_EPS = 1e-6
_BLK = 256
_TB = 256
_TBP = 512
_TT = 512
_ALIGN = 8
_RT_ROWS = 16
_BULK = 64
_SORT_CHUNKS = 3
_AHEAD = 4
_RING = 2 * _AHEAD
_NEG = -1e30
_LOG2E = 1.4426950408889634
_SWIGLU_LIMIT = 7.0
_SWIGLU_ALPHA = 1.702
_POOL_WINDOWS = (2, 4, 8, 16)
_HALO = 16
_VMEM_LIMIT = 56 * 1024 * 1024

_f32 = jnp.float32
_bf16 = jnp.bfloat16


def _dot(a, b):
    return jnp.dot(a, b, preferred_element_type=_f32)


def _dot_nt(a, b):
    return lax.dot_general(a, b, (((1,), (1,)), ((), ())), preferred_element_type=_f32)


def _split3(a):
    a0 = a.astype(_bf16)
    r1 = a - a0.astype(_f32)
    a1 = r1.astype(_bf16)
    a2 = (r1 - a1.astype(_f32)).astype(_bf16)
    return a0, a1, a2


def _dot_exact_lhs(m, a):
    a0, a1, a2 = _split3(a)
    return _dot(m, a0) + _dot(m, a1) + _dot(m, a2)


def _rms(x):
    return x * lax.rsqrt(jnp.mean(x * x, axis=-1, keepdims=True) + _EPS)


def _silu(x):
    return x * jax.nn.sigmoid(x)


def _ada_kernel(c_ref, w_ref, b_ref, o_ref):
    c = c_ref[...]
    cond = _silu(c)
    o_ref[0] = jnp.sum(w_ref[0] * cond, axis=0, keepdims=True) + b_ref[0]


def _ada(c, w_ada, b_ada):
    depth = w_ada.shape[0]
    n = depth * 6
    out = pl.pallas_call(
        _ada_kernel,
        grid=(n,),
        in_specs=[
            pl.BlockSpec((_D, 1), lambda j: (0, 0)),
            pl.BlockSpec((1, _D, _D), lambda j: (j // 6, 0, j % 6)),
            pl.BlockSpec((1, 1, _D), lambda j: (j, 0, 0)),
        ],
        out_specs=pl.BlockSpec((1, 1, _D), lambda j: (j, 0, 0)),
        out_shape=jax.ShapeDtypeStruct((n, 1, _D), _f32),
        name="ada_mod",
    )(c.reshape(_D, 1), w_ada, b_ada.reshape(n, 1, _D))
    return out.reshape(depth, 6, _D)


def _ffn_prologue(x1, mod, wr_ref, br_ref, x1_ref, hn2_ref, lg_ref):
    sh_f, sc_f = mod[3:4, :], mod[4:5, :]
    hn2 = _rms(x1) * (1.0 + sc_f) + sh_f
    x1_ref[...] = x1
    a0, a1, _ = _split3(hn2)
    hn2_ref[...] = a0
    w0, w1, _ = _split3(wr_ref[...])
    both = _dot(a0, jnp.concatenate([w0, w1], axis=1))
    lg_ref[...] = both[:, :_E] + both[:, _E:] + _dot(a1, w0) + br_ref[...]


def _mixer0_kernel(x_ref, mod_ref, win_ref, lb_ref, hgg_ref, rtg_ref, wout_ref, rot_a_ref, rot_b_ref,
                   wr_ref, br_ref, x1_ref, hn2_ref, lg_ref,
                   q_s, k_s, v_s, b_s, o_s, st_hg, st_rt):
    tb = x_ref.shape[0]
    nch = tb // _CH
    nsub = _CH // _SUB

    @pl.when(pl.program_id(0) == 0)
    def _():
        st_hg[...] = jnp.zeros_like(st_hg)
        st_rt[...] = jnp.zeros_like(st_rt)

    x = x_ref[...]
    mod = mod_ref[...]
    sh_m, sc_m, g_m = mod[0:1, :], mod[1:2, :], mod[2:3, :]
    hb = (_rms(x) * (1.0 + sc_m) + sh_m).astype(_bf16)
    width = _HEADS * _DH

    def sec(j):
        return _dot(hb, win_ref[:, j * width:(j + 1) * width])

    hq = sec(0)
    lb = lb_ref[...]
    fg = lb + (1.0 - lb) * jax.nn.sigmoid(sec(1))
    hi = sec(2)
    g = jnp.log(fg) * _LOG2E
    r_i = lax.broadcasted_iota(jnp.int32, (tb, tb), 0)
    c_i = lax.broadcasted_iota(jnp.int32, (tb, tb), 1)
    tri = jnp.where(((r_i >> 6) == (c_i >> 6)) & (c_i <= r_i), 1.0, 0.0).astype(_bf16)
    bcum = _dot_exact_lhs(tri, g)
    qh = _silu(hq)
    kk = 1.0 - fg
    for h in range(_HEADS):
        sl = slice(h * _DH, (h + 1) * _DH)
        q_s[h] = qh[:, sl]
        k_s[h] = kk[:, sl]
        v_s[h] = hi[:, sl]
        b_s[h] = bcum[:, sl]

    hgg = hgg_ref[...]
    row64 = lax.broadcasted_iota(jnp.int32, (_HEADS, _CH, _DH), 1)
    row16 = lax.broadcasted_iota(jnp.int32, (_HEADS * nsub, _SUB, 1), 1)
    lead16 = lax.broadcasted_iota(jnp.int32, (_HEADS * nsub, _SUB, _CH), 0)
    col16 = lax.broadcasted_iota(jnp.int32, (_HEADS * nsub, _SUB, _CH), 2) - _SUB * (lead16 & (nsub - 1))

    def chunk(c, carry):
        rows = pl.ds(pl.multiple_of(c * _CH, _CH), _CH)
        q = q_s[:, rows, :]
        k = k_s[:, rows, :]
        v = v_s[:, rows, :]
        b = b_s[:, rows, :]
        st = st_hg[...]
        vb = v.astype(_bf16)
        qe = (q * jnp.exp2(b)).astype(_bf16)
        o = jnp.einsum('htd,hed->hte', qe, st.astype(_bf16), preferred_element_type=_f32)
        blocks = [jnp.zeros((_HEADS, _SUB, _CH), _f32)]
        for i in range(1, nsub):
            ref = b[:, _SUB * i - 1:_SUB * i, :]
            qp = q[:, _SUB * i:_SUB * (i + 1), :] * jnp.exp2(b[:, _SUB * i:_SUB * (i + 1), :] - ref)
            kp = k * jnp.exp2(jnp.where(row64 < _SUB * i, ref - b, _NEG))
            blocks.append(jnp.einsum('htd,hsd->hts', qp.astype(_bf16), kp.astype(_bf16),
                                     preferred_element_type=_f32))
        a = jnp.concatenate(blocks, axis=1)
        q3 = q.reshape(_HEADS * nsub, _SUB, _DH)
        k3 = k.reshape(_HEADS * nsub, _SUB, _DH)
        b3 = b.reshape(_HEADS * nsub, _SUB, _DH)
        ad = jnp.zeros((_HEADS * nsub, _SUB, _CH), _f32)
        for s in range(_SUB):
            e = jnp.exp2(b3 - b3[:, s:s + 1, :])
            w = jnp.sum(q3 * e * k3[:, s:s + 1, :], axis=-1, keepdims=True)
            ad = jnp.where((col16 == s) & (row16 >= s), w, ad)
        a = a + ad.reshape(_HEADS, _CH, _CH)
        o = o + jnp.einsum('hts,hse->hte', a.astype(_bf16), vb, preferred_element_type=_f32)
        bl = b[:, _CH - 1:_CH, :]
        kd = (k * jnp.exp2(bl - b)).astype(_bf16)
        dec = jnp.exp2(bl)
        for h in range(_HEADS):
            st_hg[h] = st[h] * dec[h] + _dot(v[h].T.astype(_bf16), kd[h])
        o = o * lax.rsqrt(jnp.mean(o * o, axis=-1, keepdims=True) + _EPS) * hgg
        o_s[:, rows, :] = o
        return carry

    lax.fori_loop(0, nch, chunk, 0, unroll=True)
    hog = sec(3)
    cat_hg = jnp.concatenate([o_s[h] for h in range(_HEADS)], axis=1) * _silu(hog)

    rq = sec(4)
    rk = sec(5)
    rv = sec(6)
    cos_a, sin_a = rot_a_ref[0, 0:1, :], rot_a_ref[0, 1:2, :]
    cos_b, sin_b = rot_b_ref[0], rot_b_ref[1]
    first_half = lax.broadcasted_iota(jnp.int32, (1, _DH), 1) < _DH // 2
    cosf = cos_a * cos_b - sin_a * sin_b
    sinf = (sin_a * cos_b + cos_a * sin_b) * jnp.where(first_half, -1.0, 1.0)
    t_i = lax.broadcasted_iota(jnp.int32, (tb, tb), 0)
    s_i = lax.broadcasted_iota(jnp.int32, (tb, tb), 1)
    rel = (t_i - s_i).astype(_f32)
    causal = t_i >= s_i
    tcol = lax.broadcasted_iota(jnp.int32, (tb, 1), 0).astype(_f32)
    rtg = rtg_ref[...]
    outs = []
    for h in range(_HEADS):
        sl = slice(h * _DH, (h + 1) * _DH)
        lg = math.log(1.0 - 2.0 ** (-5.0 - h))
        xq = rq[:, sl]
        xk = rk[:, sl]
        q = xq * cosf + pltpu.roll(xq, _DH // 2, 1) * sinf
        k = (xk * cosf + pltpu.roll(xk, _DH // 2, 1) * sinf) * (_DH ** -0.5)
        v = rv[:, sl]
        dmat = jnp.where(causal, jnp.exp(jnp.where(causal, rel, 0.0) * lg), 0.0)
        scores = _dot_nt(q.astype(_bf16), k.astype(_bf16)) * dmat
        o = _dot(scores.astype(_bf16), v.astype(_bf16))
        st = st_rt[h]
        qd = q * jnp.exp((tcol + 1.0) * lg)
        o = o + _dot_nt(qd.astype(_bf16), st.astype(_bf16))
        kd = k * jnp.exp((tb - 1.0 - tcol) * lg)
        st_rt[h] = st * math.exp(tb * lg) + _dot(v.T.astype(_bf16), kd.astype(_bf16))
        o = o * lax.rsqrt(jnp.mean(o * o, axis=-1, keepdims=True) + _EPS) * rtg[h]
        outs.append(o)
    rg = sec(7)
    cat_rt = jnp.concatenate(outs, axis=1) * _silu(rg)

    cat = jnp.concatenate([cat_hg, cat_rt], axis=1).astype(_bf16)
    mix = _dot(cat, wout_ref[...])
    x1 = x + g_m * mix
    _ffn_prologue(x1, mod, wr_ref, br_ref, x1_ref, hn2_ref, lg_ref)


def _mixer0(x, mod, w_in, lb, hg_gain, ret_gain, w_out, w_router, b_router):
    s = x.shape[0]
    tb = min(_TB, s)
    rot_a, rot_b = _rotary_tables(s, tb)
    width = _HEADS * _DH
    full = lambda shape: pl.BlockSpec(shape, lambda i: (0,) * len(shape))
    rowblk = lambda w: pl.BlockSpec((tb, w), lambda i: (i, 0))
    return pl.pallas_call(
        _mixer0_kernel,
        grid=(s // tb,),
        in_specs=[
            rowblk(_D), full((6, _D)), full((_D, 8 * width)), full((1, width)),
            full((_HEADS, 1, _DH)), full((_HEADS, 1, _DH)), full((2 * width, _D)),
            pl.BlockSpec((1, 2, _DH), lambda i: (i, 0, 0)), full((2, tb, _DH)),
            full((_D, _E)), full((1, _E)),
        ],
        out_specs=[rowblk(_D), rowblk(_D), rowblk(_E)],
        out_shape=[jax.ShapeDtypeStruct((s, _D), _f32), jax.ShapeDtypeStruct((s, _D), _bf16),
                   jax.ShapeDtypeStruct((s, _E), _f32)],
        scratch_shapes=[pltpu.VMEM((_HEADS, tb, _DH), _f32)] * 5
        + [pltpu.VMEM((_HEADS, _DH, _DH), _f32)] * 2,
        compiler_params=pltpu.CompilerParams(dimension_semantics=("arbitrary",),
                                             vmem_limit_bytes=_VMEM_LIMIT),
        name="mixer_hgrn2_retention",
    )(x, mod, w_in.astype(_bf16), lb.reshape(1, width), hg_gain.reshape(_HEADS, 1, _DH),
      ret_gain.reshape(_HEADS, 1, _DH), w_out.astype(_bf16), rot_a, rot_b, w_router,
      b_router.reshape(1, _E))


def _pool_kernel(x_ref, mod_ref, pw_ref, pb_ref, ps_ref, wr_ref, br_ref,
                 x1_ref, hn2_ref, lg_ref, halo):
    tb = x_ref.shape[0]
    i = pl.program_id(0)

    @pl.when(i == 0)
    def _():
        halo[...] = jnp.zeros_like(halo)

    x = x_ref[...]
    mod = mod_ref[...]
    sh_m, sc_m, g_m = mod[0:1, :], mod[1:2, :], mod[2:3, :]
    hn = _rms(x) * (1.0 + sc_m) + sh_m
    ext = jnp.concatenate([halo[...], hn], axis=0)
    halo[...] = hn[tb - _HALO:, :]
    t = (i * tb + lax.broadcasted_iota(jnp.int32, (tb, 1), 0)).astype(_f32)
    gw = _D // len(_POOL_WINDOWS)
    ys = []
    for gi, w in enumerate(_POOL_WINDOWS):
        acc = ext[:, gi * gw:(gi + 1) * gw]
        shift = 1
        while shift < w:
            acc = acc + pltpu.roll(acc, shift, 0)
            shift *= 2
        win = acc[_HALO:, :]
        cnt = jnp.minimum(t + 1.0, float(w))
        p = win / cnt - hn[:, gi * gw:(gi + 1) * gw]
        ys.append(_dot(p.astype(_bf16), pw_ref[gi]) + pb_ref[gi])
    mix = jnp.concatenate(ys, axis=1) * ps_ref[...]
    x1 = x + g_m * mix
    _ffn_prologue(x1, mod, wr_ref, br_ref, x1_ref, hn2_ref, lg_ref)


def _pool_mixer(x, mod, pool_w, pool_b, pool_scale, w_router, b_router):
    s = x.shape[0]
    tb = min(_TBP, s)
    ng = len(_POOL_WINDOWS)
    gw = _D // ng
    full = lambda shape: pl.BlockSpec(shape, lambda i: (0,) * len(shape))
    rowblk = lambda w: pl.BlockSpec((tb, w), lambda i: (i, 0))
    return pl.pallas_call(
        _pool_kernel,
        grid=(s // tb,),
        in_specs=[rowblk(_D), full((6, _D)), full((ng, gw, gw)), full((ng, 1, gw)), full((1, _D)),
                  full((_D, _E)), full((1, _E))],
        out_specs=[rowblk(_D), rowblk(_D), rowblk(_E)],
        out_shape=[jax.ShapeDtypeStruct((s, _D), _f32), jax.ShapeDtypeStruct((s, _D), _bf16),
                   jax.ShapeDtypeStruct((s, _E), _f32)],
        scratch_shapes=[pltpu.VMEM((_HALO, _D), _f32)],
        compiler_params=pltpu.CompilerParams(dimension_semantics=("arbitrary",),
                                             vmem_limit_bytes=_VMEM_LIMIT),
        name="mixer_pool",
    )(x, mod, pool_w.astype(_bf16), pool_b.reshape(ng, 1, gw), pool_scale.reshape(1, _D),
      w_router, b_router.reshape(1, _E))


def _route_kernel(lg_ref, route_ref, route_t_ref, cnt_ref):
    tt = lg_ref.shape[0]
    vals = lg_ref[...]
    lane = lax.broadcasted_iota(jnp.int32, (tt, _E), 1).astype(_f32)
    tops, idxs, hots = [], [], []
    for _ in range(_K):
        m = jnp.max(vals, axis=-1, keepdims=True)
        idx = jnp.min(jnp.where(vals == m, lane, float(_E)), axis=-1, keepdims=True)
        hot = lane == idx
        vals = jnp.where(hot, -jnp.inf, vals)
        tops.append(m)
        idxs.append(idx)
        hots.append(hot)
    ex = [jnp.exp(m - tops[0]) for m in tops]
    den = ex[0] + ex[1] + ex[2] + ex[3]
    member = jnp.zeros((tt, _E), _f32)
    for hot in hots:
        member = member + jnp.where(hot, 1.0, 0.0)
    r_i = lax.broadcasted_iota(jnp.int32, (tt, tt), 0)
    c_i = lax.broadcasted_iota(jnp.int32, (tt, tt), 1)
    below = jnp.where(c_i < r_i, 1.0, 0.0).astype(_bf16)
    cum = _dot(below, member.astype(_bf16))
    cnt = jnp.sum(member, axis=0, keepdims=True)
    cnt_ref[0] = cnt
    r8 = jnp.floor((cnt + (_ALIGN - 1.0)) * (1.0 / _ALIGN)) * _ALIGN
    j_i = lax.broadcasted_iota(jnp.int32, (_E, _E), 0)
    e_i = lax.broadcasted_iota(jnp.int32, (_E, _E), 1)
    before = jnp.where(j_i < e_i, 1.0, 0.0).astype(_bf16)
    lo = _dot(jnp.broadcast_to(r8, (8, _E)).astype(_bf16), before)[0:1, :]
    row = cum + lo
    out_lane = lax.broadcasted_iota(jnp.int32, (tt, 128), 1)
    out = jnp.zeros((tt, 128), _f32)
    for k in range(_K):
        pos = jnp.sum(jnp.where(hots[k], row, 0.0), axis=-1, keepdims=True)
        out = jnp.where(out_lane == k, idxs[k], out)
        out = jnp.where(out_lane == _K + k, ex[k] / den, out)
        out = jnp.where(out_lane == 2 * _K + k, pos, out)
    route_ref[...] = out
    route_t_ref[...] = out.T[0:_RT_ROWS, :]


def _route(logits):
    s = logits.shape[0]
    tt = min(_TT, s)
    nt = s // tt
    return pl.pallas_call(
        _route_kernel,
        grid=(nt,),
        in_specs=[pl.BlockSpec((tt, _E), lambda i: (i, 0))],
        out_specs=[pl.BlockSpec((tt, 128), lambda i: (i, 0)),
                   pl.BlockSpec((_RT_ROWS, tt), lambda i: (0, i)),
                   pl.BlockSpec((1, 1, _E), lambda i: (i, 0, 0))],
        out_shape=[jax.ShapeDtypeStruct((s, 128), _f32),
                   jax.ShapeDtypeStruct((_RT_ROWS, s), _f32),
                   jax.ShapeDtypeStruct((nt, 1, _E), _f32)],
        compiler_params=pltpu.CompilerParams(dimension_semantics=("arbitrary",)),
        name="route_topk_rank",
    )(logits)


def _offsets_kernel(cnt_ref, lo_ref, goff_ref, nch_ref, tail_ref, tsum_ref):
    cnt = cnt_ref[...]
    ntp = cnt.shape[0]
    r8 = jnp.floor((cnt + (_ALIGN - 1.0)) * (1.0 / _ALIGN)) * _ALIGN
    j_i = lax.broadcasted_iota(jnp.int32, (_E, _E), 0)
    e_i = lax.broadcasted_iota(jnp.int32, (_E, _E), 1)
    before = jnp.where(j_i < e_i, 1.0, 0.0).astype(_bf16)

    def times_before(a):
        a0, a1, a2 = _split3(a)
        return _dot(a0, before) + _dot(a1, before) + _dot(a2, before)

    lo = times_before(r8)
    tot = jnp.sum(r8, axis=0, keepdims=True)
    cap = jnp.floor((tot + (_BLK - 1.0)) * (1.0 / _BLK)) * _BLK
    start = times_before(jnp.broadcast_to(cap, (8, _E)))[0:1, :]
    a_i = lax.broadcasted_iota(jnp.int32, (ntp, ntp), 0)
    b_i = lax.broadcasted_iota(jnp.int32, (ntp, ntp), 1)
    earlier = jnp.where(b_i < a_i, 1.0, 0.0).astype(_bf16)
    goff = start + _dot_exact_lhs(earlier, r8)
    lo_ref[...] = lo.astype(jnp.int32)
    goff_ref[...] = goff.astype(jnp.int32)
    nch_ref[...] = (r8 * (1.0 / _ALIGN)).astype(jnp.int32)
    trow = lax.broadcasted_iota(jnp.int32, (8, _E), 0)
    end = start + cap
    used_blocks = jnp.max(end, axis=-1, keepdims=True) * (1.0 / _BLK)
    tail = jnp.where(trow == 0, start + tot,
                     jnp.where(trow == 1, (cap - tot) * (1.0 / _ALIGN),
                               jnp.where(trow == 2, used_blocks,
                                         jnp.where(trow == 3, start * (1.0 / _BLK),
                                                   jnp.where(trow == 4, cap * (1.0 / _BLK), 0.0)))))
    tail_ref[...] = tail.astype(jnp.int32)
    tsum = jnp.sum(r8, axis=-1, keepdims=True)
    tsum_ref[...] = jnp.broadcast_to(tsum, (ntp, _E)).astype(jnp.int32)


def _offsets(cnt):
    ntp = cnt.shape[0]
    i32 = jnp.int32
    return pl.pallas_call(
        _offsets_kernel,
        out_shape=[jax.ShapeDtypeStruct((ntp, _E), i32),
                   jax.ShapeDtypeStruct((ntp, _E), i32), jax.ShapeDtypeStruct((ntp, _E), i32),
                   jax.ShapeDtypeStruct((8, _E), i32), jax.ShapeDtypeStruct((ntp, _E), i32)],
        name="route_offsets",
    )(cnt)


_HI_MASK = -65536


def _pack_pairs(x):
    half = x.shape[1] // 2
    bits = lax.bitcast_convert_type(x, jnp.int32)
    return (bits[:, :half] & _HI_MASK) | lax.shift_right_logical(bits[:, half:], 16)


def _unpack_pairs(p):
    hi = lax.bitcast_convert_type(p & _HI_MASK, _f32).astype(_bf16)
    lo = lax.bitcast_convert_type(lax.shift_left(p, 16), _f32).astype(_bf16)
    return hi, lo


def _for_each_part(lo_s, goff_s, nch_s, tile, fn):
    def per_expert(e, carry):
        rows = nch_s[tile * _E + e] * _ALIGN
        lo = lo_s[tile * _E + e]
        go = goff_s[tile * _E + e]

        def bulk(j, c):
            fn(lo + j * _BULK, go + j * _BULK, _BULK)
            return c

        lax.fori_loop(0, lax.shift_right_logical(rows, _BULK.bit_length() - 1), bulk, 0)
        size = _BULK // 2
        while size >= _ALIGN:
            @pl.when((rows & size) != 0)
            def _(size=size):
                off = rows & (-2 * size)
                fn(lo + off, go + off, size)
            size //= 2
        return carry
    lax.fori_loop(0, _E, per_expert, 0)


def _part_copy(src, src_row, dst, dst_row, size, sem):
    return pltpu.make_async_copy(src.at[pl.ds(pl.multiple_of(src_row, _ALIGN), size)],
                                 dst.at[pl.ds(pl.multiple_of(dst_row, _ALIGN), size)], sem)


def _wait_rows(total, limit, copy_of_size):
    size = _ALIGN
    while size * 2 <= limit:
        size *= 2
    while size >= _ALIGN:
        @pl.when((total & size) != 0)
        def _(size=size):
            copy_of_size(size).wait()
        size //= 2


def _zero_unused_blocks(zblk, dst_ref, first, sem):
    nb = dst_ref.shape[0] // _BLK

    def copy(b):
        return pltpu.make_async_copy(zblk, dst_ref.at[pl.ds(pl.multiple_of(b * _BLK, _BLK), _BLK)], sem)

    def fill(b, carry):
        copy(b).start()
        return carry

    def drain(b, carry):
        copy(b).wait()
        return carry

    lax.fori_loop(first, nb, fill, 0)
    lax.fori_loop(first, nb, drain, 0)


def _dispatch_kernel(lo_s, goff_s, nch_s, tail_s, tsum_s, hn_ref, route_t_ref, xs_ref,
                     buf, zblk, sem, *, nt):
    i = pl.program_id(0)
    tt = hn_ref.shape[0]
    cap = buf.shape[1]
    rt = route_t_ref[...]
    pos = [rt[2 * _K + k:2 * _K + k + 1, :] for k in range(_K)]

    slot = lax.rem(i, 2)
    tile_buf = buf.at[slot]
    fill_sem = sem.at[2]

    def wait_tile(tile, sl):
        _wait_rows(tsum_s[tile], cap,
                   lambda size: _part_copy(buf.at[sl], 0, xs_ref, 0, size, sem.at[sl]))

    @pl.when(i > 1)
    def _():
        wait_tile(i - 2, slot)

    rows = cap // _SORT_CHUNKS
    hn = hn_ref[...]
    for c in range(_SORT_CHUNKS):
        r_iota = (lax.broadcasted_iota(jnp.int32, (rows, tt), 0) + c * rows).astype(_f32)
        perm = jnp.zeros((rows, tt), _f32)
        for k in range(_K):
            perm = jnp.where(r_iota == pos[k], 1.0, perm)
        buf[slot, c * rows:(c + 1) * rows, :] = _pack_pairs(_dot(perm.astype(_bf16), hn))

    def start_part(local_row, global_row, size):
        _part_copy(tile_buf, local_row, xs_ref, global_row, size, sem.at[slot]).start()

    _for_each_part(lo_s, goff_s, nch_s, i, start_part)

    @pl.when(i == nt - 1)
    def _():
        zblk[...] = jnp.zeros_like(zblk)

        def tail_copy(dst_row):
            return _part_copy(zblk, 0, xs_ref, dst_row, _ALIGN, fill_sem)

        def fill(e, carry):
            def one(j, c):
                tail_copy(tail_s[e] + j * _ALIGN).start()
                return c
            return lax.fori_loop(0, tail_s[_E + e], one, carry)

        def drain(e, carry):
            def one(j, c):
                tail_copy(0).wait()
                return c
            return lax.fori_loop(0, tail_s[_E + e], one, carry)

        lax.fori_loop(0, _E, fill, 0)
        lax.fori_loop(0, _E, drain, 0)
        _zero_unused_blocks(zblk, xs_ref, tail_s[2 * _E], fill_sem)
        if nt > 1:
            wait_tile(i - 1, 1 - slot)
        wait_tile(i, slot)


def _dispatch(lo, goff, nch, tail, tsum, hn, route_t, n_rows):
    s = hn.shape[0]
    tt = min(_TT, s)
    nt = s // tt
    cap = tt * _K + _E * _ALIGN
    grid_spec = pltpu.PrefetchScalarGridSpec(
        num_scalar_prefetch=5,
        grid=(nt,),
        in_specs=[
            pl.BlockSpec((tt, _D), lambda i, *_: (i, 0)),
            pl.BlockSpec((_RT_ROWS, tt), lambda i, *_: (0, i)),
        ],
        out_specs=pl.BlockSpec(memory_space=pl.ANY),
        scratch_shapes=[pltpu.VMEM((2, cap, _D // 2), jnp.int32),
                        pltpu.VMEM((_BLK, _D // 2), jnp.int32), pltpu.SemaphoreType.DMA((3,))],
    )
    return pl.pallas_call(
        functools.partial(_dispatch_kernel, nt=nt),
        grid_spec=grid_spec,
        out_shape=jax.ShapeDtypeStruct((n_rows, _D // 2), jnp.int32),
        compiler_params=pltpu.CompilerParams(dimension_semantics=("arbitrary",),
                                             vmem_limit_bytes=_VMEM_LIMIT),
        name="moe_dispatch",
    )(lo, goff, nch, tail, tsum, hn, route_t)


def _expert_kernel(tail_s, wgu_ref, bgu_ref, wd_ref, bd_ref, xs_ref, ys_ref,
                   wgu_c, wd_c, xin, yout, zblk, sem_in, sem_out):
    e = pl.program_id(0)
    used = tail_s[2 * _E]
    first = tail_s[3 * _E + e]
    nblk = tail_s[4 * _E + e]
    half = _D // 2

    def rows(g):
        return pl.ds(pl.multiple_of(g * _BLK, _BLK), _BLK)

    def in_copy(g, slot):
        return pltpu.make_async_copy(xs_ref.at[rows(g)], xin.at[slot], sem_in.at[slot])

    def out_copy(g, slot):
        return pltpu.make_async_copy(yout.at[slot], ys_ref.at[rows(g)], sem_out.at[slot])

    def slot_of(g):
        return lax.rem(g, _RING)

    @pl.when(e == 0)
    def _():
        for g in range(_AHEAD):
            @pl.when(g < used)
            def _(g=g):
                in_copy(g, g).start()

    @pl.when(nblk > 0)
    def _():
        step = 128
        for r in range(0, _D, step):
            wgu_c[r:r + step, :] = wgu_ref[0, 0, r:r + step, :].astype(_bf16)
        for r in range(0, _F, step):
            wd_c[r:r + step, :] = wd_ref[0, 0, r:r + step, :].astype(_bf16)

    def unit(g, n):
        for d in range(n):
            in_copy(g + d, slot_of(g + d)).wait()
        for d in range(_AHEAD, _AHEAD + n):
            @pl.when(g + d < used)
            def _(d=d):
                in_copy(g + d, slot_of(g + d)).start()
        parts = [_unpack_pairs(xin[slot_of(g + d)]) for d in range(n)]
        x_hi = jnp.concatenate([p[0] for p in parts], axis=0)
        x_lo = jnp.concatenate([p[1] for p in parts], axis=0)
        gu = _dot(x_hi, wgu_c[:half, :]) + _dot(x_lo, wgu_c[half:, :]) + bgu_ref[0, 0]
        x_glu = jnp.minimum(gu[:, :_F], _SWIGLU_LIMIT)
        x_lin = jnp.clip(gu[:, _F:], -_SWIGLU_LIMIT, _SWIGLU_LIMIT)
        act = x_glu * jax.nn.sigmoid(_SWIGLU_ALPHA * x_glu) * (x_lin + 1.0)
        y = _dot(act.astype(_bf16), wd_c[...]) + bd_ref[0, 0]
        packed = _pack_pairs(y.astype(_bf16).astype(_f32))
        for d in range(n):
            @pl.when(g + d >= _RING)
            def _(d=d):
                out_copy(g + d - _RING, slot_of(g + d)).wait()
            yout[slot_of(g + d)] = packed[d * _BLK:(d + 1) * _BLK, :]
            out_copy(g + d, slot_of(g + d)).start()

    def widest(j, carry):
        unit(first + _AHEAD * j, _AHEAD)
        return carry

    lax.fori_loop(0, nblk // _AHEAD, widest, 0)
    n = _AHEAD // 2
    while n >= 1:
        @pl.when((nblk & n) != 0)
        def _(n=n):
            unit(first + (nblk & (-2 * n)), n)
        n //= 2

    @pl.when(e == _E - 1)
    def _():
        for d in range(1, _RING + 1):
            @pl.when(used >= d)
            def _(d=d):
                out_copy(used - d, slot_of(used - d)).wait()

        zblk[...] = jnp.zeros_like(zblk)
        _zero_unused_blocks(zblk, ys_ref, used, sem_out.at[0])


def _experts(tail, xs, w_gu, b_gu, w_down, b_down, layer):
    n_rows = xs.shape[0]
    wsel = lambda e, *_: (layer, e, 0, 0)
    grid_spec = pltpu.PrefetchScalarGridSpec(
        num_scalar_prefetch=1,
        grid=(_E,),
        in_specs=[
            pl.BlockSpec((1, 1, _D, 2 * _F), wsel),
            pl.BlockSpec((1, 1, 1, 2 * _F), wsel),
            pl.BlockSpec((1, 1, _F, _D), wsel),
            pl.BlockSpec((1, 1, 1, _D), wsel),
            pl.BlockSpec(memory_space=pl.ANY),
        ],
        out_specs=pl.BlockSpec(memory_space=pl.ANY),
        scratch_shapes=[pltpu.VMEM((_D, 2 * _F), _bf16), pltpu.VMEM((_F, _D), _bf16),
                        pltpu.VMEM((_RING, _BLK, _D // 2), jnp.int32),
                        pltpu.VMEM((_RING, _BLK, _D // 2), jnp.int32),
                        pltpu.VMEM((_BLK, _D // 2), jnp.int32),
                        pltpu.SemaphoreType.DMA((_RING,)), pltpu.SemaphoreType.DMA((_RING,))],
    )
    depth = w_gu.shape[0]
    return pl.pallas_call(
        _expert_kernel,
        grid_spec=grid_spec,
        out_shape=jax.ShapeDtypeStruct((n_rows, _D // 2), jnp.int32),
        compiler_params=pltpu.CompilerParams(dimension_semantics=("arbitrary",),
                                             vmem_limit_bytes=_VMEM_LIMIT),
        name="moe_experts",
    )(tail, w_gu, b_gu.reshape(depth, _E, 1, 2 * _F), w_down, b_down.reshape(depth, _E, 1, _D), xs)


def _combine_kernel(lo_s, goff_s, nch_s, tsum_s, route_ref, x1_ref, mod_ref, fin_ref, ys_ref,
                    out_ref, ybuf, sem, *, final, nt):
    i = pl.program_id(0)
    tt = x1_ref.shape[0]
    cap = ybuf.shape[1]
    slot = lax.rem(i, 2)

    def fetch(tile, dst):
        def start_part(local_row, global_row, size):
            _part_copy(ys_ref, global_row, dst, local_row, size, sem).start()
        _for_each_part(lo_s, goff_s, nch_s, tile, start_part)

    @pl.when(i == 0)
    def _():
        ybuf[...] = jnp.zeros_like(ybuf)
        fetch(0, ybuf.at[0])

    _wait_rows(tsum_s[i], cap, lambda size: _part_copy(ys_ref, 0, ybuf.at[slot], 0, size, sem))

    @pl.when(i + 1 < nt)
    def _():
        fetch(i + 1, ybuf.at[1 - slot])

    route = route_ref[...]
    pos = [route[:, 2 * _K + k:2 * _K + k + 1] for k in range(_K)]
    gate = [route[:, _K + k:_K + k + 1] for k in range(_K)]

    cols = cap // _SORT_CHUNKS
    moe_hi = jnp.zeros((tt, _D // 2), _f32)
    moe_lo = jnp.zeros((tt, _D // 2), _f32)
    for c in range(_SORT_CHUNKS):
        lane_r = (lax.broadcasted_iota(jnp.int32, (tt, cols), 1) + c * cols).astype(_f32)
        wmat = jnp.zeros((tt, cols), _f32)
        for k in range(_K):
            wmat = jnp.where(lane_r == pos[k], gate[k], wmat)
        wmat = wmat.astype(_bf16)
        y_hi, y_lo = _unpack_pairs(ybuf[slot, c * cols:(c + 1) * cols, :])
        moe_hi = moe_hi + _dot(wmat, y_hi)
        moe_lo = moe_lo + _dot(wmat, y_lo)
    moe = jnp.concatenate([moe_hi, moe_lo], axis=1)
    x2 = x1_ref[...] + mod_ref[5:6, :] * moe
    if final:
        x2 = _rms(x2) * fin_ref[...]
    out_ref[...] = x2


def _combine(lo, goff, nch, tsum, route, x1, mod, final_norm, ys, final):
    s = x1.shape[0]
    tt = min(_TT, s)
    nt = s // tt
    cap = tt * _K + _E * _ALIGN
    grid_spec = pltpu.PrefetchScalarGridSpec(
        num_scalar_prefetch=4,
        grid=(nt,),
        in_specs=[
            pl.BlockSpec((tt, 128), lambda i, *_: (i, 0)),
            pl.BlockSpec((tt, _D), lambda i, *_: (i, 0)),
            pl.BlockSpec((6, _D), lambda i, *_: (0, 0)),
            pl.BlockSpec((1, _D), lambda i, *_: (0, 0)),
            pl.BlockSpec(memory_space=pl.ANY),
        ],
        out_specs=pl.BlockSpec((tt, _D), lambda i, *_: (i, 0)),
        scratch_shapes=[pltpu.VMEM((2, cap, _D // 2), jnp.int32), pltpu.SemaphoreType.DMA(())],
    )
    return pl.pallas_call(
        functools.partial(_combine_kernel, final=final, nt=nt),
        grid_spec=grid_spec,
        out_shape=jax.ShapeDtypeStruct((s, _D), _f32),
        compiler_params=pltpu.CompilerParams(dimension_semantics=("arbitrary",),
                                             vmem_limit_bytes=_VMEM_LIMIT),
        name="moe_combine",
    )(lo, goff, nch, tsum, route, x1, mod, final_norm.reshape(1, _D), ys)


def _moe(x1, hn2, logits, mod, final_norm, w_gu, b_gu, w_down, b_down, layer, final):
    s = x1.shape[0]
    tt = min(_TT, s)
    nt = s // tt
    ntp = -(-nt // 8) * 8
    n_rows = -(-(s * _K + (_ALIGN - 1) * nt * _E) // _BLK) * _BLK + _E * _BLK
    route, route_t, cnt = _route(logits)
    cnt = jnp.pad(cnt.reshape(nt, _E), ((0, ntp - nt), (0, 0)))
    lo, goff, nch, tail, tsum = _offsets(cnt)
    lo_s, goff_s, nch_s = (a[:nt].reshape(nt * _E) for a in (lo, goff, nch))
    tail_s = tail[:5].reshape(5 * _E)
    tsum_s = tsum[:nt, 0]
    xs = _dispatch(lo_s, goff_s, nch_s, tail_s, tsum_s, hn2, route_t, n_rows)
    ys = _experts(tail_s, xs, w_gu, b_gu, w_down, b_down, layer)
    return _combine(lo_s, goff_s, nch_s, tsum_s, route, x1, mod, final_norm, ys, final)


def _rotary_tables(s, tb):
    half = _DH // 2
    inv = (np.float32(10000.0) ** (-np.arange(half, dtype=np.float32) / np.float32(half)))
    inv = inv.astype(np.float64)

    def cos_sin(positions):
        ang = positions.astype(np.float64)[:, None] * inv[None, :]
        cos = np.concatenate([np.cos(ang)] * 2, axis=1)
        sin = np.concatenate([np.sin(ang)] * 2, axis=1)
        return cos.astype(np.float32), sin.astype(np.float32)

    cos_a, sin_a = cos_sin(np.arange(s // tb) * tb)
    cos_b, sin_b = cos_sin(np.arange(tb))
    return (jnp.asarray(np.stack([cos_a, sin_a], axis=1)), jnp.asarray(np.stack([cos_b, sin_b], axis=0)))


def _lower_bounds_kernel(p_ref, o_ref):
    p = p_ref[...]
    n = p.shape[0]
    m = p[0:1, :]
    for j in range(1, n):
        m = jnp.maximum(m, p[j:j + 1, :])
    ex = [jnp.exp(p[j:j + 1, :] - m) for j in range(n)]
    den = ex[0]
    for j in range(1, n):
        den = den + ex[j]
    acc = jnp.zeros_like(den)
    rows = []
    for j in range(n):
        acc = acc + ex[j] / den
        rows.append(acc)
    o_ref[...] = jnp.concatenate(rows, axis=0)


def _lower_bounds(p):
    return pl.pallas_call(
        _lower_bounds_kernel,
        out_shape=jax.ShapeDtypeStruct(p.shape, _f32),
        name="hgrn2_lower_bounds",
    )(p)


def kernel(x, c, w_ada, b_ada, w_in, hg_lower_bounds, hg_norm, ret_norm, w_out, pool_w, pool_b,
           pool_scale, w_router, b_router, w_gu, b_gu, w_down, b_down, final_norm):
    b, s, d = x.shape
    assert b == 1 and d == _D
    depth = w_ada.shape[0]
    assert depth == 2
    xs = x.reshape(s, d)
    mod = _ada(c, w_ada, b_ada)
    lb_all = _lower_bounds(hg_lower_bounds)
    x1, hn2, logits = _mixer0(xs, mod[0], w_in[0], lb_all[0], hg_norm[0], ret_norm[0], w_out[0],
                              w_router[0], b_router[0])
    xs = _moe(x1, hn2, logits, mod[0], final_norm, w_gu, b_gu, w_down, b_down, 0, False)
    x1, hn2, logits = _pool_mixer(xs, mod[1], pool_w[0], pool_b[0], pool_scale[0], w_router[1],
                                  b_router[1])
    xs = _moe(x1, hn2, logits, mod[1], final_norm, w_gu, b_gu, w_down, b_down, 1, True)
    return xs.reshape(b, s, d)
```

```python
import functools
import math

import jax
import jax.numpy as jnp
import numpy as np
from jax import lax
from jax.experimental import pallas as pl
from jax.experimental.pallas import tpu as pltpu

_D = 1024
_E = 32
_K = 4
_F = 1024
_HEADS = 4
_DH = 128
_CH = 64
_SUB = 16
_EPS = 1e-6
_BLK = 256
_TB = 256
_TBP = 512
_TT = 512
_ALIGN = 8
_RT_ROWS = 16
_BULK = 64
_SORT_CHUNKS = 3
_AHEAD = 4
_RING = 2 * _AHEAD
_NEG = -1e30
_LOG2E = 1.4426950408889634
_SWIGLU_LIMIT = 7.0
_SWIGLU_ALPHA = 1.702
_POOL_WINDOWS = (2, 4, 8, 16)
_HALO = 16
_VMEM_LIMIT = 56 * 1024 * 1024

_f32 = jnp.float32
_bf16 = jnp.bfloat16


def _dot(a, b):
    return jnp.dot(a, b, preferred_element_type=_f32)


def _dot_nt(a, b):
    return lax.dot_general(a, b, (((1,), (1,)), ((), ())), preferred_element_type=_f32)


def _split3(a):
    a0 = a.astype(_bf16)
    r1 = a - a0.astype(_f32)
    a1 = r1.astype(_bf16)
    a2 = (r1 - a1.astype(_f32)).astype(_bf16)
    return a0, a1, a2


def _dot_exact_lhs(m, a):
    a0, a1, a2 = _split3(a)
    return _dot(m, a0) + _dot(m, a1) + _dot(m, a2)


def _rms(x):
    return x * lax.rsqrt(jnp.mean(x * x, axis=-1, keepdims=True) + _EPS)


def _silu(x):
    return x * jax.nn.sigmoid(x)


def _ada_kernel(c_ref, w_ref, b_ref, o_ref):
    c = c_ref[...]
    cond = _silu(c)
    o_ref[0] = jnp.sum(w_ref[0] * cond, axis=0, keepdims=True) + b_ref[0]


def _ada(c, w_ada, b_ada):
    depth = w_ada.shape[0]
    n = depth * 6
    out = pl.pallas_call(
        _ada_kernel,
        grid=(n,),
        in_specs=[
            pl.BlockSpec((_D, 1), lambda j: (0, 0)),
            pl.BlockSpec((1, _D, _D), lambda j: (j // 6, 0, j % 6)),
            pl.BlockSpec((1, 1, _D), lambda j: (j, 0, 0)),
        ],
        out_specs=pl.BlockSpec((1, 1, _D), lambda j: (j, 0, 0)),
        out_shape=jax.ShapeDtypeStruct((n, 1, _D), _f32),
        name="ada_mod",
    )(c.reshape(_D, 1), w_ada, b_ada.reshape(n, 1, _D))
    return out.reshape(depth, 6, _D)


def _ffn_prologue(x1, mod, wr_ref, br_ref, x1_ref, hn2_ref, lg_ref):
    sh_f, sc_f = mod[3:4, :], mod[4:5, :]
    hn2 = _rms(x1) * (1.0 + sc_f) + sh_f
    x1_ref[...] = x1
    a0, a1, _ = _split3(hn2)
    hn2_ref[...] = a0
    w0, w1, _ = _split3(wr_ref[...])
    both = _dot(a0, jnp.concatenate([w0, w1], axis=1))
    lg_ref[...] = (both[:, :_E] + both[:, _E:] + _dot(a1, w0) + br_ref[...]).T


def _mixer0_kernel(x_ref, mod_ref, win_ref, lb_ref, hgg_ref, rtg_ref, wout_ref, rot_a_ref, rot_b_ref,
                   tri_ref, dmat_ref, wr_ref, br_ref, x1_ref, hn2_ref, lg_ref,
                   q_s, k_s, v_s, b_s, o_s, st_hg, st_rt):
    tb = x_ref.shape[0]
    nch = tb // _CH
    nsub = _CH // _SUB

    @pl.when(pl.program_id(0) == 0)
    def _():
        st_hg[...] = jnp.zeros_like(st_hg)
        st_rt[...] = jnp.zeros_like(st_rt)

    x = x_ref[...]
    mod = mod_ref[...]
    sh_m, sc_m, g_m = mod[0:1, :], mod[1:2, :], mod[2:3, :]
    hb = (_rms(x) * (1.0 + sc_m) + sh_m).astype(_bf16)
    width = _HEADS * _DH

    def sec(j):
        return _dot(hb, win_ref[:, j * width:(j + 1) * width])

    hq = sec(0)
    lb = lb_ref[...]
    fg = lb + (1.0 - lb) * jax.nn.sigmoid(sec(1))
    hi = sec(2)
    g = jnp.log(fg) * _LOG2E
    bcum = _dot_exact_lhs(tri_ref[...], g)
    qh = _silu(hq)
    kk = 1.0 - fg
    for h in range(_HEADS):
        sl = slice(h * _DH, (h + 1) * _DH)
        q_s[h] = qh[:, sl]
        k_s[h] = kk[:, sl]
        v_s[h] = hi[:, sl]
        b_s[h] = bcum[:, sl]

    hgg = hgg_ref[...]
    row64 = lax.broadcasted_iota(jnp.int32, (_HEADS, _CH, _DH), 1)
    row16 = lax.broadcasted_iota(jnp.int32, (_HEADS * nsub, _SUB, 1), 1)
    lead16 = lax.broadcasted_iota(jnp.int32, (_HEADS * nsub, _SUB, _CH), 0)
    col16 = lax.broadcasted_iota(jnp.int32, (_HEADS * nsub, _SUB, _CH), 2) - _SUB * (lead16 & (nsub - 1))

    def chunk(c, carry):
        rows = pl.ds(pl.multiple_of(c * _CH, _CH), _CH)
        q = q_s[:, rows, :]
        k = k_s[:, rows, :]
        v = v_s[:, rows, :]
        b = b_s[:, rows, :]
        st = st_hg[...]
        vb = v.astype(_bf16)
        qe = (q * jnp.exp2(b)).astype(_bf16)
        o = jnp.einsum('htd,hed->hte', qe, st.astype(_bf16), preferred_element_type=_f32)
        blocks = [jnp.zeros((_HEADS, _SUB, _CH), _f32)]
        for i in range(1, nsub):
            ref = b[:, _SUB * i - 1:_SUB * i, :]
            qp = q[:, _SUB * i:_SUB * (i + 1), :] * jnp.exp2(b[:, _SUB * i:_SUB * (i + 1), :] - ref)
            kp = k * jnp.exp2(jnp.where(row64 < _SUB * i, ref - b, _NEG))
            blocks.append(jnp.einsum('htd,hsd->hts', qp.astype(_bf16), kp.astype(_bf16),
                                     preferred_element_type=_f32))
        a = jnp.concatenate(blocks, axis=1)
        q3 = q.reshape(_HEADS * nsub, _SUB, _DH)
        k3 = k.reshape(_HEADS * nsub, _SUB, _DH)
        b3 = b.reshape(_HEADS * nsub, _SUB, _DH)
        ad = jnp.zeros((_HEADS * nsub, _SUB, _CH), _f32)
        for s in range(_SUB):
            e = jnp.exp2(b3 - b3[:, s:s + 1, :])
            w = jnp.sum(q3 * e * k3[:, s:s + 1, :], axis=-1, keepdims=True)
            ad = jnp.where((col16 == s) & (row16 >= s), w, ad)
        a = a + ad.reshape(_HEADS, _CH, _CH)
        o = o + jnp.einsum('hts,hse->hte', a.astype(_bf16), vb, preferred_element_type=_f32)
        bl = b[:, _CH - 1:_CH, :]
        kd = (k * jnp.exp2(bl - b)).astype(_bf16)
        dec = jnp.exp2(bl)
        for h in range(_HEADS):
            st_hg[h] = st[h] * dec[h] + _dot(v[h].T.astype(_bf16), kd[h])
        o = o * lax.rsqrt(jnp.mean(o * o, axis=-1, keepdims=True) + _EPS) * hgg
        o_s[:, rows, :] = o
        return carry

    lax.fori_loop(0, nch, chunk, 0, unroll=True)
    hog = sec(3)
    cat_hg = jnp.concatenate([o_s[h] for h in range(_HEADS)], axis=1) * _silu(hog)

    rq = sec(4)
    rk = sec(5)
    rv = sec(6)
    cos_a, sin_a = rot_a_ref[0, 0:1, :], rot_a_ref[0, 1:2, :]
    cos_b, sin_b = rot_b_ref[0], rot_b_ref[1]
    first_half = lax.broadcasted_iota(jnp.int32, (1, _DH), 1) < _DH // 2
    cosf = cos_a * cos_b - sin_a * sin_b
    sinf = (sin_a * cos_b + cos_a * sin_b) * jnp.where(first_half, -1.0, 1.0)
    tcol = lax.broadcasted_iota(jnp.int32, (tb, 1), 0).astype(_f32)
    rtg = rtg_ref[...]
    outs = []
    for h in range(_HEADS):
        sl = slice(h * _DH, (h + 1) * _DH)
        lg = math.log(1.0 - 2.0 ** (-5.0 - h))
        xq = rq[:, sl]
        xk = rk[:, sl]
        q = xq * cosf + pltpu.roll(xq, _DH // 2, 1) * sinf
        k = (xk * cosf + pltpu.roll(xk, _DH // 2, 1) * sinf) * (_DH ** -0.5)
        v = rv[:, sl]
        scores = _dot_nt(q.astype(_bf16), k.astype(_bf16)) * dmat_ref[h]
        o = _dot(scores.astype(_bf16), v.astype(_bf16))
        st = st_rt[h]
        qd = q * jnp.exp((tcol + 1.0) * lg)
        o = o + _dot_nt(qd.astype(_bf16), st.astype(_bf16))
        kd = k * jnp.exp((tb - 1.0 - tcol) * lg)
        st_rt[h] = st * math.exp(tb * lg) + _dot(v.T.astype(_bf16), kd.astype(_bf16))
        o = o * lax.rsqrt(jnp.mean(o * o, axis=-1, keepdims=True) + _EPS) * rtg[h]
        outs.append(o)
    rg = sec(7)
    cat_rt = jnp.concatenate(outs, axis=1) * _silu(rg)

    cat = jnp.concatenate([cat_hg, cat_rt], axis=1).astype(_bf16)
    mix = _dot(cat, wout_ref[...])
    x1 = x + g_m * mix
    _ffn_prologue(x1, mod, wr_ref, br_ref, x1_ref, hn2_ref, lg_ref)


def _mixer0(x, mod, w_in, lb, hg_gain, ret_gain, w_out, w_router, b_router):
    s = x.shape[0]
    tb = min(_TB, s)
    rot_a, rot_b = _rotary_tables(s, tb)
    tri, dmat = _decay_tables(tb)
    width = _HEADS * _DH
    full = lambda shape: pl.BlockSpec(shape, lambda i: (0,) * len(shape))
    rowblk = lambda w: pl.BlockSpec((tb, w), lambda i: (i, 0))
    return pl.pallas_call(
        _mixer0_kernel,
        grid=(s // tb,),
        in_specs=[
            rowblk(_D), full((6, _D)), full((_D, 8 * width)), full((1, width)),
            full((_HEADS, 1, _DH)), full((_HEADS, 1, _DH)), full((2 * width, _D)),
            pl.BlockSpec((1, 2, _DH), lambda i: (i, 0, 0)), full((2, tb, _DH)),
            full((tb, tb)), full((_HEADS, tb, tb)), full((_D, _E)), full((1, _E)),
        ],
        out_specs=[rowblk(_D), rowblk(_D), pl.BlockSpec((_E, tb), lambda i: (0, i))],
        out_shape=[jax.ShapeDtypeStruct((s, _D), _f32), jax.ShapeDtypeStruct((s, _D), _bf16),
                   jax.ShapeDtypeStruct((_E, s), _f32)],
        scratch_shapes=[pltpu.VMEM((_HEADS, tb, _DH), _f32)] * 5
        + [pltpu.VMEM((_HEADS, _DH, _DH), _f32)] * 2,
        compiler_params=pltpu.CompilerParams(dimension_semantics=("arbitrary",),
                                             vmem_limit_bytes=_VMEM_LIMIT),
        name="mixer_hgrn2_retention",
    )(x, mod, w_in.astype(_bf16), lb.reshape(1, width), hg_gain.reshape(_HEADS, 1, _DH),
      ret_gain.reshape(_HEADS, 1, _DH), w_out.astype(_bf16), rot_a, rot_b,
      tri, dmat, w_router,
      b_router.reshape(1, _E))


def _pool_kernel(x_ref, mod_ref, pw_ref, pb_ref, ps_ref, wr_ref, br_ref,
                 x1_ref, hn2_ref, lg_ref, halo):
    tb = x_ref.shape[0]
    i = pl.program_id(0)

    @pl.when(i == 0)
    def _():
        halo[...] = jnp.zeros_like(halo)

    x = x_ref[...]
    mod = mod_ref[...]
    sh_m, sc_m, g_m = mod[0:1, :], mod[1:2, :], mod[2:3, :]
    hn = _rms(x) * (1.0 + sc_m) + sh_m
    ext = jnp.concatenate([halo[...], hn], axis=0)
    halo[...] = hn[tb - _HALO:, :]
    t = (i * tb + lax.broadcasted_iota(jnp.int32, (tb, 1), 0)).astype(_f32)
    gw = _D // len(_POOL_WINDOWS)
    ys = []
    for gi, w in enumerate(_POOL_WINDOWS):
        acc = ext[:, gi * gw:(gi + 1) * gw]
        shift = 1
        while shift < w:
            acc = acc + pltpu.roll(acc, shift, 0)
            shift *= 2
        win = acc[_HALO:, :]
        cnt = jnp.minimum(t + 1.0, float(w))
        p = win / cnt - hn[:, gi * gw:(gi + 1) * gw]
        ys.append(_dot(p.astype(_bf16), pw_ref[gi]) + pb_ref[gi])
    mix = jnp.concatenate(ys, axis=1) * ps_ref[...]
    x1 = x + g_m * mix
    _ffn_prologue(x1, mod, wr_ref, br_ref, x1_ref, hn2_ref, lg_ref)


def _pool_mixer(x, mod, pool_w, pool_b, pool_scale, w_router, b_router):
    s = x.shape[0]
    tb = min(_TBP, s)
    ng = len(_POOL_WINDOWS)
    gw = _D // ng
    full = lambda shape: pl.BlockSpec(shape, lambda i: (0,) * len(shape))
    rowblk = lambda w: pl.BlockSpec((tb, w), lambda i: (i, 0))
    return pl.pallas_call(
        _pool_kernel,
        grid=(s // tb,),
        in_specs=[rowblk(_D), full((6, _D)), full((ng, gw, gw)), full((ng, 1, gw)), full((1, _D)),
                  full((_D, _E)), full((1, _E))],
        out_specs=[rowblk(_D), rowblk(_D), pl.BlockSpec((_E, tb), lambda i: (0, i))],
        out_shape=[jax.ShapeDtypeStruct((s, _D), _f32), jax.ShapeDtypeStruct((s, _D), _bf16),
                   jax.ShapeDtypeStruct((_E, s), _f32)],
        scratch_shapes=[pltpu.VMEM((_HALO, _D), _f32)],
        compiler_params=pltpu.CompilerParams(dimension_semantics=("arbitrary",),
                                             vmem_limit_bytes=_VMEM_LIMIT),
        name="mixer_pool",
    )(x, mod, pool_w.astype(_bf16), pool_b.reshape(ng, 1, gw), pool_scale.reshape(1, _D),
      w_router, b_router.reshape(1, _E))


def _route_kernel(lg_ref, route_ref, route_t_ref, cnt_ref):
    tt = lg_ref.shape[1]
    vals = lg_ref[...]
    sub = lax.broadcasted_iota(jnp.int32, (_E, tt), 0).astype(_f32)
    tops, idxs, hots = [], [], []
    for _ in range(_K):
        m = jnp.max(vals, axis=0, keepdims=True)
        idx = jnp.min(jnp.where(vals == m, sub, float(_E)), axis=0, keepdims=True)
        hot = sub == idx
        vals = jnp.where(hot, -jnp.inf, vals)
        tops.append(m)
        idxs.append(idx)
        hots.append(hot)
    ex = [jnp.exp(m - tops[0]) for m in tops]
    den = ex[0] + ex[1] + ex[2] + ex[3]
    member = jnp.zeros((_E, tt), _f32)
    for hot in hots:
        member = member + jnp.where(hot, 1.0, 0.0)
    member_b = member.astype(_bf16)
    s_i = lax.broadcasted_iota(jnp.int32, (tt, tt), 0)
    t_i = lax.broadcasted_iota(jnp.int32, (tt, tt), 1)
    earlier = jnp.where(s_i < t_i, 1.0, 0.0).astype(_bf16)
    cum = _dot(member_b, earlier)
    cnt_ref[0] = _dot_nt(jnp.ones((8, tt), _bf16), member_b)[0:1, :]
    cnt = jnp.sum(member, axis=1, keepdims=True)
    r8 = jnp.floor((cnt + (_ALIGN - 1.0)) * (1.0 / _ALIGN)) * _ALIGN
    e_i = lax.broadcasted_iota(jnp.int32, (_E, _E), 0)
    j_i = lax.broadcasted_iota(jnp.int32, (_E, _E), 1)
    before = jnp.where(j_i < e_i, 1.0, 0.0).astype(_bf16)
    lo = _dot(before, jnp.broadcast_to(r8, (_E, 128)).astype(_bf16))[:, 0:1]
    row = cum + lo
    out_row = lax.broadcasted_iota(jnp.int32, (_RT_ROWS, tt), 0)
    out = jnp.zeros((_RT_ROWS, tt), _f32)
    for k in range(_K):
        pos = jnp.sum(jnp.where(hots[k], row, 0.0), axis=0, keepdims=True)
        out = jnp.where(out_row == k, idxs[k], out)
        out = jnp.where(out_row == _K + k, ex[k] / den, out)
        out = jnp.where(out_row == 2 * _K + k, pos, out)
    route_t_ref[...] = out
    route_ref[...] = jnp.zeros_like(route_ref)
    route_ref[:, 0:_RT_ROWS] = out.T


def _route(logits_t):
    s = logits_t.shape[1]
    tt = min(_TT, s)
    nt = s // tt
    return pl.pallas_call(
        _route_kernel,
        grid=(nt,),
        in_specs=[pl.BlockSpec((_E, tt), lambda i: (0, i))],
        out_specs=[pl.BlockSpec((tt, 128), lambda i: (i, 0)),
                   pl.BlockSpec((_RT_ROWS, tt), lambda i: (0, i)),
                   pl.BlockSpec((1, 1, _E), lambda i: (i, 0, 0))],
        out_shape=[jax.ShapeDtypeStruct((s, 128), _f32),
                   jax.ShapeDtypeStruct((_RT_ROWS, s), _f32),
                   jax.ShapeDtypeStruct((nt, 1, _E), _f32)],
        compiler_params=pltpu.CompilerParams(dimension_semantics=("arbitrary",)),
        name="route_topk_rank",
    )(logits_t)


def _offsets_kernel(cnt_ref, lo_ref, goff_ref, nch_ref, tail_ref, tsum_ref):
    cnt = cnt_ref[...]
    ntp = cnt.shape[0]
    r8 = jnp.floor((cnt + (_ALIGN - 1.0)) * (1.0 / _ALIGN)) * _ALIGN
    j_i = lax.broadcasted_iota(jnp.int32, (_E, _E), 0)
    e_i = lax.broadcasted_iota(jnp.int32, (_E, _E), 1)
    before = jnp.where(j_i < e_i, 1.0, 0.0).astype(_bf16)

    def times_before(a):
        a0, a1, a2 = _split3(a)
        return _dot(a0, before) + _dot(a1, before) + _dot(a2, before)

    lo = times_before(r8)
    tot = jnp.sum(r8, axis=0, keepdims=True)
    cap = jnp.floor((tot + (_BLK - 1.0)) * (1.0 / _BLK)) * _BLK
    start = times_before(jnp.broadcast_to(cap, (8, _E)))[0:1, :]
    a_i = lax.broadcasted_iota(jnp.int32, (ntp, ntp), 0)
    b_i = lax.broadcasted_iota(jnp.int32, (ntp, ntp), 1)
    earlier = jnp.where(b_i < a_i, 1.0, 0.0).astype(_bf16)
    goff = start + _dot_exact_lhs(earlier, r8)
    lo_ref[...] = lo.astype(jnp.int32)
    goff_ref[...] = goff.astype(jnp.int32)
    nch_ref[...] = (r8 * (1.0 / _ALIGN)).astype(jnp.int32)
    trow = lax.broadcasted_iota(jnp.int32, (8, _E), 0)
    end = start + cap
    used_blocks = jnp.max(end, axis=-1, keepdims=True) * (1.0 / _BLK)
    tail = jnp.where(trow == 0, start + tot,
                     jnp.where(trow == 1, (cap - tot) * (1.0 / _ALIGN),
                               jnp.where(trow == 2, used_blocks,
                                         jnp.where(trow == 3, start * (1.0 / _BLK),
                                                   jnp.where(trow == 4, cap * (1.0 / _BLK), 0.0)))))
    tail_ref[...] = tail.astype(jnp.int32)
    tsum = jnp.sum(r8, axis=-1, keepdims=True)
    tsum_ref[...] = jnp.broadcast_to(tsum, (ntp, _E)).astype(jnp.int32)


def _offsets(cnt):
    ntp = cnt.shape[0]
    i32 = jnp.int32
    return pl.pallas_call(
        _offsets_kernel,
        out_shape=[jax.ShapeDtypeStruct((ntp, _E), i32),
                   jax.ShapeDtypeStruct((ntp, _E), i32), jax.ShapeDtypeStruct((ntp, _E), i32),
                   jax.ShapeDtypeStruct((8, _E), i32), jax.ShapeDtypeStruct((ntp, _E), i32)],
        name="route_offsets",
    )(cnt)


_HI_MASK = -65536


def _pack_pairs(x):
    half = x.shape[1] // 2
    bits = lax.bitcast_convert_type(x, jnp.int32)
    return (bits[:, :half] & _HI_MASK) | lax.shift_right_logical(bits[:, half:], 16)


def _unpack_pairs(p):
    hi = lax.bitcast_convert_type(p & _HI_MASK, _f32).astype(_bf16)
    lo = lax.bitcast_convert_type(lax.shift_left(p, 16), _f32).astype(_bf16)
    return hi, lo


def _for_each_part(lo_s, goff_s, nch_s, tile, fn):
    def per_expert(e, carry):
        rows = nch_s[tile * _E + e] * _ALIGN
        lo = lo_s[tile * _E + e]
        go = goff_s[tile * _E + e]

        def bulk(j, c):
            fn(lo + j * _BULK, go + j * _BULK, _BULK)
            return c

        lax.fori_loop(0, lax.shift_right_logical(rows, _BULK.bit_length() - 1), bulk, 0)
        size = _BULK // 2
        while size >= _ALIGN:
            @pl.when((rows & size) != 0)
            def _(size=size):
                off = rows & (-2 * size)
                fn(lo + off, go + off, size)
            size //= 2
        return carry
    lax.fori_loop(0, _E, per_expert, 0)


def _part_copy(src, src_row, dst, dst_row, size, sem):
    return pltpu.make_async_copy(src.at[pl.ds(pl.multiple_of(src_row, _ALIGN), size)],
                                 dst.at[pl.ds(pl.multiple_of(dst_row, _ALIGN), size)], sem)


def _wait_rows(total, limit, copy_of_size):
    size = _ALIGN
    while size * 2 <= limit:
        size *= 2
    while size >= _ALIGN:
        @pl.when((total & size) != 0)
        def _(size=size):
            copy_of_size(size).wait()
        size //= 2


def _zero_unused_blocks(zblk, dst_ref, first, sem):
    nb = dst_ref.shape[0] // _BLK

    def copy(b):
        return pltpu.make_async_copy(zblk, dst_ref.at[pl.ds(pl.multiple_of(b * _BLK, _BLK), _BLK)], sem)

    def fill(b, carry):
        copy(b).start()
        return carry

    def drain(b, carry):
        copy(b).wait()
        return carry

    lax.fori_loop(first, nb, fill, 0)
    lax.fori_loop(first, nb, drain, 0)


def _dispatch_kernel(lo_s, goff_s, nch_s, tail_s, tsum_s, hn_ref, route_t_ref, xs_ref,
                     buf, zblk, sem, *, nt):
    i = pl.program_id(0)
    tt = hn_ref.shape[0]
    cap = buf.shape[1]
    rt = route_t_ref[...]
    pos = [rt[2 * _K + k:2 * _K + k + 1, :] for k in range(_K)]

    slot = lax.rem(i, 2)
    tile_buf = buf.at[slot]
    fill_sem = sem.at[2]

    def wait_tile(tile, sl):
        _wait_rows(tsum_s[tile], cap,
                   lambda size: _part_copy(buf.at[sl], 0, xs_ref, 0, size, sem.at[sl]))

    @pl.when(i > 1)
    def _():
        wait_tile(i - 2, slot)

    rows = cap // _SORT_CHUNKS
    hn = hn_ref[...]
    for c in range(_SORT_CHUNKS):
        r_iota = (lax.broadcasted_iota(jnp.int32, (rows, tt), 0) + c * rows).astype(_f32)
        perm = jnp.zeros((rows, tt), _f32)
        for k in range(_K):
            perm = jnp.where(r_iota == pos[k], 1.0, perm)
        buf[slot, c * rows:(c + 1) * rows, :] = _pack_pairs(_dot(perm.astype(_bf16), hn))

    def start_part(local_row, global_row, size):
        _part_copy(tile_buf, local_row, xs_ref, global_row, size, sem.at[slot]).start()

    _for_each_part(lo_s, goff_s, nch_s, i, start_part)

    @pl.when(i == nt - 1)
    def _():
        zblk[...] = jnp.zeros_like(zblk)

        def tail_copy(dst_row):
            return _part_copy(zblk, 0, xs_ref, dst_row, _ALIGN, fill_sem)

        def fill(e, carry):
            def one(j, c):
                tail_copy(tail_s[e] + j * _ALIGN).start()
                return c
            return lax.fori_loop(0, tail_s[_E + e], one, carry)

        def drain(e, carry):
            def one(j, c):
                tail_copy(0).wait()
                return c
            return lax.fori_loop(0, tail_s[_E + e], one, carry)

        lax.fori_loop(0, _E, fill, 0)
        lax.fori_loop(0, _E, drain, 0)
        _zero_unused_blocks(zblk, xs_ref, tail_s[2 * _E], fill_sem)
        if nt > 1:
            wait_tile(i - 1, 1 - slot)
        wait_tile(i, slot)


def _dispatch(lo, goff, nch, tail, tsum, hn, route_t, n_rows):
    s = hn.shape[0]
    tt = min(_TT, s)
    nt = s // tt
    cap = tt * _K + _E * _ALIGN
    grid_spec = pltpu.PrefetchScalarGridSpec(
        num_scalar_prefetch=5,
        grid=(nt,),
        in_specs=[
            pl.BlockSpec((tt, _D), lambda i, *_: (i, 0)),
            pl.BlockSpec((_RT_ROWS, tt), lambda i, *_: (0, i)),
        ],
        out_specs=pl.BlockSpec(memory_space=pl.ANY),
        scratch_shapes=[pltpu.VMEM((2, cap, _D // 2), jnp.int32),
                        pltpu.VMEM((_BLK, _D // 2), jnp.int32), pltpu.SemaphoreType.DMA((3,))],
    )
    return pl.pallas_call(
        functools.partial(_dispatch_kernel, nt=nt),
        grid_spec=grid_spec,
        out_shape=jax.ShapeDtypeStruct((n_rows, _D // 2), jnp.int32),
        compiler_params=pltpu.CompilerParams(dimension_semantics=("arbitrary",),
                                             vmem_limit_bytes=_VMEM_LIMIT),
        name="moe_dispatch",
    )(lo, goff, nch, tail, tsum, hn, route_t)


def _expert_kernel(tail_s, wgu_ref, bgu_ref, wd_ref, bd_ref, xs_ref, ys_ref,
                   wgu_c, wd_c, xin, yout, zblk, sem_in, sem_out):
    e = pl.program_id(0)
    used = tail_s[2 * _E]
    first = tail_s[3 * _E + e]
    nblk = tail_s[4 * _E + e]
    half = _D // 2

    def rows(g):
        return pl.ds(pl.multiple_of(g * _BLK, _BLK), _BLK)

    def in_copy(g, slot):
        return pltpu.make_async_copy(xs_ref.at[rows(g)], xin.at[slot], sem_in.at[slot])

    def out_copy(g, slot):
        return pltpu.make_async_copy(yout.at[slot], ys_ref.at[rows(g)], sem_out.at[slot])

    def slot_of(g):
        return lax.rem(g, _RING)

    @pl.when(e == 0)
    def _():
        for g in range(_AHEAD):
            @pl.when(g < used)
            def _(g=g):
                in_copy(g, g).start()

    @pl.when(nblk > 0)
    def _():
        step = 128
        for r in range(0, _D, step):
            wgu_c[r:r + step, :] = wgu_ref[0, 0, r:r + step, :].astype(_bf16)
        for r in range(0, _F, step):
            wd_c[r:r + step, :] = wd_ref[0, 0, r:r + step, :].astype(_bf16)

    def unit(g, n):
        for d in range(n):
            in_copy(g + d, slot_of(g + d)).wait()
        for d in range(_AHEAD, _AHEAD + n):
            @pl.when(g + d < used)
            def _(d=d):
                in_copy(g + d, slot_of(g + d)).start()
        parts = [_unpack_pairs(xin[slot_of(g + d)]) for d in range(n)]
        x_hi = jnp.concatenate([p[0] for p in parts], axis=0)
        x_lo = jnp.concatenate([p[1] for p in parts], axis=0)
        gu = _dot(x_hi, wgu_c[:half, :]) + _dot(x_lo, wgu_c[half:, :]) + bgu_ref[0, 0]
        x_glu = jnp.minimum(gu[:, :_F], _SWIGLU_LIMIT)
        x_lin = jnp.clip(gu[:, _F:], -_SWIGLU_LIMIT, _SWIGLU_LIMIT)
        act = x_glu * jax.nn.sigmoid(_SWIGLU_ALPHA * x_glu) * (x_lin + 1.0)
        y = _dot(act.astype(_bf16), wd_c[...]) + bd_ref[0, 0]
        packed = _pack_pairs(y.astype(_bf16).astype(_f32))
        for d in range(n):
            @pl.when(g + d >= _RING)
            def _(d=d):
                out_copy(g + d - _RING, slot_of(g + d)).wait()
            yout[slot_of(g + d)] = packed[d * _BLK:(d + 1) * _BLK, :]
            out_copy(g + d, slot_of(g + d)).start()

    def widest(j, carry):
        unit(first + _AHEAD * j, _AHEAD)
        return carry

    lax.fori_loop(0, nblk // _AHEAD, widest, 0)
    n = _AHEAD // 2
    while n >= 1:
        @pl.when((nblk & n) != 0)
        def _(n=n):
            unit(first + (nblk & (-2 * n)), n)
        n //= 2

    @pl.when(e == _E - 1)
    def _():
        for d in range(1, _RING + 1):
            @pl.when(used >= d)
            def _(d=d):
                out_copy(used - d, slot_of(used - d)).wait()

        zblk[...] = jnp.zeros_like(zblk)
        _zero_unused_blocks(zblk, ys_ref, used, sem_out.at[0])


def _experts(tail, xs, w_gu, b_gu, w_down, b_down, layer):
    n_rows = xs.shape[0]
    wsel = lambda e, *_: (layer, e, 0, 0)
    grid_spec = pltpu.PrefetchScalarGridSpec(
        num_scalar_prefetch=1,
        grid=(_E,),
        in_specs=[
            pl.BlockSpec((1, 1, _D, 2 * _F), wsel),
            pl.BlockSpec((1, 1, 1, 2 * _F), wsel),
            pl.BlockSpec((1, 1, _F, _D), wsel),
            pl.BlockSpec((1, 1, 1, _D), wsel),
            pl.BlockSpec(memory_space=pl.ANY),
        ],
        out_specs=pl.BlockSpec(memory_space=pl.ANY),
        scratch_shapes=[pltpu.VMEM((_D, 2 * _F), _bf16), pltpu.VMEM((_F, _D), _bf16),
                        pltpu.VMEM((_RING, _BLK, _D // 2), jnp.int32),
                        pltpu.VMEM((_RING, _BLK, _D // 2), jnp.int32),
                        pltpu.VMEM((_BLK, _D // 2), jnp.int32),
                        pltpu.SemaphoreType.DMA((_RING,)), pltpu.SemaphoreType.DMA((_RING,))],
    )
    depth = w_gu.shape[0]
    return pl.pallas_call(
        _expert_kernel,
        grid_spec=grid_spec,
        out_shape=jax.ShapeDtypeStruct((n_rows, _D // 2), jnp.int32),
        compiler_params=pltpu.CompilerParams(dimension_semantics=("arbitrary",),
                                             vmem_limit_bytes=_VMEM_LIMIT),
        name="moe_experts",
    )(tail, w_gu, b_gu.reshape(depth, _E, 1, 2 * _F), w_down, b_down.reshape(depth, _E, 1, _D), xs)


def _combine_kernel(lo_s, goff_s, nch_s, tsum_s, route_ref, x1_ref, mod_ref, fin_ref, ys_ref,
                    out_ref, ybuf, sem, *, final, nt):
    i = pl.program_id(0)
    tt = x1_ref.shape[0]
    cap = ybuf.shape[1]
    slot = lax.rem(i, 2)

    def fetch(tile, dst):
        def start_part(local_row, global_row, size):
            _part_copy(ys_ref, global_row, dst, local_row, size, sem).start()
        _for_each_part(lo_s, goff_s, nch_s, tile, start_part)

    @pl.when(i == 0)
    def _():
        ybuf[...] = jnp.zeros_like(ybuf)
        fetch(0, ybuf.at[0])

    _wait_rows(tsum_s[i], cap, lambda size: _part_copy(ys_ref, 0, ybuf.at[slot], 0, size, sem))

    @pl.when(i + 1 < nt)
    def _():
        fetch(i + 1, ybuf.at[1 - slot])

    route = route_ref[...]
    pos = [route[:, 2 * _K + k:2 * _K + k + 1] for k in range(_K)]
    gate = [route[:, _K + k:_K + k + 1] for k in range(_K)]

    cols = cap // _SORT_CHUNKS
    moe_hi = jnp.zeros((tt, _D // 2), _f32)
    moe_lo = jnp.zeros((tt, _D // 2), _f32)
    for c in range(_SORT_CHUNKS):
        lane_r = (lax.broadcasted_iota(jnp.int32, (tt, cols), 1) + c * cols).astype(_f32)
        wmat = jnp.zeros((tt, cols), _f32)
        for k in range(_K):
            wmat = jnp.where(lane_r == pos[k], gate[k], wmat)
        wmat = wmat.astype(_bf16)
        y_hi, y_lo = _unpack_pairs(ybuf[slot, c * cols:(c + 1) * cols, :])
        moe_hi = moe_hi + _dot(wmat, y_hi)
        moe_lo = moe_lo + _dot(wmat, y_lo)
    moe = jnp.concatenate([moe_hi, moe_lo], axis=1)
    x2 = x1_ref[...] + mod_ref[5:6, :] * moe
    if final:
        x2 = _rms(x2) * fin_ref[...]
    out_ref[...] = x2


def _combine(lo, goff, nch, tsum, route, x1, mod, final_norm, ys, final):
    s = x1.shape[0]
    tt = min(_TT, s)
    nt = s // tt
    cap = tt * _K + _E * _ALIGN
    grid_spec = pltpu.PrefetchScalarGridSpec(
        num_scalar_prefetch=4,
        grid=(nt,),
        in_specs=[
            pl.BlockSpec((tt, 128), lambda i, *_: (i, 0)),
            pl.BlockSpec((tt, _D), lambda i, *_: (i, 0)),
            pl.BlockSpec((6, _D), lambda i, *_: (0, 0)),
            pl.BlockSpec((1, _D), lambda i, *_: (0, 0)),
            pl.BlockSpec(memory_space=pl.ANY),
        ],
        out_specs=pl.BlockSpec((tt, _D), lambda i, *_: (i, 0)),
        scratch_shapes=[pltpu.VMEM((2, cap, _D // 2), jnp.int32), pltpu.SemaphoreType.DMA(())],
    )
    return pl.pallas_call(
        functools.partial(_combine_kernel, final=final, nt=nt),
        grid_spec=grid_spec,
        out_shape=jax.ShapeDtypeStruct((s, _D), _f32),
        compiler_params=pltpu.CompilerParams(dimension_semantics=("arbitrary",),
                                             vmem_limit_bytes=_VMEM_LIMIT),
        name="moe_combine",
    )(lo, goff, nch, tsum, route, x1, mod, final_norm.reshape(1, _D), ys)


def _moe(x1, hn2, logits, mod, final_norm, w_gu, b_gu, w_down, b_down, layer, final):
    s = x1.shape[0]
    tt = min(_TT, s)
    nt = s // tt
    ntp = -(-nt // 8) * 8
    n_rows = -(-(s * _K + (_ALIGN - 1) * nt * _E) // _BLK) * _BLK + _E * _BLK
    route, route_t, cnt = _route(logits)
    cnt = jnp.pad(cnt.reshape(nt, _E), ((0, ntp - nt), (0, 0)))
    lo, goff, nch, tail, tsum = _offsets(cnt)
    lo_s, goff_s, nch_s = (a[:nt].reshape(nt * _E) for a in (lo, goff, nch))
    tail_s = tail[:5].reshape(5 * _E)
    tsum_s = tsum[:nt, 0]
    xs = _dispatch(lo_s, goff_s, nch_s, tail_s, tsum_s, hn2, route_t, n_rows)
    ys = _experts(tail_s, xs, w_gu, b_gu, w_down, b_down, layer)
    return _combine(lo_s, goff_s, nch_s, tsum_s, route, x1, mod, final_norm, ys, final)


def _decay_tables(tb):
    t = np.arange(tb)
    tri = ((t[:, None] // _CH) == (t[None, :] // _CH)) & (t[None, :] <= t[:, None])
    rel = (t[:, None] - t[None, :]).astype(np.float64)
    lg = np.log(1.0 - 2.0 ** (-5.0 - np.arange(_HEADS, dtype=np.float64)))
    dmat = np.where(rel >= 0, np.exp(np.maximum(rel, 0.0)[None] * lg[:, None, None]), 0.0)
    return jnp.asarray(tri.astype(np.float32), _bf16), jnp.asarray(dmat.astype(np.float32))


def _rotary_tables(s, tb):
    half = _DH // 2
    inv = (np.float32(10000.0) ** (-np.arange(half, dtype=np.float32) / np.float32(half)))
    inv = inv.astype(np.float64)

    def cos_sin(positions):
        ang = positions.astype(np.float64)[:, None] * inv[None, :]
        cos = np.concatenate([np.cos(ang)] * 2, axis=1)
        sin = np.concatenate([np.sin(ang)] * 2, axis=1)
        return cos.astype(np.float32), sin.astype(np.float32)

    cos_a, sin_a = cos_sin(np.arange(s // tb) * tb)
    cos_b, sin_b = cos_sin(np.arange(tb))
    return (jnp.asarray(np.stack([cos_a, sin_a], axis=1)), jnp.asarray(np.stack([cos_b, sin_b], axis=0)))


def _lower_bounds_kernel(p_ref, o_ref):
    p = p_ref[...]
    n = p.shape[0]
    m = p[0:1, :]
    for j in range(1, n):
        m = jnp.maximum(m, p[j:j + 1, :])
    ex = [jnp.exp(p[j:j + 1, :] - m) for j in range(n)]
    den = ex[0]
    for j in range(1, n):
        den = den + ex[j]
    acc = jnp.zeros_like(den)
    rows = []
    for j in range(n):
        acc = acc + ex[j] / den
        rows.append(acc)
    o_ref[...] = jnp.concatenate(rows, axis=0)


def _lower_bounds(p):
    return pl.pallas_call(
        _lower_bounds_kernel,
        out_shape=jax.ShapeDtypeStruct(p.shape, _f32),
        name="hgrn2_lower_bounds",
    )(p)


def kernel(x, c, w_ada, b_ada, w_in, hg_lower_bounds, hg_norm, ret_norm, w_out, pool_w, pool_b,
           pool_scale, w_router, b_router, w_gu, b_gu, w_down, b_down, final_norm):
    b, s, d = x.shape
    assert b == 1 and d == _D
    depth = w_ada.shape[0]
    assert depth == 2
    xs = x.reshape(s, d)
    mod = _ada(c, w_ada, b_ada)
    lb_all = _lower_bounds(hg_lower_bounds)
    x1, hn2, logits = _mixer0(xs, mod[0], w_in[0], lb_all[0], hg_norm[0], ret_norm[0], w_out[0],
                              w_router[0], b_router[0])
    xs = _moe(x1, hn2, logits, mod[0], final_norm, w_gu, b_gu, w_down, b_down, 0, False)
    x1, hn2, logits = _pool_mixer(xs, mod[1], pool_w[0], pool_b[0], pool_scale[0], w_router[1],
                                  b_router[1])
    xs = _moe(x1, hn2, logits, mod[1], final_norm, w_gu, b_gu, w_down, b_down, 1, True)
    return xs.reshape(b, s, d)
```

```python
import functools
import math

import jax
import jax.numpy as jnp
import numpy as np
from jax import lax
from jax.experimental import pallas as pl
from jax.experimental.pallas import tpu as pltpu

_LANES = 128
_SUBLANES = 8
_D = 1024
_E = 32
_K = 4
_F = 1024
_HEADS = 4
_DH = 128
_CH = 64
_SUB = 16
_EPS = 1e-6
_BLK = 256
_TB = 256
_TBP = 512
_TT = 512
_ALIGN = _SUBLANES
_RT_ROWS = 16
_BULK = 64
_SORT_CHUNKS = 3
_AHEAD = 4
_RING = 2 * _AHEAD
_NEG = -1e30
_LOG2E = 1.4426950408889634
_SWIGLU_LIMIT = 7.0
_SWIGLU_ALPHA = 1.702
_POOL_WINDOWS = (2, 4, 8, 16)
_HALO = 16
_VMEM_BYTES = 64 * 1024 * 1024
_VMEM_LIMIT = _VMEM_BYTES // 8 * 7

_f32 = jnp.float32
_bf16 = jnp.bfloat16


def _dot(a, b):
    return jnp.dot(a, b, preferred_element_type=_f32)


def _dot_nt(a, b):
    return lax.dot_general(a, b, (((1,), (1,)), ((), ())), preferred_element_type=_f32)


def _split3(a):
    a0 = a.astype(_bf16)
    r1 = a - a0.astype(_f32)
    a1 = r1.astype(_bf16)
    a2 = (r1 - a1.astype(_f32)).astype(_bf16)
    return a0, a1, a2


def _dot_exact_lhs(m, a):
    a0, a1, a2 = _split3(a)
    return _dot(m, a0) + _dot(m, a1) + _dot(m, a2)


def _rms(x):
    return x * lax.rsqrt(jnp.mean(x * x, axis=-1, keepdims=True) + _EPS)


def _silu(x):
    return x * jax.nn.sigmoid(x)


def _ada_kernel(c_ref, w_ref, b_ref, o_ref):
    c = c_ref[...]
    cond = _silu(c)
    o_ref[0] = jnp.sum(w_ref[0] * cond, axis=0, keepdims=True) + b_ref[0]


def _ada(c, w_ada, b_ada):
    depth = w_ada.shape[0]
    n = depth * 6
    out = pl.pallas_call(
        _ada_kernel,
        grid=(n,),
        in_specs=[
            pl.BlockSpec((_D, 1), lambda j: (0, 0)),
            pl.BlockSpec((1, _D, _D), lambda j: (j // 6, 0, j % 6)),
            pl.BlockSpec((1, 1, _D), lambda j: (j, 0, 0)),
        ],
        out_specs=pl.BlockSpec((1, 1, _D), lambda j: (j, 0, 0)),
        out_shape=jax.ShapeDtypeStruct((n, 1, _D), _f32),
        name="ada_mod",
    )(c.reshape(_D, 1), w_ada, b_ada.reshape(n, 1, _D))
    return out.reshape(depth, 6, _D)


def _ffn_prologue(x1, mod, wr_ref, br_ref, x1_ref, hn2_ref, lg_ref):
    sh_f, sc_f = mod[3:4, :], mod[4:5, :]
    hn2 = _rms(x1) * (1.0 + sc_f) + sh_f
    x1_ref[...] = x1
    a0, a1, _ = _split3(hn2)
    hn2_ref[...] = a0
    w0, w1, _ = _split3(wr_ref[...])
    both = _dot(a0, jnp.concatenate([w0, w1], axis=1))
    lg_ref[...] = (both[:, :_E] + both[:, _E:] + _dot(a1, w0) + br_ref[...]).T


def _mixer0_kernel(x_ref, mod_ref, win_ref, lb_ref, hgg_ref, rtg_ref, wout_ref, rot_a_ref, rot_b_ref,
                   tri_ref, dmat_ref, wr_ref, br_ref, x1_ref, hn2_ref, lg_ref,
                   q_s, k_s, v_s, b_s, o_s, st_hg, st_rt):
    tb = x_ref.shape[0]
    nch = tb // _CH
    nsub = _CH // _SUB

    @pl.when(pl.program_id(0) == 0)
    def _():
        st_hg[...] = jnp.zeros_like(st_hg)
        st_rt[...] = jnp.zeros_like(st_rt)

    x = x_ref[...]
    mod = mod_ref[...]
    sh_m, sc_m, g_m = mod[0:1, :], mod[1:2, :], mod[2:3, :]
    hb = (_rms(x) * (1.0 + sc_m) + sh_m).astype(_bf16)
    width = _HEADS * _DH

    def sec(j):
        return _dot(hb, win_ref[:, j * width:(j + 1) * width])

    hq = sec(0)
    lb = lb_ref[...]
    fg = lb + (1.0 - lb) * jax.nn.sigmoid(sec(1))
    hi = sec(2)
    g = jnp.log(fg) * _LOG2E
    bcum = _dot_exact_lhs(tri_ref[...], g)
    qh = _silu(hq)
    kk = 1.0 - fg
    for h in range(_HEADS):
        sl = slice(h * _DH, (h + 1) * _DH)
        q_s[h] = qh[:, sl]
        k_s[h] = kk[:, sl]
        v_s[h] = hi[:, sl]
        b_s[h] = bcum[:, sl]

    hgg = hgg_ref[...]
    row64 = lax.broadcasted_iota(jnp.int32, (_HEADS, _CH, _DH), 1)
    row16 = lax.broadcasted_iota(jnp.int32, (_HEADS * nsub, _SUB, 1), 1)
    lead16 = lax.broadcasted_iota(jnp.int32, (_HEADS * nsub, _SUB, _CH), 0)
    col16 = lax.broadcasted_iota(jnp.int32, (_HEADS * nsub, _SUB, _CH), 2) - _SUB * (lead16 & (nsub - 1))

    def chunk(c, carry):
        rows = pl.ds(pl.multiple_of(c * _CH, _CH), _CH)
        q = q_s[:, rows, :]
        k = k_s[:, rows, :]
        v = v_s[:, rows, :]
        b = b_s[:, rows, :]
        st = st_hg[...]
        vb = v.astype(_bf16)
        qe = (q * jnp.exp2(b)).astype(_bf16)
        o = jnp.einsum('htd,hed->hte', qe, st.astype(_bf16), preferred_element_type=_f32)
        blocks = [jnp.zeros((_HEADS, _SUB, _CH), _f32)]
        for i in range(1, nsub):
            ref = b[:, _SUB * i - 1:_SUB * i, :]
            qp = q[:, _SUB * i:_SUB * (i + 1), :] * jnp.exp2(b[:, _SUB * i:_SUB * (i + 1), :] - ref)
            kp = k * jnp.exp2(jnp.where(row64 < _SUB * i, ref - b, _NEG))
            blocks.append(jnp.einsum('htd,hsd->hts', qp.astype(_bf16), kp.astype(_bf16),
                                     preferred_element_type=_f32))
        a = jnp.concatenate(blocks, axis=1)
        q3 = q.reshape(_HEADS * nsub, _SUB, _DH)
        k3 = k.reshape(_HEADS * nsub, _SUB, _DH)
        b3 = b.reshape(_HEADS * nsub, _SUB, _DH)
        ad = jnp.zeros((_HEADS * nsub, _SUB, _CH), _f32)
        for s in range(_SUB):
            e = jnp.exp2(b3 - b3[:, s:s + 1, :])
            w = jnp.sum(q3 * e * k3[:, s:s + 1, :], axis=-1, keepdims=True)
            ad = jnp.where((col16 == s) & (row16 >= s), w, ad)
        a = a + ad.reshape(_HEADS, _CH, _CH)
        o = o + jnp.einsum('hts,hse->hte', a.astype(_bf16), vb, preferred_element_type=_f32)
        bl = b[:, _CH - 1:_CH, :]
        kd = (k * jnp.exp2(bl - b)).astype(_bf16)
        dec = jnp.exp2(bl)
        for h in range(_HEADS):
            st_hg[h] = st[h] * dec[h] + _dot(v[h].T.astype(_bf16), kd[h])
        o = o * lax.rsqrt(jnp.mean(o * o, axis=-1, keepdims=True) + _EPS) * hgg
        o_s[:, rows, :] = o
        return carry

    lax.fori_loop(0, nch, chunk, 0, unroll=True)
    hog = sec(3)
    cat_hg = jnp.concatenate([o_s[h] for h in range(_HEADS)], axis=1) * _silu(hog)

    rq = sec(4)
    rk = sec(5)
    rv = sec(6)
    cos_a, sin_a = rot_a_ref[0, 0:1, :], rot_a_ref[0, 1:2, :]
    cos_b, sin_b = rot_b_ref[0], rot_b_ref[1]
    first_half = lax.broadcasted_iota(jnp.int32, (1, _DH), 1) < _DH // 2
    cosf = cos_a * cos_b - sin_a * sin_b
    sinf = (sin_a * cos_b + cos_a * sin_b) * jnp.where(first_half, -1.0, 1.0)
    tcol = lax.broadcasted_iota(jnp.int32, (tb, 1), 0).astype(_f32)
    rtg = rtg_ref[...]
    outs = []
    for h in range(_HEADS):
        sl = slice(h * _DH, (h + 1) * _DH)
        lg = math.log(1.0 - 2.0 ** (-5.0 - h))
        xq = rq[:, sl]
        xk = rk[:, sl]
        q = xq * cosf + pltpu.roll(xq, _DH // 2, 1) * sinf
        k = (xk * cosf + pltpu.roll(xk, _DH // 2, 1) * sinf) * (_DH ** -0.5)
        v = rv[:, sl]
        scores = _dot_nt(q.astype(_bf16), k.astype(_bf16)) * dmat_ref[h]
        o = _dot(scores.astype(_bf16), v.astype(_bf16))
        st = st_rt[h]
        qd = q * jnp.exp((tcol + 1.0) * lg)
        o = o + _dot_nt(qd.astype(_bf16), st.astype(_bf16))
        kd = k * jnp.exp((tb - 1.0 - tcol) * lg)
        st_rt[h] = st * math.exp(tb * lg) + _dot(v.T.astype(_bf16), kd.astype(_bf16))
        o = o * lax.rsqrt(jnp.mean(o * o, axis=-1, keepdims=True) + _EPS) * rtg[h]
        outs.append(o)
    rg = sec(7)
    cat_rt = jnp.concatenate(outs, axis=1) * _silu(rg)

    cat = jnp.concatenate([cat_hg, cat_rt], axis=1).astype(_bf16)
    mix = _dot(cat, wout_ref[...])
    x1 = x + g_m * mix
    _ffn_prologue(x1, mod, wr_ref, br_ref, x1_ref, hn2_ref, lg_ref)


def _mixer0(x, mod, w_in, lb, hg_gain, ret_gain, w_out, w_router, b_router):
    s = x.shape[0]
    tb = min(_TB, s)
    rot_a, rot_b = _rotary_tables(s, tb)
    tri, dmat = _decay_tables(tb)
    width = _HEADS * _DH
    full = lambda shape: pl.BlockSpec(shape, lambda i: (0,) * len(shape))
    rowblk = lambda w: pl.BlockSpec((tb, w), lambda i: (i, 0))
    return pl.pallas_call(
        _mixer0_kernel,
        grid=(s // tb,),
        in_specs=[
            rowblk(_D), full((6, _D)), full((_D, 8 * width)), full((1, width)),
            full((_HEADS, 1, _DH)), full((_HEADS, 1, _DH)), full((2 * width, _D)),
            pl.BlockSpec((1, 2, _DH), lambda i: (i, 0, 0)), full((2, tb, _DH)),
            full((tb, tb)), full((_HEADS, tb, tb)), full((_D, _E)), full((1, _E)),
        ],
        out_specs=[rowblk(_D), rowblk(_D), pl.BlockSpec((_E, tb), lambda i: (0, i))],
        out_shape=[jax.ShapeDtypeStruct((s, _D), _f32), jax.ShapeDtypeStruct((s, _D), _bf16),
                   jax.ShapeDtypeStruct((_E, s), _f32)],
        scratch_shapes=[pltpu.VMEM((_HEADS, tb, _DH), _f32)] * 5
        + [pltpu.VMEM((_HEADS, _DH, _DH), _f32)] * 2,
        compiler_params=pltpu.CompilerParams(dimension_semantics=("arbitrary",),
                                             vmem_limit_bytes=_VMEM_LIMIT),
        name="mixer_hgrn2_retention",
    )(x, mod, w_in.astype(_bf16), lb.reshape(1, width), hg_gain.reshape(_HEADS, 1, _DH),
      ret_gain.reshape(_HEADS, 1, _DH), w_out.astype(_bf16), rot_a, rot_b,
      tri, dmat, w_router,
      b_router.reshape(1, _E))


def _pool_kernel(x_ref, mod_ref, pw_ref, pb_ref, ps_ref, wr_ref, br_ref,
                 x1_ref, hn2_ref, lg_ref, halo):
    tb = x_ref.shape[0]
    i = pl.program_id(0)

    @pl.when(i == 0)
    def _():
        halo[...] = jnp.zeros_like(halo)

    x = x_ref[...]
    mod = mod_ref[...]
    sh_m, sc_m, g_m = mod[0:1, :], mod[1:2, :], mod[2:3, :]
    hn = _rms(x) * (1.0 + sc_m) + sh_m
    ext = jnp.concatenate([halo[...], hn], axis=0)
    halo[...] = hn[tb - _HALO:, :]
    t = (i * tb + lax.broadcasted_iota(jnp.int32, (tb, 1), 0)).astype(_f32)
    gw = _D // len(_POOL_WINDOWS)
    ys = []
    for gi, w in enumerate(_POOL_WINDOWS):
        acc = ext[:, gi * gw:(gi + 1) * gw]
        shift = 1
        while shift < w:
            acc = acc + pltpu.roll(acc, shift, 0)
            shift *= 2
        win = acc[_HALO:, :]
        inv_cnt = 1.0 / jnp.minimum(t + 1.0, float(w))
        p = win * inv_cnt - hn[:, gi * gw:(gi + 1) * gw]
        ys.append(_dot(p.astype(_bf16), pw_ref[gi]) + pb_ref[gi])
    mix = jnp.concatenate(ys, axis=1) * ps_ref[...]
    x1 = x + g_m * mix
    _ffn_prologue(x1, mod, wr_ref, br_ref, x1_ref, hn2_ref, lg_ref)


def _pool_mixer(x, mod, pool_w, pool_b, pool_scale, w_router, b_router):
    s = x.shape[0]
    tb = min(_TBP, s)
    ng = len(_POOL_WINDOWS)
    gw = _D // ng
    full = lambda shape: pl.BlockSpec(shape, lambda i: (0,) * len(shape))
    rowblk = lambda w: pl.BlockSpec((tb, w), lambda i: (i, 0))
    return pl.pallas_call(
        _pool_kernel,
        grid=(s // tb,),
        in_specs=[rowblk(_D), full((6, _D)), full((ng, gw, gw)), full((ng, 1, gw)), full((1, _D)),
                  full((_D, _E)), full((1, _E))],
        out_specs=[rowblk(_D), rowblk(_D), pl.BlockSpec((_E, tb), lambda i: (0, i))],
        out_shape=[jax.ShapeDtypeStruct((s, _D), _f32), jax.ShapeDtypeStruct((s, _D), _bf16),
                   jax.ShapeDtypeStruct((_E, s), _f32)],
        scratch_shapes=[pltpu.VMEM((_HALO, _D), _f32)],
        compiler_params=pltpu.CompilerParams(dimension_semantics=("arbitrary",),
                                             vmem_limit_bytes=_VMEM_LIMIT),
        name="mixer_pool",
    )(x, mod, pool_w.astype(_bf16), pool_b.reshape(ng, 1, gw), pool_scale.reshape(1, _D),
      w_router, b_router.reshape(1, _E))


def _route_kernel(lg_ref, route_ref, route_t_ref, cnt_ref):
    tt = lg_ref.shape[1]
    vals = lg_ref[...]
    sub = lax.broadcasted_iota(jnp.int32, (_E, tt), 0).astype(_f32)
    tops, idxs, hots = [], [], []
    for _ in range(_K):
        m = jnp.max(vals, axis=0, keepdims=True)
        idx = jnp.min(jnp.where(vals == m, sub, float(_E)), axis=0, keepdims=True)
        hot = sub == idx
        vals = jnp.where(hot, -jnp.inf, vals)
        tops.append(m)
        idxs.append(idx)
        hots.append(hot)
    ex = [jnp.exp(m - tops[0]) for m in tops]
    den = ex[0] + ex[1] + ex[2] + ex[3]
    member = jnp.zeros((_E, tt), _f32)
    for hot in hots:
        member = member + jnp.where(hot, 1.0, 0.0)
    member_b = member.astype(_bf16)
    s_i = lax.broadcasted_iota(jnp.int32, (tt, tt), 0)
    t_i = lax.broadcasted_iota(jnp.int32, (tt, tt), 1)
    earlier = jnp.where(s_i < t_i, 1.0, 0.0).astype(_bf16)
    cum = _dot(member_b, earlier)
    cnt_ref[0] = _dot_nt(jnp.ones((_SUBLANES, tt), _bf16), member_b)[0:1, :]
    cnt = jnp.sum(member, axis=1, keepdims=True)
    r8 = jnp.floor((cnt + (_ALIGN - 1.0)) * (1.0 / _ALIGN)) * _ALIGN
    e_i = lax.broadcasted_iota(jnp.int32, (_E, _E), 0)
    j_i = lax.broadcasted_iota(jnp.int32, (_E, _E), 1)
    before = jnp.where(j_i < e_i, 1.0, 0.0).astype(_bf16)
    lo = _dot(before, jnp.broadcast_to(r8, (_E, _LANES)).astype(_bf16))[:, 0:1]
    row = cum + lo
    out_row = lax.broadcasted_iota(jnp.int32, (_RT_ROWS, tt), 0)
    out = jnp.zeros((_RT_ROWS, tt), _f32)
    for k in range(_K):
        pos = jnp.sum(jnp.where(hots[k], row, 0.0), axis=0, keepdims=True)
        out = jnp.where(out_row == k, idxs[k], out)
        out = jnp.where(out_row == _K + k, ex[k] / den, out)
        out = jnp.where(out_row == 2 * _K + k, pos, out)
    route_t_ref[...] = out
    route_ref[...] = jnp.zeros_like(route_ref)
    route_ref[:, 0:_RT_ROWS] = out.T


def _route(logits_t):
    s = logits_t.shape[1]
    tt = min(_TT, s)
    nt = s // tt
    return pl.pallas_call(
        _route_kernel,
        grid=(nt,),
        in_specs=[pl.BlockSpec((_E, tt), lambda i: (0, i))],
        out_specs=[pl.BlockSpec((tt, _LANES), lambda i: (i, 0)),
                   pl.BlockSpec((_RT_ROWS, tt), lambda i: (0, i)),
                   pl.BlockSpec((1, 1, _E), lambda i: (i, 0, 0))],
        out_shape=[jax.ShapeDtypeStruct((s, _LANES), _f32),
                   jax.ShapeDtypeStruct((_RT_ROWS, s), _f32),
                   jax.ShapeDtypeStruct((nt, 1, _E), _f32)],
        compiler_params=pltpu.CompilerParams(dimension_semantics=("arbitrary",)),
        name="route_topk_rank",
    )(logits_t)


def _offsets_kernel(cnt_ref, lo_ref, goff_ref, nch_ref, tail_ref, tsum_ref):
    cnt = cnt_ref[...]
    ntp = cnt.shape[0]
    r8 = jnp.floor((cnt + (_ALIGN - 1.0)) * (1.0 / _ALIGN)) * _ALIGN
    j_i = lax.broadcasted_iota(jnp.int32, (_E, _E), 0)
    e_i = lax.broadcasted_iota(jnp.int32, (_E, _E), 1)
    before = jnp.where(j_i < e_i, 1.0, 0.0).astype(_bf16)

    def times_before(a):
        a0, a1, a2 = _split3(a)
        return _dot(a0, before) + _dot(a1, before) + _dot(a2, before)

    lo = times_before(r8)
    tot = jnp.sum(r8, axis=0, keepdims=True)
    cap = jnp.floor((tot + (_BLK - 1.0)) * (1.0 / _BLK)) * _BLK
    start = times_before(jnp.broadcast_to(cap, (_SUBLANES, _E)))[0:1, :]
    a_i = lax.broadcasted_iota(jnp.int32, (ntp, ntp), 0)
    b_i = lax.broadcasted_iota(jnp.int32, (ntp, ntp), 1)
    earlier = jnp.where(b_i < a_i, 1.0, 0.0).astype(_bf16)
    goff = start + _dot_exact_lhs(earlier, r8)
    lo_ref[...] = lo.astype(jnp.int32)
    goff_ref[...] = goff.astype(jnp.int32)
    nch_ref[...] = (r8 * (1.0 / _ALIGN)).astype(jnp.int32)
    trow = lax.broadcasted_iota(jnp.int32, (_SUBLANES, _E), 0)
    end = start + cap
    used_blocks = jnp.max(end, axis=-1, keepdims=True) * (1.0 / _BLK)
    tail = jnp.where(trow == 0, start + tot,
                     jnp.where(trow == 1, (cap - tot) * (1.0 / _ALIGN),
                               jnp.where(trow == 2, used_blocks,
                                         jnp.where(trow == 3, start * (1.0 / _BLK),
                                                   jnp.where(trow == 4, cap * (1.0 / _BLK), 0.0)))))
    tail_ref[...] = tail.astype(jnp.int32)
    tsum = jnp.sum(r8, axis=-1, keepdims=True)
    tsum_ref[...] = jnp.broadcast_to(tsum, (ntp, _E)).astype(jnp.int32)


def _offsets(cnt):
    ntp = cnt.shape[0]
    i32 = jnp.int32
    return pl.pallas_call(
        _offsets_kernel,
        out_shape=[jax.ShapeDtypeStruct((ntp, _E), i32),
                   jax.ShapeDtypeStruct((ntp, _E), i32), jax.ShapeDtypeStruct((ntp, _E), i32),
                   jax.ShapeDtypeStruct((_SUBLANES, _E), i32), jax.ShapeDtypeStruct((ntp, _E), i32)],
        name="route_offsets",
    )(cnt)


_HI_MASK = -65536


def _pack_pairs(x):
    half = x.shape[1] // 2
    bits = lax.bitcast_convert_type(x, jnp.int32)
    return (bits[:, :half] & _HI_MASK) | lax.shift_right_logical(bits[:, half:], 16)


def _unpack_pairs(p):
    hi = lax.bitcast_convert_type(p & _HI_MASK, _f32).astype(_bf16)
    lo = lax.bitcast_convert_type(lax.shift_left(p, 16), _f32).astype(_bf16)
    return hi, lo


def _for_each_part(lo_s, goff_s, nch_s, tile, fn):
    def per_expert(e, carry):
        rows = nch_s[tile * _E + e] * _ALIGN
        lo = lo_s[tile * _E + e]
        go = goff_s[tile * _E + e]

        def bulk(j, c):
            fn(lo + j * _BULK, go + j * _BULK, _BULK)
            return c

        lax.fori_loop(0, lax.shift_right_logical(rows, _BULK.bit_length() - 1), bulk, 0)
        size = _BULK // 2
        while size >= _ALIGN:
            @pl.when((rows & size) != 0)
            def _(size=size):
                off = rows & (-2 * size)
                fn(lo + off, go + off, size)
            size //= 2
        return carry
    lax.fori_loop(0, _E, per_expert, 0)


def _part_copy(src, src_row, dst, dst_row, size, sem):
    return pltpu.make_async_copy(src.at[pl.ds(pl.multiple_of(src_row, _ALIGN), size)],
                                 dst.at[pl.ds(pl.multiple_of(dst_row, _ALIGN), size)], sem)


def _wait_rows(total, limit, copy_of_size):
    size = _ALIGN
    while size * 2 <= limit:
        size *= 2
    while size >= _ALIGN:
        @pl.when((total & size) != 0)
        def _(size=size):
            copy_of_size(size).wait()
        size //= 2


def _zero_unused_blocks(zblk, dst_ref, first, sem):
    nb = dst_ref.shape[0] // _BLK

    def copy(b):
        return pltpu.make_async_copy(zblk, dst_ref.at[pl.ds(pl.multiple_of(b * _BLK, _BLK), _BLK)], sem)

    def fill(b, carry):
        copy(b).start()
        return carry

    def drain(b, carry):
        copy(b).wait()
        return carry

    lax.fori_loop(first, nb, fill, 0)
    lax.fori_loop(first, nb, drain, 0)


def _dispatch_kernel(lo_s, goff_s, nch_s, tail_s, tsum_s, hn_ref, route_t_ref, xs_ref,
                     buf, zblk, sem, *, nt):
    i = pl.program_id(0)
    tt = hn_ref.shape[0]
    cap = buf.shape[1]
    rt = route_t_ref[...]
    pos16 = [rt[2 * _K + k:2 * _K + k + 1, :].astype(jnp.int32).astype(jnp.int16)
             for k in range(_K)]

    slot = lax.rem(i, 2)
    tile_buf = buf.at[slot]
    fill_sem = sem.at[2]

    def wait_tile(tile, sl):
        _wait_rows(tsum_s[tile], cap,
                   lambda size: _part_copy(buf.at[sl], 0, xs_ref, 0, size, sem.at[sl]))

    @pl.when(i > 1)
    def _():
        wait_tile(i - 2, slot)

    rows = cap // _SORT_CHUNKS
    hn = hn_ref[...]
    for c in range(_SORT_CHUNKS):
        r_iota = (lax.broadcasted_iota(jnp.int32, (rows, tt), 0) + c * rows).astype(jnp.int16)
        perm = jnp.zeros((rows, tt), _bf16)
        for k in range(_K):
            perm = jnp.where(r_iota == pos16[k], jnp.ones((), _bf16), perm)
        buf[slot, c * rows:(c + 1) * rows, :] = _pack_pairs(_dot(perm, hn))

    def start_part(local_row, global_row, size):
        _part_copy(tile_buf, local_row, xs_ref, global_row, size, sem.at[slot]).start()

    _for_each_part(lo_s, goff_s, nch_s, i, start_part)

    @pl.when(i == nt - 1)
    def _():
        zblk[...] = jnp.zeros_like(zblk)

        def tail_copy(dst_row):
            return _part_copy(zblk, 0, xs_ref, dst_row, _ALIGN, fill_sem)

        def fill(e, carry):
            def one(j, c):
                tail_copy(tail_s[e] + j * _ALIGN).start()
                return c
            return lax.fori_loop(0, tail_s[_E + e], one, carry)

        def drain(e, carry):
            def one(j, c):
                tail_copy(0).wait()
                return c
            return lax.fori_loop(0, tail_s[_E + e], one, carry)

        lax.fori_loop(0, _E, fill, 0)
        lax.fori_loop(0, _E, drain, 0)
        _zero_unused_blocks(zblk, xs_ref, tail_s[2 * _E], fill_sem)
        if nt > 1:
            wait_tile(i - 1, 1 - slot)
        wait_tile(i, slot)


def _dispatch(lo, goff, nch, tail, tsum, hn, route_t, n_rows):
    s = hn.shape[0]
    tt = min(_TT, s)
    nt = s // tt
    cap = tt * _K + _E * _ALIGN
    grid_spec = pltpu.PrefetchScalarGridSpec(
        num_scalar_prefetch=5,
        grid=(nt,),
        in_specs=[
            pl.BlockSpec((tt, _D), lambda i, *_: (i, 0)),
            pl.BlockSpec((_RT_ROWS, tt), lambda i, *_: (0, i)),
        ],
        out_specs=pl.BlockSpec(memory_space=pl.ANY),
        scratch_shapes=[pltpu.VMEM((2, cap, _D // 2), jnp.int32),
                        pltpu.VMEM((_BLK, _D // 2), jnp.int32), pltpu.SemaphoreType.DMA((3,))],
    )
    return pl.pallas_call(
        functools.partial(_dispatch_kernel, nt=nt),
        grid_spec=grid_spec,
        out_shape=jax.ShapeDtypeStruct((n_rows, _D // 2), jnp.int32),
        compiler_params=pltpu.CompilerParams(dimension_semantics=("arbitrary",),
                                             vmem_limit_bytes=_VMEM_LIMIT),
        name="moe_dispatch",
    )(lo, goff, nch, tail, tsum, hn, route_t)


def _expert_kernel(tail_s, wgu_ref, bgu_ref, wd_ref, bd_ref, xs_ref, ys_ref,
                   wgu_c, wd_c, xin, yout, zblk, sem_in, sem_out):
    e = pl.program_id(0)
    used = tail_s[2 * _E]
    first = tail_s[3 * _E + e]
    nblk = tail_s[4 * _E + e]
    half = _D // 2

    def rows(g):
        return pl.ds(pl.multiple_of(g * _BLK, _BLK), _BLK)

    def in_copy(g, slot):
        return pltpu.make_async_copy(xs_ref.at[rows(g)], xin.at[slot], sem_in.at[slot])

    def out_copy(g, slot):
        return pltpu.make_async_copy(yout.at[slot], ys_ref.at[rows(g)], sem_out.at[slot])

    def slot_of(g):
        return lax.rem(g, _RING)

    @pl.when(e == 0)
    def _():
        for g in range(_AHEAD):
            @pl.when(g < used)
            def _(g=g):
                in_copy(g, g).start()

    @pl.when(nblk > 0)
    def _():
        step = _LANES
        for r in range(0, _D, step):
            wgu_c[r:r + step, :] = wgu_ref[0, 0, r:r + step, :].astype(_bf16)
        for r in range(0, _F, step):
            wd_c[r:r + step, :] = wd_ref[0, 0, r:r + step, :].astype(_bf16)

    def unit(g, n):
        for d in range(n):
            in_copy(g + d, slot_of(g + d)).wait()
        for d in range(_AHEAD, _AHEAD + n):
            @pl.when(g + d < used)
            def _(d=d):
                in_copy(g + d, slot_of(g + d)).start()
        parts = [_unpack_pairs(xin[slot_of(g + d)]) for d in range(n)]
        x_hi = jnp.concatenate([p[0] for p in parts], axis=0)
        x_lo = jnp.concatenate([p[1] for p in parts], axis=0)
        gu = _dot(x_hi, wgu_c[:half, :]) + _dot(x_lo, wgu_c[half:, :]) + bgu_ref[0, 0]
        x_glu = jnp.minimum(gu[:, :_F], _SWIGLU_LIMIT)
        x_lin = jnp.clip(gu[:, _F:], -_SWIGLU_LIMIT, _SWIGLU_LIMIT)
        act = x_glu * jax.nn.sigmoid(_SWIGLU_ALPHA * x_glu) * (x_lin + 1.0)
        y = _dot(act.astype(_bf16), wd_c[...]) + bd_ref[0, 0]
        packed = _pack_pairs(y.astype(_bf16).astype(_f32))
        for d in range(n):
            @pl.when(g + d >= _RING)
            def _(d=d):
                out_copy(g + d - _RING, slot_of(g + d)).wait()
            yout[slot_of(g + d)] = packed[d * _BLK:(d + 1) * _BLK, :]
            out_copy(g + d, slot_of(g + d)).start()

    def widest(j, carry):
        unit(first + _AHEAD * j, _AHEAD)
        return carry

    lax.fori_loop(0, nblk // _AHEAD, widest, 0)
    n = _AHEAD // 2
    while n >= 1:
        @pl.when((nblk & n) != 0)
        def _(n=n):
            unit(first + (nblk & (-2 * n)), n)
        n //= 2

    @pl.when(e == _E - 1)
    def _():
        for d in range(1, _RING + 1):
            @pl.when(used >= d)
            def _(d=d):
                out_copy(used - d, slot_of(used - d)).wait()

        zblk[...] = jnp.zeros_like(zblk)
        _zero_unused_blocks(zblk, ys_ref, used, sem_out.at[0])


def _experts(tail, xs, w_gu, b_gu, w_down, b_down, layer):
    n_rows = xs.shape[0]
    wsel = lambda e, *_: (layer, e, 0, 0)
    grid_spec = pltpu.PrefetchScalarGridSpec(
        num_scalar_prefetch=1,
        grid=(_E,),
        in_specs=[
            pl.BlockSpec((1, 1, _D, 2 * _F), wsel),
            pl.BlockSpec((1, 1, 1, 2 * _F), wsel),
            pl.BlockSpec((1, 1, _F, _D), wsel),
            pl.BlockSpec((1, 1, 1, _D), wsel),
            pl.BlockSpec(memory_space=pl.ANY),
        ],
        out_specs=pl.BlockSpec(memory_space=pl.ANY),
        scratch_shapes=[pltpu.VMEM((_D, 2 * _F), _bf16), pltpu.VMEM((_F, _D), _bf16),
                        pltpu.VMEM((_RING, _BLK, _D // 2), jnp.int32),
                        pltpu.VMEM((_RING, _BLK, _D // 2), jnp.int32),
                        pltpu.VMEM((_BLK, _D // 2), jnp.int32),
                        pltpu.SemaphoreType.DMA((_RING,)), pltpu.SemaphoreType.DMA((_RING,))],
    )
    depth = w_gu.shape[0]
    return pl.pallas_call(
        _expert_kernel,
        grid_spec=grid_spec,
        out_shape=jax.ShapeDtypeStruct((n_rows, _D // 2), jnp.int32),
        compiler_params=pltpu.CompilerParams(dimension_semantics=("arbitrary",),
                                             vmem_limit_bytes=_VMEM_LIMIT),
        name="moe_experts",
    )(tail, w_gu, b_gu.reshape(depth, _E, 1, 2 * _F), w_down, b_down.reshape(depth, _E, 1, _D), xs)


def _combine_kernel(lo_s, goff_s, nch_s, tsum_s, route_ref, x1_ref, mod_ref, fin_ref, ys_ref,
                    out_ref, ybuf, sem, *, final, nt):
    i = pl.program_id(0)
    tt = x1_ref.shape[0]
    cap = ybuf.shape[1]
    slot = lax.rem(i, 2)

    def fetch(tile, dst):
        def start_part(local_row, global_row, size):
            _part_copy(ys_ref, global_row, dst, local_row, size, sem).start()
        _for_each_part(lo_s, goff_s, nch_s, tile, start_part)

    @pl.when(i == 0)
    def _():
        ybuf[...] = jnp.zeros_like(ybuf)
        fetch(0, ybuf.at[0])

    _wait_rows(tsum_s[i], cap, lambda size: _part_copy(ys_ref, 0, ybuf.at[slot], 0, size, sem))

    @pl.when(i + 1 < nt)
    def _():
        fetch(i + 1, ybuf.at[1 - slot])

    route = route_ref[...]
    pos16 = [route[:, 2 * _K + k:2 * _K + k + 1].astype(jnp.int32).astype(jnp.int16)
             for k in range(_K)]
    gate16 = [route[:, _K + k:_K + k + 1].astype(_bf16) for k in range(_K)]

    cols = cap // _SORT_CHUNKS
    moe_hi = jnp.zeros((tt, _D // 2), _f32)
    moe_lo = jnp.zeros((tt, _D // 2), _f32)
    for c in range(_SORT_CHUNKS):
        lane_r = (lax.broadcasted_iota(jnp.int32, (tt, cols), 1) + c * cols).astype(jnp.int16)
        wmat = jnp.zeros((tt, cols), _bf16)
        for k in range(_K):
            wmat = jnp.where(lane_r == pos16[k], gate16[k], wmat)
        y_hi, y_lo = _unpack_pairs(ybuf[slot, c * cols:(c + 1) * cols, :])
        moe_hi = moe_hi + _dot(wmat, y_hi)
        moe_lo = moe_lo + _dot(wmat, y_lo)
    moe = jnp.concatenate([moe_hi, moe_lo], axis=1)
    x2 = x1_ref[...] + mod_ref[5:6, :] * moe
    if final:
        x2 = _rms(x2) * fin_ref[...]
    out_ref[...] = x2


def _combine(lo, goff, nch, tsum, route, x1, mod, final_norm, ys, final):
    s = x1.shape[0]
    tt = min(_TT, s)
    nt = s // tt
    cap = tt * _K + _E * _ALIGN
    grid_spec = pltpu.PrefetchScalarGridSpec(
        num_scalar_prefetch=4,
        grid=(nt,),
        in_specs=[
            pl.BlockSpec((tt, _LANES), lambda i, *_: (i, 0)),
            pl.BlockSpec((tt, _D), lambda i, *_: (i, 0)),
            pl.BlockSpec((6, _D), lambda i, *_: (0, 0)),
            pl.BlockSpec((1, _D), lambda i, *_: (0, 0)),
            pl.BlockSpec(memory_space=pl.ANY),
        ],
        out_specs=pl.BlockSpec((tt, _D), lambda i, *_: (i, 0)),
        scratch_shapes=[pltpu.VMEM((2, cap, _D // 2), jnp.int32), pltpu.SemaphoreType.DMA(())],
    )
    return pl.pallas_call(
        functools.partial(_combine_kernel, final=final, nt=nt),
        grid_spec=grid_spec,
        out_shape=jax.ShapeDtypeStruct((s, _D), _f32),
        compiler_params=pltpu.CompilerParams(dimension_semantics=("arbitrary",),
                                             vmem_limit_bytes=_VMEM_LIMIT),
        name="moe_combine",
    )(lo, goff, nch, tsum, route, x1, mod, final_norm.reshape(1, _D), ys)


def _moe(x1, hn2, logits, mod, final_norm, w_gu, b_gu, w_down, b_down, layer, final):
    s = x1.shape[0]
    tt = min(_TT, s)
    nt = s // tt
    ntp = -(-nt // _SUBLANES) * _SUBLANES
    n_rows = -(-(s * _K + (_ALIGN - 1) * nt * _E) // _BLK) * _BLK + _E * _BLK
    route, route_t, cnt = _route(logits)
    cnt = jnp.pad(cnt.reshape(nt, _E), ((0, ntp - nt), (0, 0)))
    lo, goff, nch, tail, tsum = _offsets(cnt)
    lo_s, goff_s, nch_s = (a[:nt].reshape(nt * _E) for a in (lo, goff, nch))
    tail_s = tail[:5].reshape(5 * _E)
    tsum_s = tsum[:nt, 0]
    xs = _dispatch(lo_s, goff_s, nch_s, tail_s, tsum_s, hn2, route_t, n_rows)
    ys = _experts(tail_s, xs, w_gu, b_gu, w_down, b_down, layer)
    return _combine(lo_s, goff_s, nch_s, tsum_s, route, x1, mod, final_norm, ys, final)


def _decay_tables(tb):
    t = np.arange(tb)
    tri = ((t[:, None] // _CH) == (t[None, :] // _CH)) & (t[None, :] <= t[:, None])
    rel = (t[:, None] - t[None, :]).astype(np.float64)
    lg = np.log(1.0 - 2.0 ** (-5.0 - np.arange(_HEADS, dtype=np.float64)))
    dmat = np.where(rel >= 0, np.exp(np.maximum(rel, 0.0)[None] * lg[:, None, None]), 0.0)
    return jnp.asarray(tri.astype(np.float32), _bf16), jnp.asarray(dmat.astype(np.float32))


def _rotary_tables(s, tb):
    half = _DH // 2
    inv = (np.float32(10000.0) ** (-np.arange(half, dtype=np.float32) / np.float32(half)))
    inv = inv.astype(np.float64)

    def cos_sin(positions):
        ang = positions.astype(np.float64)[:, None] * inv[None, :]
        cos = np.concatenate([np.cos(ang)] * 2, axis=1)
        sin = np.concatenate([np.sin(ang)] * 2, axis=1)
        return cos.astype(np.float32), sin.astype(np.float32)

    cos_a, sin_a = cos_sin(np.arange(s // tb) * tb)
    cos_b, sin_b = cos_sin(np.arange(tb))
    return (jnp.asarray(np.stack([cos_a, sin_a], axis=1)), jnp.asarray(np.stack([cos_b, sin_b], axis=0)))


def _lower_bounds_kernel(p_ref, o_ref):
    p = p_ref[...]
    n = p.shape[0]
    m = p[0:1, :]
    for j in range(1, n):
        m = jnp.maximum(m, p[j:j + 1, :])
    ex = [jnp.exp(p[j:j + 1, :] - m) for j in range(n)]
    den = ex[0]
    for j in range(1, n):
        den = den + ex[j]
    acc = jnp.zeros_like(den)
    rows = []
    for j in range(n):
        acc = acc + ex[j] / den
        rows.append(acc)
    o_ref[...] = jnp.concatenate(rows, axis=0)


def _lower_bounds(p):
    return pl.pallas_call(
        _lower_bounds_kernel,
        out_shape=jax.ShapeDtypeStruct(p.shape, _f32),
        name="hgrn2_lower_bounds",
    )(p)


def kernel(x, c, w_ada, b_ada, w_in, hg_lower_bounds, hg_norm, ret_norm, w_out, pool_w, pool_b,
           pool_scale, w_router, b_router, w_gu, b_gu, w_down, b_down, final_norm):
    b, s, d = x.shape
    assert b == 1 and d == _D
    depth = w_ada.shape[0]
    assert depth == 2
    xs = x.reshape(s, d)
    mod = _ada(c, w_ada, b_ada)
    lb_all = _lower_bounds(hg_lower_bounds)
    x1, hn2, logits = _mixer0(xs, mod[0], w_in[0], lb_all[0], hg_norm[0], ret_norm[0], w_out[0],
                              w_router[0], b_router[0])
    xs = _moe(x1, hn2, logits, mod[0], final_norm, w_gu, b_gu, w_down, b_down, 0, False)
    x1, hn2, logits = _pool_mixer(xs, mod[1], pool_w[0], pool_b[0], pool_scale[0], w_router[1],
                                  b_router[1])
    xs = _moe(x1, hn2, logits, mod[1], final_norm, w_gu, b_gu, w_down, b_down, 1, True)
    return xs.reshape(b, s, d)
```

```python
import functools
import math

import jax
import jax.numpy as jnp
import numpy as np
from jax import lax
from jax.experimental import pallas as pl
from jax.experimental.pallas import tpu as pltpu

_LANES = 128
_SUBLANES = 8
_D = 1024
_E = 32
_K = 4
_F = 1024
_HEADS = 4
_DH = 128
_CH = 64
_SUB = 16
_EPS = 1e-6
_BLK = 256
_TB = 256
_TBP = 512
_TT = 512
_ALIGN = _SUBLANES
_RT_ROWS = 16
_BULK = 64
_SORT_CHUNKS = 3
_AHEAD = 4
_RING = 2 * _AHEAD
_NEG = -1e30
_LOG2E = 1.4426950408889634
_SWIGLU_LIMIT = 7.0
_SWIGLU_ALPHA = 1.702
_POOL_WINDOWS = (2, 4, 8, 16)
_HALO = 16
_VMEM_BYTES = 64 * 1024 * 1024
_VMEM_LIMIT = _VMEM_BYTES // 8 * 7

_f32 = jnp.float32
_bf16 = jnp.bfloat16


def _dot(a, b):
    return jnp.dot(a, b, preferred_element_type=_f32)


def _dot_nt(a, b):
    return lax.dot_general(a, b, (((1,), (1,)), ((), ())), preferred_element_type=_f32)


def _split3(a):
    a0 = a.astype(_bf16)
    r1 = a - a0.astype(_f32)
    a1 = r1.astype(_bf16)
    a2 = (r1 - a1.astype(_f32)).astype(_bf16)
    return a0, a1, a2


def _dot_exact_lhs(m, a):
    a0, a1, a2 = _split3(a)
    return _dot(m, a0) + _dot(m, a1) + _dot(m, a2)


def _rms(x):
    return x * lax.rsqrt(jnp.mean(x * x, axis=-1, keepdims=True) + _EPS)


def _silu(x):
    return x * jax.nn.sigmoid(x)


def _ada_kernel(c_ref, w_ref, b_ref, o_ref):
    c = c_ref[...]
    cond = _silu(c)
    o_ref[0] = jnp.sum(w_ref[0] * cond, axis=0, keepdims=True) + b_ref[0]


def _ada(c, w_ada, b_ada):
    depth = w_ada.shape[0]
    n = depth * 6
    out = pl.pallas_call(
        _ada_kernel,
        grid=(n,),
        in_specs=[
            pl.BlockSpec((_D, 1), lambda j: (0, 0)),
            pl.BlockSpec((1, _D, _D), lambda j: (j // 6, 0, j % 6)),
            pl.BlockSpec((1, 1, _D), lambda j: (j, 0, 0)),
        ],
        out_specs=pl.BlockSpec((1, 1, _D), lambda j: (j, 0, 0)),
        out_shape=jax.ShapeDtypeStruct((n, 1, _D), _f32),
        name="ada_mod",
    )(c.reshape(_D, 1), w_ada, b_ada.reshape(n, 1, _D))
    return out.reshape(depth, 6, _D)


def _ffn_prologue(x1, mod, wr_ref, br_ref, x1_ref, hn2_ref, lg_ref):
    sh_f, sc_f = mod[3:4, :], mod[4:5, :]
    hn2 = _rms(x1) * (1.0 + sc_f) + sh_f
    x1_ref[...] = x1
    a0, a1, _ = _split3(hn2)
    hn2_ref[...] = a0
    w0, w1, _ = _split3(wr_ref[...])
    both = _dot(a0, jnp.concatenate([w0, w1], axis=1))
    lg_ref[...] = (both[:, :_E] + both[:, _E:] + _dot(a1, w0) + br_ref[...]).T


def _mixer0_kernel(x_ref, mod_ref, win_ref, lb_ref, hgg_ref, rtg_ref, wout_ref, rot_a_ref, rot_b_ref,
                   tri_ref, dmat_ref, wr_ref, br_ref, x1_ref, hn2_ref, lg_ref,
                   q_s, k_s, v_s, b_s, o_s, st_hg, st_rt):
    tb = x_ref.shape[0]
    nch = tb // _CH
    nsub = _CH // _SUB

    @pl.when(pl.program_id(0) == 0)
    def _():
        st_hg[...] = jnp.zeros_like(st_hg)
        st_rt[...] = jnp.zeros_like(st_rt)

    x = x_ref[...]
    mod = mod_ref[...]
    sh_m, sc_m, g_m = mod[0:1, :], mod[1:2, :], mod[2:3, :]
    hb = (_rms(x) * (1.0 + sc_m) + sh_m).astype(_bf16)
    width = _HEADS * _DH

    def sec(j):
        return _dot(hb, win_ref[:, j * width:(j + 1) * width])

    hq = sec(0)
    lb = lb_ref[...]
    fg = lb + (1.0 - lb) * jax.nn.sigmoid(sec(1))
    hi = sec(2)
    g = jnp.log(fg) * _LOG2E
    bcum = _dot_exact_lhs(tri_ref[...], g)
    qh = _silu(hq)
    kk = 1.0 - fg
    for h in range(_HEADS):
        sl = slice(h * _DH, (h + 1) * _DH)
        q_s[h] = qh[:, sl]
        k_s[h] = kk[:, sl]
        v_s[h] = hi[:, sl]
        b_s[h] = bcum[:, sl]

    hgg = hgg_ref[...]
    row64 = lax.broadcasted_iota(jnp.int32, (_HEADS, _CH, _DH), 1)
    row16 = lax.broadcasted_iota(jnp.int32, (_HEADS * nsub, _SUB, 1), 1)
    lead16 = lax.broadcasted_iota(jnp.int32, (_HEADS * nsub, _SUB, _CH), 0)
    col16 = lax.broadcasted_iota(jnp.int32, (_HEADS * nsub, _SUB, _CH), 2) - _SUB * (lead16 & (nsub - 1))

    def chunk(c, carry):
        rows = pl.ds(pl.multiple_of(c * _CH, _CH), _CH)
        q = q_s[:, rows, :]
        k = k_s[:, rows, :]
        v = v_s[:, rows, :]
        b = b_s[:, rows, :]
        st = st_hg[...]
        vb = v.astype(_bf16)
        qe = (q * jnp.exp2(b)).astype(_bf16)
        o = jnp.einsum('htd,hed->hte', qe, st.astype(_bf16), preferred_element_type=_f32)
        blocks = [jnp.zeros((_HEADS, _SUB, _CH), _f32)]
        for i in range(1, nsub):
            ref = b[:, _SUB * i - 1:_SUB * i, :]
            qp = q[:, _SUB * i:_SUB * (i + 1), :] * jnp.exp2(b[:, _SUB * i:_SUB * (i + 1), :] - ref)
            kp = k * jnp.exp2(jnp.where(row64 < _SUB * i, ref - b, _NEG))
            blocks.append(jnp.einsum('htd,hsd->hts', qp.astype(_bf16), kp.astype(_bf16),
                                     preferred_element_type=_f32))
        a = jnp.concatenate(blocks, axis=1)
        q3 = q.reshape(_HEADS * nsub, _SUB, _DH)
        k3 = k.reshape(_HEADS * nsub, _SUB, _DH)
        b3 = b.reshape(_HEADS * nsub, _SUB, _DH)
        ad = jnp.zeros((_HEADS * nsub, _SUB, _CH), _f32)
        for s in range(_SUB):
            e = jnp.exp2(b3 - b3[:, s:s + 1, :])
            w = jnp.sum(q3 * e * k3[:, s:s + 1, :], axis=-1, keepdims=True)
            ad = jnp.where((col16 == s) & (row16 >= s), w, ad)
        a = a + ad.reshape(_HEADS, _CH, _CH)
        o = o + jnp.einsum('hts,hse->hte', a.astype(_bf16), vb, preferred_element_type=_f32)
        bl = b[:, _CH - 1:_CH, :]
        kd = (k * jnp.exp2(bl - b)).astype(_bf16)
        dec = jnp.exp2(bl)
        for h in range(_HEADS):
            st_hg[h] = st[h] * dec[h] + _dot(v[h].T.astype(_bf16), kd[h])
        o = o * lax.rsqrt(jnp.mean(o * o, axis=-1, keepdims=True) + _EPS) * hgg
        o_s[:, rows, :] = o
        return carry

    lax.fori_loop(0, nch, chunk, 0, unroll=True)
    hog = sec(3)
    cat_hg = jnp.concatenate([o_s[h] for h in range(_HEADS)], axis=1) * _silu(hog)

    rq = sec(4)
    rk = sec(5)
    rv = sec(6)
    cos_a, sin_a = rot_a_ref[0, 0:1, :], rot_a_ref[0, 1:2, :]
    cos_b, sin_b = rot_b_ref[0], rot_b_ref[1]
    first_half = lax.broadcasted_iota(jnp.int32, (1, _DH), 1) < _DH // 2
    cosf = cos_a * cos_b - sin_a * sin_b
    sinf = (sin_a * cos_b + cos_a * sin_b) * jnp.where(first_half, -1.0, 1.0)
    tcol = lax.broadcasted_iota(jnp.int32, (tb, 1), 0).astype(_f32)
    rtg = rtg_ref[...]
    outs = []
    for h in range(_HEADS):
        sl = slice(h * _DH, (h + 1) * _DH)
        lg = math.log(1.0 - 2.0 ** (-5.0 - h))
        xq = rq[:, sl]
        xk = rk[:, sl]
        q = xq * cosf + pltpu.roll(xq, _DH // 2, 1) * sinf
        k = (xk * cosf + pltpu.roll(xk, _DH // 2, 1) * sinf) * (_DH ** -0.5)
        v = rv[:, sl]
        scores = _dot_nt(q.astype(_bf16), k.astype(_bf16)) * dmat_ref[h]
        o = _dot(scores.astype(_bf16), v.astype(_bf16))
        st = st_rt[h]
        qd = q * jnp.exp((tcol + 1.0) * lg)
        o = o + _dot_nt(qd.astype(_bf16), st.astype(_bf16))
        kd = k * jnp.exp((tb - 1.0 - tcol) * lg)
        st_rt[h] = st * math.exp(tb * lg) + _dot(v.T.astype(_bf16), kd.astype(_bf16))
        o = o * lax.rsqrt(jnp.mean(o * o, axis=-1, keepdims=True) + _EPS) * rtg[h]
        outs.append(o)
    rg = sec(7)
    cat_rt = jnp.concatenate(outs, axis=1) * _silu(rg)

    cat = jnp.concatenate([cat_hg, cat_rt], axis=1).astype(_bf16)
    mix = _dot(cat, wout_ref[...])
    x1 = x + g_m * mix
    _ffn_prologue(x1, mod, wr_ref, br_ref, x1_ref, hn2_ref, lg_ref)


def _mixer0(x, mod, w_in, lb, hg_gain, ret_gain, w_out, w_router, b_router):
    s = x.shape[0]
    tb = min(_TB, s)
    rot_a, rot_b = _rotary_tables(s, tb)
    tri, dmat = _decay_tables(tb)
    width = _HEADS * _DH
    full = lambda shape: pl.BlockSpec(shape, lambda i: (0,) * len(shape))
    rowblk = lambda w: pl.BlockSpec((tb, w), lambda i: (i, 0))
    return pl.pallas_call(
        _mixer0_kernel,
        grid=(s // tb,),
        in_specs=[
            rowblk(_D), full((6, _D)), full((_D, 8 * width)), full((1, width)),
            full((_HEADS, 1, _DH)), full((_HEADS, 1, _DH)), full((2 * width, _D)),
            pl.BlockSpec((1, 2, _DH), lambda i: (i, 0, 0)), full((2, tb, _DH)),
            full((tb, tb)), full((_HEADS, tb, tb)), full((_D, _E)), full((1, _E)),
        ],
        out_specs=[rowblk(_D), rowblk(_D), pl.BlockSpec((_E, tb), lambda i: (0, i))],
        out_shape=[jax.ShapeDtypeStruct((s, _D), _f32), jax.ShapeDtypeStruct((s, _D), _bf16),
                   jax.ShapeDtypeStruct((_E, s), _f32)],
        scratch_shapes=[pltpu.VMEM((_HEADS, tb, _DH), _f32)] * 5
        + [pltpu.VMEM((_HEADS, _DH, _DH), _f32)] * 2,
        compiler_params=pltpu.CompilerParams(dimension_semantics=("arbitrary",),
                                             vmem_limit_bytes=_VMEM_LIMIT),
        name="mixer_hgrn2_retention",
    )(x, mod, w_in.astype(_bf16), lb.reshape(1, width), hg_gain.reshape(_HEADS, 1, _DH),
      ret_gain.reshape(_HEADS, 1, _DH), w_out.astype(_bf16), rot_a, rot_b,
      tri, dmat, w_router,
      b_router.reshape(1, _E))


def _pool_kernel(x_ref, mod_ref, pw_ref, pb_ref, ps_ref, wr_ref, br_ref,
                 x1_ref, hn2_ref, lg_ref, halo):
    tb = x_ref.shape[0]
    i = pl.program_id(0)

    @pl.when(i == 0)
    def _():
        halo[...] = jnp.zeros_like(halo)

    x = x_ref[...]
    mod = mod_ref[...]
    sh_m, sc_m, g_m = mod[0:1, :], mod[1:2, :], mod[2:3, :]
    hn = _rms(x) * (1.0 + sc_m) + sh_m
    ext = jnp.concatenate([halo[...], hn], axis=0)
    halo[...] = hn[tb - _HALO:, :]
    t = (i * tb + lax.broadcasted_iota(jnp.int32, (tb, 1), 0)).astype(_f32)
    gw = _D // len(_POOL_WINDOWS)
    ys = []
    for gi, w in enumerate(_POOL_WINDOWS):
        acc = ext[:, gi * gw:(gi + 1) * gw]
        shift = 1
        while shift < w:
            acc = acc + pltpu.roll(acc, shift, 0)
            shift *= 2
        win = acc[_HALO:, :]
        inv_cnt = 1.0 / jnp.minimum(t + 1.0, float(w))
        p = win * inv_cnt - hn[:, gi * gw:(gi + 1) * gw]
        ys.append(_dot(p.astype(_bf16), pw_ref[gi]) + pb_ref[gi])
    mix = jnp.concatenate(ys, axis=1) * ps_ref[...]
    x1 = x + g_m * mix
    _ffn_prologue(x1, mod, wr_ref, br_ref, x1_ref, hn2_ref, lg_ref)


def _pool_mixer(x, mod, pool_w, pool_b, pool_scale, w_router, b_router):
    s = x.shape[0]
    tb = min(_TBP, s)
    ng = len(_POOL_WINDOWS)
    gw = _D // ng
    full = lambda shape: pl.BlockSpec(shape, lambda i: (0,) * len(shape))
    rowblk = lambda w: pl.BlockSpec((tb, w), lambda i: (i, 0))
    return pl.pallas_call(
        _pool_kernel,
        grid=(s // tb,),
        in_specs=[rowblk(_D), full((6, _D)), full((ng, gw, gw)), full((ng, 1, gw)), full((1, _D)),
                  full((_D, _E)), full((1, _E))],
        out_specs=[rowblk(_D), rowblk(_D), pl.BlockSpec((_E, tb), lambda i: (0, i))],
        out_shape=[jax.ShapeDtypeStruct((s, _D), _f32), jax.ShapeDtypeStruct((s, _D), _bf16),
                   jax.ShapeDtypeStruct((_E, s), _f32)],
        scratch_shapes=[pltpu.VMEM((_HALO, _D), _f32)],
        compiler_params=pltpu.CompilerParams(dimension_semantics=("arbitrary",),
                                             vmem_limit_bytes=_VMEM_LIMIT),
        name="mixer_pool",
    )(x, mod, pool_w.astype(_bf16), pool_b.reshape(ng, 1, gw), pool_scale.reshape(1, _D),
      w_router, b_router.reshape(1, _E))


def _route_kernel(lg_ref, route_ref, route_t_ref, cnt_ref):
    tt = lg_ref.shape[1]
    vals = lg_ref[...]
    sub = lax.broadcasted_iota(jnp.int32, (_E, tt), 0).astype(_f32)
    tops, idxs, hots = [], [], []
    for _ in range(_K):
        m = jnp.max(vals, axis=0, keepdims=True)
        idx = jnp.min(jnp.where(vals == m, sub, float(_E)), axis=0, keepdims=True)
        hot = sub == idx
        vals = jnp.where(hot, -jnp.inf, vals)
        tops.append(m)
        idxs.append(idx)
        hots.append(hot)
    ex = [jnp.exp(m - tops[0]) for m in tops]
    den = ex[0] + ex[1] + ex[2] + ex[3]
    member = jnp.zeros((_E, tt), _f32)
    for hot in hots:
        member = member + jnp.where(hot, 1.0, 0.0)
    member_b = member.astype(_bf16)
    s_i = lax.broadcasted_iota(jnp.int32, (tt, tt), 0)
    t_i = lax.broadcasted_iota(jnp.int32, (tt, tt), 1)
    earlier = jnp.where(s_i < t_i, 1.0, 0.0).astype(_bf16)
    cum = _dot(member_b, earlier)
    cnt_ref[0] = _dot_nt(jnp.ones((_SUBLANES, tt), _bf16), member_b)[0:1, :]
    cnt = jnp.sum(member, axis=1, keepdims=True)
    r8 = jnp.floor((cnt + (_ALIGN - 1.0)) * (1.0 / _ALIGN)) * _ALIGN
    e_i = lax.broadcasted_iota(jnp.int32, (_E, _E), 0)
    j_i = lax.broadcasted_iota(jnp.int32, (_E, _E), 1)
    before = jnp.where(j_i < e_i, 1.0, 0.0).astype(_bf16)
    lo = _dot(before, jnp.broadcast_to(r8, (_E, _LANES)).astype(_bf16))[:, 0:1]
    row = cum + lo
    out_row = lax.broadcasted_iota(jnp.int32, (_RT_ROWS, tt), 0)
    out = jnp.zeros((_RT_ROWS, tt), _f32)
    for k in range(_K):
        pos = jnp.sum(jnp.where(hots[k], row, 0.0), axis=0, keepdims=True)
        out = jnp.where(out_row == k, idxs[k], out)
        out = jnp.where(out_row == _K + k, ex[k] / den, out)
        out = jnp.where(out_row == 2 * _K + k, pos, out)
    route_t_ref[...] = out
    route_ref[...] = jnp.zeros_like(route_ref)
    route_ref[:, 0:_RT_ROWS] = out.T


def _route(logits_t):
    s = logits_t.shape[1]
    tt = min(_TT, s)
    nt = s // tt
    return pl.pallas_call(
        _route_kernel,
        grid=(nt,),
        in_specs=[pl.BlockSpec((_E, tt), lambda i: (0, i))],
        out_specs=[pl.BlockSpec((tt, _LANES), lambda i: (i, 0)),
                   pl.BlockSpec((_RT_ROWS, tt), lambda i: (0, i)),
                   pl.BlockSpec((1, 1, _E), lambda i: (i, 0, 0))],
        out_shape=[jax.ShapeDtypeStruct((s, _LANES), _f32),
                   jax.ShapeDtypeStruct((_RT_ROWS, s), _f32),
                   jax.ShapeDtypeStruct((nt, 1, _E), _f32)],
        compiler_params=pltpu.CompilerParams(dimension_semantics=("arbitrary",)),
        name="route_topk_rank",
    )(logits_t)


def _offsets_kernel(cnt_ref, lo_ref, goff_ref, nch_ref, tail_ref, tsum_ref):
    cnt = cnt_ref[...]
    ntp = cnt.shape[0]
    r8 = jnp.floor((cnt + (_ALIGN - 1.0)) * (1.0 / _ALIGN)) * _ALIGN
    j_i = lax.broadcasted_iota(jnp.int32, (_E, _E), 0)
    e_i = lax.broadcasted_iota(jnp.int32, (_E, _E), 1)
    before = jnp.where(j_i < e_i, 1.0, 0.0).astype(_bf16)

    def times_before(a):
        a0, a1, a2 = _split3(a)
        return _dot(a0, before) + _dot(a1, before) + _dot(a2, before)

    lo = times_before(r8)
    tot = jnp.sum(r8, axis=0, keepdims=True)
    cap = jnp.floor((tot + (_BLK - 1.0)) * (1.0 / _BLK)) * _BLK
    start = times_before(jnp.broadcast_to(cap, (_SUBLANES, _E)))[0:1, :]
    a_i = lax.broadcasted_iota(jnp.int32, (ntp, ntp), 0)
    b_i = lax.broadcasted_iota(jnp.int32, (ntp, ntp), 1)
    earlier = jnp.where(b_i < a_i, 1.0, 0.0).astype(_bf16)
    goff = start + _dot_exact_lhs(earlier, r8)
    lo_ref[...] = lo.astype(jnp.int32)
    goff_ref[...] = goff.astype(jnp.int32)
    nch_ref[...] = (r8 * (1.0 / _ALIGN)).astype(jnp.int32)
    trow = lax.broadcasted_iota(jnp.int32, (_SUBLANES, _E), 0)
    end = start + cap
    used_blocks = jnp.max(end, axis=-1, keepdims=True) * (1.0 / _BLK)
    tail = jnp.where(trow == 0, start + tot,
                     jnp.where(trow == 1, (cap - tot) * (1.0 / _ALIGN),
                               jnp.where(trow == 2, used_blocks,
                                         jnp.where(trow == 3, start * (1.0 / _BLK),
                                                   jnp.where(trow == 4, cap * (1.0 / _BLK), 0.0)))))
    tail_ref[...] = tail.astype(jnp.int32)
    tsum = jnp.sum(r8, axis=-1, keepdims=True)
    tsum_ref[...] = jnp.broadcast_to(tsum, (ntp, _E)).astype(jnp.int32)


def _offsets(cnt):
    ntp = cnt.shape[0]
    i32 = jnp.int32
    return pl.pallas_call(
        _offsets_kernel,
        out_shape=[jax.ShapeDtypeStruct((ntp, _E), i32),
                   jax.ShapeDtypeStruct((ntp, _E), i32), jax.ShapeDtypeStruct((ntp, _E), i32),
                   jax.ShapeDtypeStruct((_SUBLANES, _E), i32), jax.ShapeDtypeStruct((ntp, _E), i32)],
        name="route_offsets",
    )(cnt)


_HI_MASK = -65536


def _pack_pairs(x):
    half = x.shape[1] // 2
    bits = lax.bitcast_convert_type(x, jnp.int32)
    return (bits[:, :half] & _HI_MASK) | lax.shift_right_logical(bits[:, half:], 16)


def _unpack_pairs(p):
    hi = lax.bitcast_convert_type(p & _HI_MASK, _f32).astype(_bf16)
    lo = lax.bitcast_convert_type(lax.shift_left(p, 16), _f32).astype(_bf16)
    return hi, lo


def _for_each_part(lo_s, goff_s, nch_s, tile, fn):
    def per_expert(e, carry):
        rows = nch_s[tile * _E + e] * _ALIGN
        lo = lo_s[tile * _E + e]
        go = goff_s[tile * _E + e]

        def bulk(j, c):
            fn(lo + j * _BULK, go + j * _BULK, _BULK)
            return c

        lax.fori_loop(0, lax.shift_right_logical(rows, _BULK.bit_length() - 1), bulk, 0)
        size = _BULK // 2
        while size >= _ALIGN:
            @pl.when((rows & size) != 0)
            def _(size=size):
                off = rows & (-2 * size)
                fn(lo + off, go + off, size)
            size //= 2
        return carry
    lax.fori_loop(0, _E, per_expert, 0)


def _part_copy(src, src_row, dst, dst_row, size, sem):
    return pltpu.make_async_copy(src.at[pl.ds(pl.multiple_of(src_row, _ALIGN), size)],
                                 dst.at[pl.ds(pl.multiple_of(dst_row, _ALIGN), size)], sem)


def _wait_rows(total, limit, copy_of_size):
    size = _ALIGN
    while size * 2 <= limit:
        size *= 2
    while size >= _ALIGN:
        @pl.when((total & size) != 0)
        def _(size=size):
            copy_of_size(size).wait()
        size //= 2


def _zero_unused_blocks(zblk, dst_ref, first, sem, wait):
    nb = dst_ref.shape[0] // _BLK

    def one(b, carry):
        copy = pltpu.make_async_copy(
            zblk, dst_ref.at[pl.ds(pl.multiple_of(b * _BLK, _BLK), _BLK)], sem)
        if wait:
            copy.wait()
        else:
            copy.start()
        return carry

    lax.fori_loop(first, nb, one, 0)


def _dispatch_kernel(lo_s, goff_s, nch_s, tail_s, tsum_s, hn_ref, route_t_ref, xs_ref,
                     buf, zblk, sem, *, nt):
    i = pl.program_id(0)
    tt = hn_ref.shape[0]
    cap = buf.shape[1]
    rt = route_t_ref[...]
    pos16 = [rt[2 * _K + k:2 * _K + k + 1, :].astype(jnp.int32).astype(jnp.int16)
             for k in range(_K)]

    slot = lax.rem(i, 2)
    tile_buf = buf.at[slot]
    fill_sem = sem.at[2]

    def wait_tile(tile, sl):
        _wait_rows(tsum_s[tile], cap,
                   lambda size: _part_copy(buf.at[sl], 0, xs_ref, 0, size, sem.at[sl]))

    def zero_fill(wait):
        def per_expert(e, carry):
            def one(j, c):
                copy = _part_copy(zblk, 0, xs_ref, tail_s[e] + j * _ALIGN, _ALIGN, fill_sem)
                if wait:
                    copy.wait()
                else:
                    copy.start()
                return c
            return lax.fori_loop(0, tail_s[_E + e], one, carry)

        lax.fori_loop(0, _E, per_expert, 0)
        _zero_unused_blocks(zblk, xs_ref, tail_s[2 * _E], fill_sem, wait)

    @pl.when(i == 0)
    def _():
        zblk[...] = jnp.zeros_like(zblk)
        zero_fill(wait=False)

    @pl.when(i > 1)
    def _():
        wait_tile(i - 2, slot)

    rows = cap // _SORT_CHUNKS
    hn = hn_ref[...]
    for c in range(_SORT_CHUNKS):
        r_iota = (lax.broadcasted_iota(jnp.int32, (rows, tt), 0) + c * rows).astype(jnp.int16)
        perm = jnp.zeros((rows, tt), _bf16)
        for k in range(_K):
            perm = jnp.where(r_iota == pos16[k], jnp.ones((), _bf16), perm)
        buf[slot, c * rows:(c + 1) * rows, :] = _pack_pairs(_dot(perm, hn))

    def start_part(local_row, global_row, size):
        _part_copy(tile_buf, local_row, xs_ref, global_row, size, sem.at[slot]).start()

    _for_each_part(lo_s, goff_s, nch_s, i, start_part)

    @pl.when(i == nt - 1)
    def _():
        zero_fill(wait=True)
        if nt > 1:
            wait_tile(i - 1, 1 - slot)
        wait_tile(i, slot)


def _dispatch(lo, goff, nch, tail, tsum, hn, route_t, n_rows):
    s = hn.shape[0]
    tt = min(_TT, s)
    nt = s // tt
    cap = tt * _K + _E * _ALIGN
    grid_spec = pltpu.PrefetchScalarGridSpec(
        num_scalar_prefetch=5,
        grid=(nt,),
        in_specs=[
            pl.BlockSpec((tt, _D), lambda i, *_: (i, 0)),
            pl.BlockSpec((_RT_ROWS, tt), lambda i, *_: (0, i)),
        ],
        out_specs=pl.BlockSpec(memory_space=pl.ANY),
        scratch_shapes=[pltpu.VMEM((2, cap, _D // 2), jnp.int32),
                        pltpu.VMEM((_BLK, _D // 2), jnp.int32), pltpu.SemaphoreType.DMA((3,))],
    )
    return pl.pallas_call(
        functools.partial(_dispatch_kernel, nt=nt),
        grid_spec=grid_spec,
        out_shape=jax.ShapeDtypeStruct((n_rows, _D // 2), jnp.int32),
        compiler_params=pltpu.CompilerParams(dimension_semantics=("arbitrary",),
                                             vmem_limit_bytes=_VMEM_LIMIT),
        name="moe_dispatch",
    )(lo, goff, nch, tail, tsum, hn, route_t)


def _expert_kernel(tail_s, wgu_ref, bgu_ref, wd_ref, bd_ref, xs_ref, ys_ref,
                   wgu_c, wd_c, xin, yout, zblk, sem_in, sem_out):
    e = pl.program_id(0)
    used = tail_s[2 * _E]
    first = tail_s[3 * _E + e]
    nblk = tail_s[4 * _E + e]
    half = _D // 2

    def rows(g):
        return pl.ds(pl.multiple_of(g * _BLK, _BLK), _BLK)

    def in_copy(g, slot):
        return pltpu.make_async_copy(xs_ref.at[rows(g)], xin.at[slot], sem_in.at[slot])

    def out_copy(g, slot):
        return pltpu.make_async_copy(yout.at[slot], ys_ref.at[rows(g)], sem_out.at[slot])

    def slot_of(g):
        return lax.rem(g, _RING)

    @pl.when(e == 0)
    def _():
        for g in range(_AHEAD):
            @pl.when(g < used)
            def _(g=g):
                in_copy(g, g).start()
        zblk[...] = jnp.zeros_like(zblk)
        _zero_unused_blocks(zblk, ys_ref, used, sem_out.at[_RING], False)

    @pl.when(nblk > 0)
    def _():
        step = _LANES
        for r in range(0, _D, step):
            wgu_c[r:r + step, :] = wgu_ref[0, 0, r:r + step, :].astype(_bf16)
        for r in range(0, _F, step):
            wd_c[r:r + step, :] = wd_ref[0, 0, r:r + step, :].astype(_bf16)

    def unit(g, n):
        for d in range(n):
            in_copy(g + d, slot_of(g + d)).wait()
        for d in range(_AHEAD, _AHEAD + n):
            @pl.when(g + d < used)
            def _(d=d):
                in_copy(g + d, slot_of(g + d)).start()
        parts = [_unpack_pairs(xin[slot_of(g + d)]) for d in range(n)]
        x_hi = jnp.concatenate([p[0] for p in parts], axis=0)
        x_lo = jnp.concatenate([p[1] for p in parts], axis=0)
        gu = _dot(x_hi, wgu_c[:half, :]) + _dot(x_lo, wgu_c[half:, :]) + bgu_ref[0, 0]
        x_glu = jnp.minimum(gu[:, :_F], _SWIGLU_LIMIT)
        x_lin = jnp.clip(gu[:, _F:], -_SWIGLU_LIMIT, _SWIGLU_LIMIT)
        act = x_glu * jax.nn.sigmoid(_SWIGLU_ALPHA * x_glu) * (x_lin + 1.0)
        y = _dot(act.astype(_bf16), wd_c[...]) + bd_ref[0, 0]
        packed = _pack_pairs(y.astype(_bf16).astype(_f32))
        for d in range(n):
            @pl.when(g + d >= _RING)
            def _(d=d):
                out_copy(g + d - _RING, slot_of(g + d)).wait()
            yout[slot_of(g + d)] = packed[d * _BLK:(d + 1) * _BLK, :]
            out_copy(g + d, slot_of(g + d)).start()

    def widest(j, carry):
        unit(first + _AHEAD * j, _AHEAD)
        return carry

    lax.fori_loop(0, nblk // _AHEAD, widest, 0)
    n = _AHEAD // 2
    while n >= 1:
        @pl.when((nblk & n) != 0)
        def _(n=n):
            unit(first + (nblk & (-2 * n)), n)
        n //= 2

    @pl.when(e == _E - 1)
    def _():
        for d in range(1, _RING + 1):
            @pl.when(used >= d)
            def _(d=d):
                out_copy(used - d, slot_of(used - d)).wait()

        _zero_unused_blocks(zblk, ys_ref, used, sem_out.at[_RING], True)


def _experts(tail, xs, w_gu, b_gu, w_down, b_down, layer):
    n_rows = xs.shape[0]
    wsel = lambda e, *_: (layer, e, 0, 0)
    grid_spec = pltpu.PrefetchScalarGridSpec(
        num_scalar_prefetch=1,
        grid=(_E,),
        in_specs=[
            pl.BlockSpec((1, 1, _D, 2 * _F), wsel),
            pl.BlockSpec((1, 1, 1, 2 * _F), wsel),
            pl.BlockSpec((1, 1, _F, _D), wsel),
            pl.BlockSpec((1, 1, 1, _D), wsel),
            pl.BlockSpec(memory_space=pl.ANY),
        ],
        out_specs=pl.BlockSpec(memory_space=pl.ANY),
        scratch_shapes=[pltpu.VMEM((_D, 2 * _F), _bf16), pltpu.VMEM((_F, _D), _bf16),
                        pltpu.VMEM((_RING, _BLK, _D // 2), jnp.int32),
                        pltpu.VMEM((_RING, _BLK, _D // 2), jnp.int32),
                        pltpu.VMEM((_BLK, _D // 2), jnp.int32),
                        pltpu.SemaphoreType.DMA((_RING,)), pltpu.SemaphoreType.DMA((_RING + 1,))],
    )
    depth = w_gu.shape[0]
    return pl.pallas_call(
        _expert_kernel,
        grid_spec=grid_spec,
        out_shape=jax.ShapeDtypeStruct((n_rows, _D // 2), jnp.int32),
        compiler_params=pltpu.CompilerParams(dimension_semantics=("arbitrary",),
                                             vmem_limit_bytes=_VMEM_LIMIT),
        name="moe_experts",
    )(tail, w_gu, b_gu.reshape(depth, _E, 1, 2 * _F), w_down, b_down.reshape(depth, _E, 1, _D), xs)


def _combine_kernel(lo_s, goff_s, nch_s, tsum_s, route_ref, x1_ref, mod_ref, fin_ref, ys_ref,
                    out_ref, ybuf, sem, *, final, nt):
    i = pl.program_id(0)
    tt = x1_ref.shape[0]
    cap = ybuf.shape[1]
    slot = lax.rem(i, 2)

    def fetch(tile, dst):
        def start_part(local_row, global_row, size):
            _part_copy(ys_ref, global_row, dst, local_row, size, sem).start()
        _for_each_part(lo_s, goff_s, nch_s, tile, start_part)

    @pl.when(i == 0)
    def _():
        ybuf[...] = jnp.zeros_like(ybuf)
        fetch(0, ybuf.at[0])

    _wait_rows(tsum_s[i], cap, lambda size: _part_copy(ys_ref, 0, ybuf.at[slot], 0, size, sem))

    @pl.when(i + 1 < nt)
    def _():
        fetch(i + 1, ybuf.at[1 - slot])

    route = route_ref[...]
    pos16 = [route[:, 2 * _K + k:2 * _K + k + 1].astype(jnp.int32).astype(jnp.int16)
             for k in range(_K)]
    gate16 = [route[:, _K + k:_K + k + 1].astype(_bf16) for k in range(_K)]

    cols = cap // _SORT_CHUNKS
    moe_hi = jnp.zeros((tt, _D // 2), _f32)
    moe_lo = jnp.zeros((tt, _D // 2), _f32)
    for c in range(_SORT_CHUNKS):
        lane_r = (lax.broadcasted_iota(jnp.int32, (tt, cols), 1) + c * cols).astype(jnp.int16)
        wmat = jnp.zeros((tt, cols), _bf16)
        for k in range(_K):
            wmat = jnp.where(lane_r == pos16[k], gate16[k], wmat)
        y_hi, y_lo = _unpack_pairs(ybuf[slot, c * cols:(c + 1) * cols, :])
        moe_hi = moe_hi + _dot(wmat, y_hi)
        moe_lo = moe_lo + _dot(wmat, y_lo)
    moe = jnp.concatenate([moe_hi, moe_lo], axis=1)
    x2 = x1_ref[...] + mod_ref[5:6, :] * moe
    if final:
        x2 = _rms(x2) * fin_ref[...]
    out_ref[...] = x2


def _combine(lo, goff, nch, tsum, route, x1, mod, final_norm, ys, final):
    s = x1.shape[0]
    tt = min(_TT, s)
    nt = s // tt
    cap = tt * _K + _E * _ALIGN
    grid_spec = pltpu.PrefetchScalarGridSpec(
        num_scalar_prefetch=4,
        grid=(nt,),
        in_specs=[
            pl.BlockSpec((tt, _LANES), lambda i, *_: (i, 0)),
            pl.BlockSpec((tt, _D), lambda i, *_: (i, 0)),
            pl.BlockSpec((6, _D), lambda i, *_: (0, 0)),
            pl.BlockSpec((1, _D), lambda i, *_: (0, 0)),
            pl.BlockSpec(memory_space=pl.ANY),
        ],
        out_specs=pl.BlockSpec((tt, _D), lambda i, *_: (i, 0)),
        scratch_shapes=[pltpu.VMEM((2, cap, _D // 2), jnp.int32), pltpu.SemaphoreType.DMA(())],
    )
    return pl.pallas_call(
        functools.partial(_combine_kernel, final=final, nt=nt),
        grid_spec=grid_spec,
        out_shape=jax.ShapeDtypeStruct((s, _D), _f32),
        compiler_params=pltpu.CompilerParams(dimension_semantics=("arbitrary",),
                                             vmem_limit_bytes=_VMEM_LIMIT),
        name="moe_combine",
    )(lo, goff, nch, tsum, route, x1, mod, final_norm.reshape(1, _D), ys)


def _moe(x1, hn2, logits, mod, final_norm, w_gu, b_gu, w_down, b_down, layer, final):
    s = x1.shape[0]
    tt = min(_TT, s)
    nt = s // tt
    ntp = -(-nt // _SUBLANES) * _SUBLANES
    n_rows = -(-(s * _K + (_ALIGN - 1) * nt * _E) // _BLK) * _BLK + _E * _BLK
    route, route_t, cnt = _route(logits)
    cnt = jnp.pad(cnt.reshape(nt, _E), ((0, ntp - nt), (0, 0)))
    lo, goff, nch, tail, tsum = _offsets(cnt)
    lo_s, goff_s, nch_s = (a[:nt].reshape(nt * _E) for a in (lo, goff, nch))
    tail_s = tail[:5].reshape(5 * _E)
    tsum_s = tsum[:nt, 0]
    xs = _dispatch(lo_s, goff_s, nch_s, tail_s, tsum_s, hn2, route_t, n_rows)
    ys = _experts(tail_s, xs, w_gu, b_gu, w_down, b_down, layer)
    return _combine(lo_s, goff_s, nch_s, tsum_s, route, x1, mod, final_norm, ys, final)


def _decay_tables(tb):
    t = np.arange(tb)
    tri = ((t[:, None] // _CH) == (t[None, :] // _CH)) & (t[None, :] <= t[:, None])
    rel = (t[:, None] - t[None, :]).astype(np.float64)
    lg = np.log(1.0 - 2.0 ** (-5.0 - np.arange(_HEADS, dtype=np.float64)))
    dmat = np.where(rel >= 0, np.exp(np.maximum(rel, 0.0)[None] * lg[:, None, None]), 0.0)
    return jnp.asarray(tri.astype(np.float32), _bf16), jnp.asarray(dmat.astype(np.float32))


def _rotary_tables(s, tb):
    half = _DH // 2
    inv = (np.float32(10000.0) ** (-np.arange(half, dtype=np.float32) / np.float32(half)))
    inv = inv.astype(np.float64)

    def cos_sin(positions):
        ang = positions.astype(np.float64)[:, None] * inv[None, :]
        cos = np.concatenate([np.cos(ang)] * 2, axis=1)
        sin = np.concatenate([np.sin(ang)] * 2, axis=1)
        return cos.astype(np.float32), sin.astype(np.float32)

    cos_a, sin_a = cos_sin(np.arange(s // tb) * tb)
    cos_b, sin_b = cos_sin(np.arange(tb))
    return (jnp.asarray(np.stack([cos_a, sin_a], axis=1)), jnp.asarray(np.stack([cos_b, sin_b], axis=0)))


def _lower_bounds_kernel(p_ref, o_ref):
    p = p_ref[...]
    n = p.shape[0]
    m = p[0:1, :]
    for j in range(1, n):
        m = jnp.maximum(m, p[j:j + 1, :])
    ex = [jnp.exp(p[j:j + 1, :] - m) for j in range(n)]
    den = ex[0]
    for j in range(1, n):
        den = den + ex[j]
    acc = jnp.zeros_like(den)
    rows = []
    for j in range(n):
        acc = acc + ex[j] / den
        rows.append(acc)
    o_ref[...] = jnp.concatenate(rows, axis=0)


def _lower_bounds(p):
    return pl.pallas_call(
        _lower_bounds_kernel,
        out_shape=jax.ShapeDtypeStruct(p.shape, _f32),
        name="hgrn2_lower_bounds",
    )(p)


def kernel(x, c, w_ada, b_ada, w_in, hg_lower_bounds, hg_norm, ret_norm, w_out, pool_w, pool_b,
           pool_scale, w_router, b_router, w_gu, b_gu, w_down, b_down, final_norm):
    b, s, d = x.shape
    assert b == 1 and d == _D
    depth = w_ada.shape[0]
    assert depth == 2
    xs = x.reshape(s, d)
    mod = _ada(c, w_ada, b_ada)
    lb_all = _lower_bounds(hg_lower_bounds)
    x1, hn2, logits = _mixer0(xs, mod[0], w_in[0], lb_all[0], hg_norm[0], ret_norm[0], w_out[0],
                              w_router[0], b_router[0])
    xs = _moe(x1, hn2, logits, mod[0], final_norm, w_gu, b_gu, w_down, b_down, 0, False)
    x1, hn2, logits = _pool_mixer(xs, mod[1], pool_w[0], pool_b[0], pool_scale[0], w_router[1],
                                  b_router[1])
    xs = _moe(x1, hn2, logits, mod[1], final_norm, w_gu, b_gu, w_down, b_down, 1, True)
    return xs.reshape(b, s, d)
```

```python
import functools
import math

import jax
import jax.numpy as jnp
import numpy as np
from jax import lax
from jax.experimental import pallas as pl
from jax.experimental.pallas import tpu as pltpu

_LANES = 128
_SUBLANES = 8
_D = 1024
_E = 32
_K = 4
_F = 1024
_HEADS = 4
_DH = 128
_CH = 64
_SUB = 16
_EPS = 1e-6
_BLK = 256
_TB = 256
_TBP = 512
_TT = 512
_ALIGN = _SUBLANES
_RT_ROWS = 16
_BULK = 64
_SORT_CHUNKS = 3
_AHEAD = 4
_RING = 2 * _AHEAD
_NEG = -1e30
_LOG2E = 1.4426950408889634
_SWIGLU_LIMIT = 7.0
_SWIGLU_ALPHA = 1.702
_POOL_WINDOWS = (2, 4, 8, 16)
_HALO = 16
_VMEM_BYTES = 64 * 1024 * 1024
_VMEM_LIMIT = _VMEM_BYTES // 8 * 7

_f32 = jnp.float32
_bf16 = jnp.bfloat16


def _dot(a, b):
    return jnp.dot(a, b, preferred_element_type=_f32)


def _dot_nt(a, b):
    return lax.dot_general(a, b, (((1,), (1,)), ((), ())), preferred_element_type=_f32)


def _split3(a):
    a0 = a.astype(_bf16)
    r1 = a - a0.astype(_f32)
    a1 = r1.astype(_bf16)
    a2 = (r1 - a1.astype(_f32)).astype(_bf16)
    return a0, a1, a2


def _dot_exact_lhs(m, a):
    a0, a1, a2 = _split3(a)
    return _dot(m, a0) + _dot(m, a1) + _dot(m, a2)


def _rms(x):
    return x * lax.rsqrt(jnp.mean(x * x, axis=-1, keepdims=True) + _EPS)


def _silu(x):
    return x * jax.nn.sigmoid(x)


def _ada_kernel(c_ref, w_ref, b_ref, o_ref):
    c = c_ref[...]
    cond = _silu(c)
    o_ref[0] = jnp.sum(w_ref[0] * cond, axis=0, keepdims=True) + b_ref[0]


def _ada(c, w_ada, b_ada):
    depth = w_ada.shape[0]
    n = depth * 6
    out = pl.pallas_call(
        _ada_kernel,
        grid=(n,),
        in_specs=[
            pl.BlockSpec((_D, 1), lambda j: (0, 0)),
            pl.BlockSpec((1, _D, _D), lambda j: (j // 6, 0, j % 6)),
            pl.BlockSpec((1, 1, _D), lambda j: (j, 0, 0)),
        ],
        out_specs=pl.BlockSpec((1, 1, _D), lambda j: (j, 0, 0)),
        out_shape=jax.ShapeDtypeStruct((n, 1, _D), _f32),
        name="ada_mod",
    )(c.reshape(_D, 1), w_ada, b_ada.reshape(n, 1, _D))
    return out.reshape(depth, 6, _D)


def _ffn_prologue(x1, mod, wr_ref, br_ref, x1_ref, hn2_ref, lg_ref):
    sh_f, sc_f = mod[3:4, :], mod[4:5, :]
    hn2 = _rms(x1) * (1.0 + sc_f) + sh_f
    x1_ref[...] = x1
    a0, a1, _ = _split3(hn2)
    hn2_ref[...] = a0
    w0, w1, _ = _split3(wr_ref[...])
    both = _dot(a0, jnp.concatenate([w0, w1], axis=1))
    lg_ref[...] = (both[:, :_E] + both[:, _E:] + _dot(a1, w0) + br_ref[...]).T


def _mixer0_kernel(x_ref, mod_ref, win_ref, lb_ref, hgg_ref, rtg_ref, wout_ref, rot_a_ref, rot_b_ref,
                   tri_ref, dmat_ref, wr_ref, br_ref, x1_ref, hn2_ref, lg_ref,
                   q_s, k_s, v_s, b_s, o_s, st_hg, st_rt):
    tb = x_ref.shape[0]
    nch = tb // _CH
    nsub = _CH // _SUB

    @pl.when(pl.program_id(0) == 0)
    def _():
        st_hg[...] = jnp.zeros_like(st_hg)
        st_rt[...] = jnp.zeros_like(st_rt)

    x = x_ref[...]
    mod = mod_ref[...]
    sh_m, sc_m, g_m = mod[0:1, :], mod[1:2, :], mod[2:3, :]
    hb = (_rms(x) * (1.0 + sc_m) + sh_m).astype(_bf16)
    width = _HEADS * _DH

    def sec(j):
        return _dot(hb, win_ref[:, j * width:(j + 1) * width])

    hq = sec(0)
    lb = lb_ref[...]
    fg = lb + (1.0 - lb) * jax.nn.sigmoid(sec(1))
    hi = sec(2)
    g = jnp.log(fg) * _LOG2E
    bcum = _dot_exact_lhs(tri_ref[...], g)
    qh = _silu(hq)
    kk = 1.0 - fg
    for h in range(_HEADS):
        sl = slice(h * _DH, (h + 1) * _DH)
        q_s[h] = qh[:, sl]
        k_s[h] = kk[:, sl]
        v_s[h] = hi[:, sl]
        b_s[h] = bcum[:, sl]

    hgg = hgg_ref[...]
    row64 = lax.broadcasted_iota(jnp.int32, (_HEADS, _CH, _DH), 1)
    row16 = lax.broadcasted_iota(jnp.int32, (_HEADS * nsub, _SUB, 1), 1)
    lead16 = lax.broadcasted_iota(jnp.int32, (_HEADS * nsub, _SUB, _CH), 0)
    col16 = lax.broadcasted_iota(jnp.int32, (_HEADS * nsub, _SUB, _CH), 2) - _SUB * (lead16 & (nsub - 1))

    def chunk(c, carry):
        rows = pl.ds(pl.multiple_of(c * _CH, _CH), _CH)
        q = q_s[:, rows, :]
        k = k_s[:, rows, :]
        v = v_s[:, rows, :]
        b = b_s[:, rows, :]
        st = st_hg[...]
        vb = v.astype(_bf16)
        qe = (q * jnp.exp2(b)).astype(_bf16)
        o = jnp.einsum('htd,hed->hte', qe, st.astype(_bf16), preferred_element_type=_f32)
        blocks = [jnp.zeros((_HEADS, _SUB, _CH), _f32)]
        for i in range(1, nsub):
            ref = b[:, _SUB * i - 1:_SUB * i, :]
            qp = q[:, _SUB * i:_SUB * (i + 1), :] * jnp.exp2(b[:, _SUB * i:_SUB * (i + 1), :] - ref)
            kp = k * jnp.exp2(jnp.where(row64 < _SUB * i, ref - b, _NEG))
            blocks.append(jnp.einsum('htd,hsd->hts', qp.astype(_bf16), kp.astype(_bf16),
                                     preferred_element_type=_f32))
        a = jnp.concatenate(blocks, axis=1)
        q3 = q.reshape(_HEADS * nsub, _SUB, _DH)
        k3 = k.reshape(_HEADS * nsub, _SUB, _DH)
        b3 = b.reshape(_HEADS * nsub, _SUB, _DH)
        ad = jnp.zeros((_HEADS * nsub, _SUB, _CH), _f32)
        for s in range(_SUB):
            e = jnp.exp2(b3 - b3[:, s:s + 1, :])
            w = jnp.sum(q3 * e * k3[:, s:s + 1, :], axis=-1, keepdims=True)
            ad = jnp.where((col16 == s) & (row16 >= s), w, ad)
        a = a + ad.reshape(_HEADS, _CH, _CH)
        o = o + jnp.einsum('hts,hse->hte', a.astype(_bf16), vb, preferred_element_type=_f32)
        bl = b[:, _CH - 1:_CH, :]
        kd = (k * jnp.exp2(bl - b)).astype(_bf16)
        dec = jnp.exp2(bl)
        for h in range(_HEADS):
            st_hg[h] = st[h] * dec[h] + _dot(v[h].T.astype(_bf16), kd[h])
        o = o * lax.rsqrt(jnp.mean(o * o, axis=-1, keepdims=True) + _EPS) * hgg
        o_s[:, rows, :] = o
        return carry

    lax.fori_loop(0, nch, chunk, 0, unroll=True)
    hog = sec(3)
    cat_hg = jnp.concatenate([o_s[h] for h in range(_HEADS)], axis=1) * _silu(hog)

    rq = sec(4)
    rk = sec(5)
    rv = sec(6)
    cos_a, sin_a = rot_a_ref[0, 0:1, :], rot_a_ref[0, 1:2, :]
    cos_b, sin_b = rot_b_ref[0], rot_b_ref[1]
    first_half = lax.broadcasted_iota(jnp.int32, (1, _DH), 1) < _DH // 2
    cosf = cos_a * cos_b - sin_a * sin_b
    sinf = (sin_a * cos_b + cos_a * sin_b) * jnp.where(first_half, -1.0, 1.0)
    tcol = lax.broadcasted_iota(jnp.int32, (tb, 1), 0).astype(_f32)
    rtg = rtg_ref[...]
    outs = []
    for h in range(_HEADS):
        sl = slice(h * _DH, (h + 1) * _DH)
        lg = math.log(1.0 - 2.0 ** (-5.0 - h))
        xq = rq[:, sl]
        xk = rk[:, sl]
        q = xq * cosf + pltpu.roll(xq, _DH // 2, 1) * sinf
        k = (xk * cosf + pltpu.roll(xk, _DH // 2, 1) * sinf) * (_DH ** -0.5)
        v = rv[:, sl]
        scores = _dot_nt(q.astype(_bf16), k.astype(_bf16)) * dmat_ref[h]
        o = _dot(scores.astype(_bf16), v.astype(_bf16))
        st = st_rt[h]
        qd = q * jnp.exp((tcol + 1.0) * lg)
        o = o + _dot_nt(qd.astype(_bf16), st.astype(_bf16))
        kd = k * jnp.exp((tb - 1.0 - tcol) * lg)
        st_rt[h] = st * math.exp(tb * lg) + _dot(v.T.astype(_bf16), kd.astype(_bf16))
        o = o * lax.rsqrt(jnp.mean(o * o, axis=-1, keepdims=True) + _EPS) * rtg[h]
        outs.append(o)
    rg = sec(7)
    cat_rt = jnp.concatenate(outs, axis=1) * _silu(rg)

    cat = jnp.concatenate([cat_hg, cat_rt], axis=1).astype(_bf16)
    mix = _dot(cat, wout_ref[...])
    x1 = x + g_m * mix
    _ffn_prologue(x1, mod, wr_ref, br_ref, x1_ref, hn2_ref, lg_ref)


def _mixer0(x, mod, w_in, lb, hg_gain, ret_gain, w_out, w_router, b_router):
    s = x.shape[0]
    tb = min(_TB, s)
    rot_a, rot_b = _rotary_tables(s, tb)
    tri, dmat = _decay_tables(tb)
    width = _HEADS * _DH
    full = lambda shape: pl.BlockSpec(shape, lambda i: (0,) * len(shape))
    rowblk = lambda w: pl.BlockSpec((tb, w), lambda i: (i, 0))
    return pl.pallas_call(
        _mixer0_kernel,
        grid=(s // tb,),
        in_specs=[
            rowblk(_D), full((6, _D)), full((_D, 8 * width)), full((1, width)),
            full((_HEADS, 1, _DH)), full((_HEADS, 1, _DH)), full((2 * width, _D)),
            pl.BlockSpec((1, 2, _DH), lambda i: (i, 0, 0)), full((2, tb, _DH)),
            full((tb, tb)), full((_HEADS, tb, tb)), full((_D, _E)), full((1, _E)),
        ],
        out_specs=[rowblk(_D), rowblk(_D), pl.BlockSpec((_E, tb), lambda i: (0, i))],
        out_shape=[jax.ShapeDtypeStruct((s, _D), _f32), jax.ShapeDtypeStruct((s, _D), _bf16),
                   jax.ShapeDtypeStruct((_E, s), _f32)],
        scratch_shapes=[pltpu.VMEM((_HEADS, tb, _DH), _f32)] * 5
        + [pltpu.VMEM((_HEADS, _DH, _DH), _f32)] * 2,
        compiler_params=pltpu.CompilerParams(dimension_semantics=("arbitrary",),
                                             vmem_limit_bytes=_VMEM_LIMIT),
        name="mixer_hgrn2_retention",
    )(x, mod, w_in.astype(_bf16), lb.reshape(1, width), hg_gain.reshape(_HEADS, 1, _DH),
      ret_gain.reshape(_HEADS, 1, _DH), w_out.astype(_bf16), rot_a, rot_b,
      tri, dmat, w_router,
      b_router.reshape(1, _E))


def _pool_kernel(x_ref, mod_ref, pw_ref, pb_ref, ps_ref, wr_ref, br_ref,
                 x1_ref, hn2_ref, lg_ref, halo):
    tb = x_ref.shape[0]
    i = pl.program_id(0)

    @pl.when(i == 0)
    def _():
        halo[...] = jnp.zeros_like(halo)

    x = x_ref[...]
    mod = mod_ref[...]
    sh_m, sc_m, g_m = mod[0:1, :], mod[1:2, :], mod[2:3, :]
    hn = _rms(x) * (1.0 + sc_m) + sh_m
    ext = jnp.concatenate([halo[...], hn], axis=0)
    halo[...] = hn[tb - _HALO:, :]
    t = (i * tb + lax.broadcasted_iota(jnp.int32, (tb, 1), 0)).astype(_f32)
    gw = _D // len(_POOL_WINDOWS)
    ys = []
    for gi, w in enumerate(_POOL_WINDOWS):
        acc = ext[:, gi * gw:(gi + 1) * gw]
        shift = 1
        while shift < w:
            acc = acc + pltpu.roll(acc, shift, 0)
            shift *= 2
        win = acc[_HALO:, :]
        inv_cnt = 1.0 / jnp.minimum(t + 1.0, float(w))
        p = win * inv_cnt - hn[:, gi * gw:(gi + 1) * gw]
        ys.append(_dot(p.astype(_bf16), pw_ref[gi]) + pb_ref[gi])
    mix = jnp.concatenate(ys, axis=1) * ps_ref[...]
    x1 = x + g_m * mix
    _ffn_prologue(x1, mod, wr_ref, br_ref, x1_ref, hn2_ref, lg_ref)


def _pool_mixer(x, mod, pool_w, pool_b, pool_scale, w_router, b_router):
    s = x.shape[0]
    tb = min(_TBP, s)
    ng = len(_POOL_WINDOWS)
    gw = _D // ng
    full = lambda shape: pl.BlockSpec(shape, lambda i: (0,) * len(shape))
    rowblk = lambda w: pl.BlockSpec((tb, w), lambda i: (i, 0))
    return pl.pallas_call(
        _pool_kernel,
        grid=(s // tb,),
        in_specs=[rowblk(_D), full((6, _D)), full((ng, gw, gw)), full((ng, 1, gw)), full((1, _D)),
                  full((_D, _E)), full((1, _E))],
        out_specs=[rowblk(_D), rowblk(_D), pl.BlockSpec((_E, tb), lambda i: (0, i))],
        out_shape=[jax.ShapeDtypeStruct((s, _D), _f32), jax.ShapeDtypeStruct((s, _D), _bf16),
                   jax.ShapeDtypeStruct((_E, s), _f32)],
        scratch_shapes=[pltpu.VMEM((_HALO, _D), _f32)],
        compiler_params=pltpu.CompilerParams(dimension_semantics=("arbitrary",),
                                             vmem_limit_bytes=_VMEM_LIMIT),
        name="mixer_pool",
    )(x, mod, pool_w.astype(_bf16), pool_b.reshape(ng, 1, gw), pool_scale.reshape(1, _D),
      w_router, b_router.reshape(1, _E))


def _route_kernel(lg_ref, route_ref, route_t_ref, cnt_ref):
    tt = lg_ref.shape[1]
    vals = lg_ref[...]
    sub = lax.broadcasted_iota(jnp.int32, (_E, tt), 0).astype(_f32)
    tops, idxs, hots = [], [], []
    for _ in range(_K):
        m = jnp.max(vals, axis=0, keepdims=True)
        idx = jnp.min(jnp.where(vals == m, sub, float(_E)), axis=0, keepdims=True)
        hot = sub == idx
        vals = jnp.where(hot, -jnp.inf, vals)
        tops.append(m)
        idxs.append(idx)
        hots.append(hot)
    ex = [jnp.exp(m - tops[0]) for m in tops]
    den = ex[0] + ex[1] + ex[2] + ex[3]
    member = jnp.zeros((_E, tt), _f32)
    for hot in hots:
        member = member + jnp.where(hot, 1.0, 0.0)
    member_b = member.astype(_bf16)
    s_i = lax.broadcasted_iota(jnp.int32, (tt, tt), 0)
    t_i = lax.broadcasted_iota(jnp.int32, (tt, tt), 1)
    earlier = jnp.where(s_i < t_i, 1.0, 0.0).astype(_bf16)
    cum = _dot(member_b, earlier)
    cnt_ref[0] = _dot_nt(jnp.ones((_SUBLANES, tt), _bf16), member_b)[0:1, :]
    cnt = jnp.sum(member, axis=1, keepdims=True)
    r8 = jnp.floor((cnt + (_ALIGN - 1.0)) * (1.0 / _ALIGN)) * _ALIGN
    e_i = lax.broadcasted_iota(jnp.int32, (_E, _E), 0)
    j_i = lax.broadcasted_iota(jnp.int32, (_E, _E), 1)
    before = jnp.where(j_i < e_i, 1.0, 0.0).astype(_bf16)
    lo = _dot(before, jnp.broadcast_to(r8, (_E, _LANES)).astype(_bf16))[:, 0:1]
    row = cum + lo
    out_row = lax.broadcasted_iota(jnp.int32, (_RT_ROWS, tt), 0)
    out = jnp.zeros((_RT_ROWS, tt), _f32)
    for k in range(_K):
        pos = jnp.sum(jnp.where(hots[k], row, 0.0), axis=0, keepdims=True)
        out = jnp.where(out_row == k, idxs[k], out)
        out = jnp.where(out_row == _K + k, ex[k] / den, out)
        out = jnp.where(out_row == 2 * _K + k, pos, out)
    route_t_ref[...] = out
    route_ref[...] = jnp.zeros_like(route_ref)
    route_ref[:, 0:_RT_ROWS] = out.T


def _route(logits_t):
    s = logits_t.shape[1]
    tt = min(_TT, s)
    nt = s // tt
    return pl.pallas_call(
        _route_kernel,
        grid=(nt,),
        in_specs=[pl.BlockSpec((_E, tt), lambda i: (0, i))],
        out_specs=[pl.BlockSpec((tt, _LANES), lambda i: (i, 0)),
                   pl.BlockSpec((_RT_ROWS, tt), lambda i: (0, i)),
                   pl.BlockSpec((1, 1, _E), lambda i: (i, 0, 0))],
        out_shape=[jax.ShapeDtypeStruct((s, _LANES), _f32),
                   jax.ShapeDtypeStruct((_RT_ROWS, s), _f32),
                   jax.ShapeDtypeStruct((nt, 1, _E), _f32)],
        compiler_params=pltpu.CompilerParams(dimension_semantics=("arbitrary",)),
        name="route_topk_rank",
    )(logits_t)


def _offsets_kernel(cnt_ref, lo_ref, goff_ref, nch_ref, tail_ref, tsum_ref):
    cnt = cnt_ref[...]
    ntp = cnt.shape[0]
    r8 = jnp.floor((cnt + (_ALIGN - 1.0)) * (1.0 / _ALIGN)) * _ALIGN
    j_i = lax.broadcasted_iota(jnp.int32, (_E, _E), 0)
    e_i = lax.broadcasted_iota(jnp.int32, (_E, _E), 1)
    before = jnp.where(j_i < e_i, 1.0, 0.0).astype(_bf16)

    def times_before(a):
        a0, a1, a2 = _split3(a)
        return _dot(a0, before) + _dot(a1, before) + _dot(a2, before)

    lo = times_before(r8)
    tot = jnp.sum(r8, axis=0, keepdims=True)
    cap = jnp.floor((tot + (_BLK - 1.0)) * (1.0 / _BLK)) * _BLK
    start = times_before(jnp.broadcast_to(cap, (_SUBLANES, _E)))[0:1, :]
    a_i = lax.broadcasted_iota(jnp.int32, (ntp, ntp), 0)
    b_i = lax.broadcasted_iota(jnp.int32, (ntp, ntp), 1)
    earlier = jnp.where(b_i < a_i, 1.0, 0.0).astype(_bf16)
    goff = start + _dot_exact_lhs(earlier, r8)
    lo_ref[...] = lo.astype(jnp.int32)
    goff_ref[...] = goff.astype(jnp.int32)
    nch_ref[...] = (r8 * (1.0 / _ALIGN)).astype(jnp.int32)
    trow = lax.broadcasted_iota(jnp.int32, (_SUBLANES, _E), 0)
    end = start + cap
    used_blocks = jnp.max(end, axis=-1, keepdims=True) * (1.0 / _BLK)
    tail = jnp.where(trow == 0, start + tot,
                     jnp.where(trow == 1, (cap - tot) * (1.0 / _ALIGN),
                               jnp.where(trow == 2, used_blocks,
                                         jnp.where(trow == 3, start * (1.0 / _BLK),
                                                   jnp.where(trow == 4, cap * (1.0 / _BLK), 0.0)))))
    tail_ref[...] = tail.astype(jnp.int32)
    tsum = jnp.sum(r8, axis=-1, keepdims=True)
    tsum_ref[...] = jnp.broadcast_to(tsum, (ntp, _E)).astype(jnp.int32)


def _offsets(cnt):
    ntp = cnt.shape[0]
    i32 = jnp.int32
    return pl.pallas_call(
        _offsets_kernel,
        out_shape=[jax.ShapeDtypeStruct((ntp, _E), i32),
                   jax.ShapeDtypeStruct((ntp, _E), i32), jax.ShapeDtypeStruct((ntp, _E), i32),
                   jax.ShapeDtypeStruct((_SUBLANES, _E), i32), jax.ShapeDtypeStruct((ntp, _E), i32)],
        name="route_offsets",
    )(cnt)


_HI_MASK = -65536


def _pack_pairs(x):
    half = x.shape[1] // 2
    bits = lax.bitcast_convert_type(x, jnp.int32)
    return (bits[:, :half] & _HI_MASK) | lax.shift_right_logical(bits[:, half:], 16)


def _unpack_pairs(p):
    hi = lax.bitcast_convert_type(p & _HI_MASK, _f32).astype(_bf16)
    lo = lax.bitcast_convert_type(lax.shift_left(p, 16), _f32).astype(_bf16)
    return hi, lo


def _for_each_part(lo_s, goff_s, nch_s, tile, fn):
    def per_expert(e, carry):
        rows = nch_s[tile * _E + e] * _ALIGN
        lo = lo_s[tile * _E + e]
        go = goff_s[tile * _E + e]

        def bulk(j, c):
            fn(lo + j * _BULK, go + j * _BULK, _BULK)
            return c

        lax.fori_loop(0, lax.shift_right_logical(rows, _BULK.bit_length() - 1), bulk, 0)
        size = _BULK // 2
        while size >= _ALIGN:
            @pl.when((rows & size) != 0)
            def _(size=size):
                off = rows & (-2 * size)
                fn(lo + off, go + off, size)
            size //= 2
        return carry
    lax.fori_loop(0, _E, per_expert, 0)


def _part_copy(src, src_row, dst, dst_row, size, sem):
    return pltpu.make_async_copy(src.at[pl.ds(pl.multiple_of(src_row, _ALIGN), size)],
                                 dst.at[pl.ds(pl.multiple_of(dst_row, _ALIGN), size)], sem)


def _wait_rows(total, limit, copy_of_size):
    size = _ALIGN
    while size * 2 <= limit:
        size *= 2
    while size >= _ALIGN:
        @pl.when((total & size) != 0)
        def _(size=size):
            copy_of_size(size).wait()
        size //= 2


def _zero_unused_blocks(zblk, dst_ref, first, sem, wait):
    nb = dst_ref.shape[0] // _BLK

    def one(b, carry):
        copy = pltpu.make_async_copy(
            zblk, dst_ref.at[pl.ds(pl.multiple_of(b * _BLK, _BLK), _BLK)], sem)
        if wait:
            copy.wait()
        else:
            copy.start()
        return carry

    lax.fori_loop(first, nb, one, 0)


def _dispatch_kernel(lo_s, goff_s, nch_s, tail_s, tsum_s, hn_ref, route_t_ref, xs_ref,
                     buf, zblk, sem, *, nt):
    i = pl.program_id(0)
    tt = hn_ref.shape[0]
    cap = buf.shape[1]
    rt = route_t_ref[...]
    pos16 = [rt[2 * _K + k:2 * _K + k + 1, :].astype(jnp.int32).astype(jnp.int16)
             for k in range(_K)]

    slot = lax.rem(i, 2)
    tile_buf = buf.at[slot]
    fill_sem = sem.at[2]

    def wait_tile(tile, sl):
        _wait_rows(tsum_s[tile], cap,
                   lambda size: _part_copy(buf.at[sl], 0, xs_ref, 0, size, sem.at[sl]))

    def zero_fill(wait):
        def per_expert(e, carry):
            def one(j, c):
                copy = _part_copy(zblk, 0, xs_ref, tail_s[e] + j * _ALIGN, _ALIGN, fill_sem)
                if wait:
                    copy.wait()
                else:
                    copy.start()
                return c
            return lax.fori_loop(0, tail_s[_E + e], one, carry)

        lax.fori_loop(0, _E, per_expert, 0)
        _zero_unused_blocks(zblk, xs_ref, tail_s[2 * _E], fill_sem, wait)

    @pl.when(i == 0)
    def _():
        zblk[...] = jnp.zeros_like(zblk)
        zero_fill(wait=False)

    @pl.when(i > 1)
    def _():
        wait_tile(i - 2, slot)

    rows = cap // _SORT_CHUNKS
    hn = hn_ref[...]
    for c in range(_SORT_CHUNKS):
        r_iota = (lax.broadcasted_iota(jnp.int32, (rows, tt), 0) + c * rows).astype(jnp.int16)
        perm = jnp.zeros((rows, tt), _bf16)
        for k in range(_K):
            perm = jnp.where(r_iota == pos16[k], jnp.ones((), _bf16), perm)
        buf[slot, c * rows:(c + 1) * rows, :] = _pack_pairs(_dot(perm, hn))

    def start_part(local_row, global_row, size):
        _part_copy(tile_buf, local_row, xs_ref, global_row, size, sem.at[slot]).start()

    _for_each_part(lo_s, goff_s, nch_s, i, start_part)

    @pl.when(i == nt - 1)
    def _():
        zero_fill(wait=True)
        if nt > 1:
            wait_tile(i - 1, 1 - slot)
        wait_tile(i, slot)


def _dispatch(lo, goff, nch, tail, tsum, hn, route_t, n_rows):
    s = hn.shape[0]
    tt = min(_TT, s)
    nt = s // tt
    cap = tt * _K + _E * _ALIGN
    grid_spec = pltpu.PrefetchScalarGridSpec(
        num_scalar_prefetch=5,
        grid=(nt,),
        in_specs=[
            pl.BlockSpec((tt, _D), lambda i, *_: (i, 0)),
            pl.BlockSpec((_RT_ROWS, tt), lambda i, *_: (0, i)),
        ],
        out_specs=pl.BlockSpec(memory_space=pl.ANY),
        scratch_shapes=[pltpu.VMEM((2, cap, _D // 2), jnp.int32),
                        pltpu.VMEM((_BLK, _D // 2), jnp.int32), pltpu.SemaphoreType.DMA((3,))],
    )
    return pl.pallas_call(
        functools.partial(_dispatch_kernel, nt=nt),
        grid_spec=grid_spec,
        out_shape=jax.ShapeDtypeStruct((n_rows, _D // 2), jnp.int32),
        compiler_params=pltpu.CompilerParams(dimension_semantics=("arbitrary",),
                                             vmem_limit_bytes=_VMEM_LIMIT),
        name="moe_dispatch",
    )(lo, goff, nch, tail, tsum, hn, route_t)


def _expert_kernel(tail_s, wgu_ref, bgu_ref, wd_ref, bd_ref, xs_ref, ys_ref,
                   wgu_c, wd_c, xin, yout, zblk, sem_in, sem_out):
    e = pl.program_id(0)
    used = tail_s[2 * _E]
    first = tail_s[3 * _E + e]
    nblk = tail_s[4 * _E + e]
    half = _D // 2

    def rows(g):
        return pl.ds(pl.multiple_of(g * _BLK, _BLK), _BLK)

    def in_copy(g, slot):
        return pltpu.make_async_copy(xs_ref.at[rows(g)], xin.at[slot], sem_in.at[slot])

    def out_copy(g, slot):
        return pltpu.make_async_copy(yout.at[slot], ys_ref.at[rows(g)], sem_out.at[slot])

    def slot_of(g):
        return lax.rem(g, _RING)

    @pl.when(e == 0)
    def _():
        for g in range(_AHEAD):
            @pl.when(g < used)
            def _(g=g):
                in_copy(g, g).start()
        zblk[...] = jnp.zeros_like(zblk)
        _zero_unused_blocks(zblk, ys_ref, used, sem_out.at[_RING], False)

    @pl.when(nblk > 0)
    def _():
        step = _LANES
        for r in range(0, _D, step):
            wgu_c[r:r + step, :] = wgu_ref[0, 0, r:r + step, :].astype(_bf16)
        for r in range(0, _F, step):
            wd_c[r:r + step, :] = wd_ref[0, 0, r:r + step, :].astype(_bf16)

    def unit(g, n):
        for d in range(n):
            in_copy(g + d, slot_of(g + d)).wait()
        for d in range(_AHEAD, _AHEAD + n):
            @pl.when(g + d < used)
            def _(d=d):
                in_copy(g + d, slot_of(g + d)).start()
        parts = [_unpack_pairs(xin[slot_of(g + d)]) for d in range(n)]
        x_hi = jnp.concatenate([p[0] for p in parts], axis=0)
        x_lo = jnp.concatenate([p[1] for p in parts], axis=0)
        gu = _dot(jnp.concatenate([x_hi, x_lo], axis=1), wgu_c[...]) + bgu_ref[0, 0]
        x_glu = jnp.minimum(gu[:, :_F], _SWIGLU_LIMIT)
        x_lin = jnp.clip(gu[:, _F:], -_SWIGLU_LIMIT, _SWIGLU_LIMIT)
        act = x_glu * jax.nn.sigmoid(_SWIGLU_ALPHA * x_glu) * (x_lin + 1.0)
        y = _dot(act.astype(_bf16), wd_c[...]) + bd_ref[0, 0]
        packed = _pack_pairs(y.astype(_bf16).astype(_f32))
        for d in range(n):
            @pl.when(g + d >= _RING)
            def _(d=d):
                out_copy(g + d - _RING, slot_of(g + d)).wait()
            yout[slot_of(g + d)] = packed[d * _BLK:(d + 1) * _BLK, :]
            out_copy(g + d, slot_of(g + d)).start()

    def widest(j, carry):
        unit(first + _AHEAD * j, _AHEAD)
        return carry

    lax.fori_loop(0, nblk // _AHEAD, widest, 0)
    n = _AHEAD // 2
    while n >= 1:
        @pl.when((nblk & n) != 0)
        def _(n=n):
            unit(first + (nblk & (-2 * n)), n)
        n //= 2

    @pl.when(e == _E - 1)
    def _():
        for d in range(1, _RING + 1):
            @pl.when(used >= d)
            def _(d=d):
                out_copy(used - d, slot_of(used - d)).wait()

        _zero_unused_blocks(zblk, ys_ref, used, sem_out.at[_RING], True)


def _experts(tail, xs, w_gu, b_gu, w_down, b_down, layer):
    n_rows = xs.shape[0]
    wsel = lambda e, *_: (layer, e, 0, 0)
    grid_spec = pltpu.PrefetchScalarGridSpec(
        num_scalar_prefetch=1,
        grid=(_E,),
        in_specs=[
            pl.BlockSpec((1, 1, _D, 2 * _F), wsel),
            pl.BlockSpec((1, 1, 1, 2 * _F), wsel),
            pl.BlockSpec((1, 1, _F, _D), wsel),
            pl.BlockSpec((1, 1, 1, _D), wsel),
            pl.BlockSpec(memory_space=pl.ANY),
        ],
        out_specs=pl.BlockSpec(memory_space=pl.ANY),
        scratch_shapes=[pltpu.VMEM((_D, 2 * _F), _bf16), pltpu.VMEM((_F, _D), _bf16),
                        pltpu.VMEM((_RING, _BLK, _D // 2), jnp.int32),
                        pltpu.VMEM((_RING, _BLK, _D // 2), jnp.int32),
                        pltpu.VMEM((_BLK, _D // 2), jnp.int32),
                        pltpu.SemaphoreType.DMA((_RING,)), pltpu.SemaphoreType.DMA((_RING + 1,))],
    )
    depth = w_gu.shape[0]
    return pl.pallas_call(
        _expert_kernel,
        grid_spec=grid_spec,
        out_shape=jax.ShapeDtypeStruct((n_rows, _D // 2), jnp.int32),
        compiler_params=pltpu.CompilerParams(dimension_semantics=("arbitrary",),
                                             vmem_limit_bytes=_VMEM_LIMIT),
        name="moe_experts",
    )(tail, w_gu, b_gu.reshape(depth, _E, 1, 2 * _F), w_down, b_down.reshape(depth, _E, 1, _D), xs)


def _combine_kernel(lo_s, goff_s, nch_s, tsum_s, route_ref, x1_ref, mod_ref, fin_ref, ys_ref,
                    out_ref, ybuf, sem, *, final, nt):
    i = pl.program_id(0)
    tt = x1_ref.shape[0]
    cap = ybuf.shape[1]
    slot = lax.rem(i, 2)

    def fetch(tile, dst):
        def start_part(local_row, global_row, size):
            _part_copy(ys_ref, global_row, dst, local_row, size, sem).start()
        _for_each_part(lo_s, goff_s, nch_s, tile, start_part)

    @pl.when(i == 0)
    def _():
        ybuf[...] = jnp.zeros_like(ybuf)
        fetch(0, ybuf.at[0])

    _wait_rows(tsum_s[i], cap, lambda size: _part_copy(ys_ref, 0, ybuf.at[slot], 0, size, sem))

    @pl.when(i + 1 < nt)
    def _():
        fetch(i + 1, ybuf.at[1 - slot])

    route = route_ref[...]
    pos16 = [route[:, 2 * _K + k:2 * _K + k + 1].astype(jnp.int32).astype(jnp.int16)
             for k in range(_K)]
    gate16 = [route[:, _K + k:_K + k + 1].astype(_bf16) for k in range(_K)]

    cols = cap // _SORT_CHUNKS
    moe_hi = jnp.zeros((tt, _D // 2), _f32)
    moe_lo = jnp.zeros((tt, _D // 2), _f32)
    for c in range(_SORT_CHUNKS):
        lane_r = (lax.broadcasted_iota(jnp.int32, (tt, cols), 1) + c * cols).astype(jnp.int16)
        wmat = jnp.zeros((tt, cols), _bf16)
        for k in range(_K):
            wmat = jnp.where(lane_r == pos16[k], gate16[k], wmat)
        y_hi, y_lo = _unpack_pairs(ybuf[slot, c * cols:(c + 1) * cols, :])
        moe_hi = moe_hi + _dot(wmat, y_hi)
        moe_lo = moe_lo + _dot(wmat, y_lo)
    moe = jnp.concatenate([moe_hi, moe_lo], axis=1)
    x2 = x1_ref[...] + mod_ref[5:6, :] * moe
    if final:
        x2 = _rms(x2) * fin_ref[...]
    out_ref[...] = x2


def _combine(lo, goff, nch, tsum, route, x1, mod, final_norm, ys, final):
    s = x1.shape[0]
    tt = min(_TT, s)
    nt = s // tt
    cap = tt * _K + _E * _ALIGN
    grid_spec = pltpu.PrefetchScalarGridSpec(
        num_scalar_prefetch=4,
        grid=(nt,),
        in_specs=[
            pl.BlockSpec((tt, _LANES), lambda i, *_: (i, 0)),
            pl.BlockSpec((tt, _D), lambda i, *_: (i, 0)),
            pl.BlockSpec((6, _D), lambda i, *_: (0, 0)),
            pl.BlockSpec((1, _D), lambda i, *_: (0, 0)),
            pl.BlockSpec(memory_space=pl.ANY),
        ],
        out_specs=pl.BlockSpec((tt, _D), lambda i, *_: (i, 0)),
        scratch_shapes=[pltpu.VMEM((2, cap, _D // 2), jnp.int32), pltpu.SemaphoreType.DMA(())],
    )
    return pl.pallas_call(
        functools.partial(_combine_kernel, final=final, nt=nt),
        grid_spec=grid_spec,
        out_shape=jax.ShapeDtypeStruct((s, _D), _f32),
        compiler_params=pltpu.CompilerParams(dimension_semantics=("arbitrary",),
                                             vmem_limit_bytes=_VMEM_LIMIT),
        name="moe_combine",
    )(lo, goff, nch, tsum, route, x1, mod, final_norm.reshape(1, _D), ys)


def _moe(x1, hn2, logits, mod, final_norm, w_gu, b_gu, w_down, b_down, layer, final):
    s = x1.shape[0]
    tt = min(_TT, s)
    nt = s // tt
    ntp = -(-nt // _SUBLANES) * _SUBLANES
    n_rows = -(-(s * _K + (_ALIGN - 1) * nt * _E) // _BLK) * _BLK + _E * _BLK
    route, route_t, cnt = _route(logits)
    cnt = jnp.pad(cnt.reshape(nt, _E), ((0, ntp - nt), (0, 0)))
    lo, goff, nch, tail, tsum = _offsets(cnt)
    lo_s, goff_s, nch_s = (a[:nt].reshape(nt * _E) for a in (lo, goff, nch))
    tail_s = tail[:5].reshape(5 * _E)
    tsum_s = tsum[:nt, 0]
    xs = _dispatch(lo_s, goff_s, nch_s, tail_s, tsum_s, hn2, route_t, n_rows)
    ys = _experts(tail_s, xs, w_gu, b_gu, w_down, b_down, layer)
    return _combine(lo_s, goff_s, nch_s, tsum_s, route, x1, mod, final_norm, ys, final)


def _decay_tables(tb):
    t = np.arange(tb)
    tri = ((t[:, None] // _CH) == (t[None, :] // _CH)) & (t[None, :] <= t[:, None])
    rel = (t[:, None] - t[None, :]).astype(np.float64)
    lg = np.log(1.0 - 2.0 ** (-5.0 - np.arange(_HEADS, dtype=np.float64)))
    dmat = np.where(rel >= 0, np.exp(np.maximum(rel, 0.0)[None] * lg[:, None, None]), 0.0)
    return jnp.asarray(tri.astype(np.float32), _bf16), jnp.asarray(dmat.astype(np.float32))


def _rotary_tables(s, tb):
    half = _DH // 2
    inv = (np.float32(10000.0) ** (-np.arange(half, dtype=np.float32) / np.float32(half)))
    inv = inv.astype(np.float64)

    def cos_sin(positions):
        ang = positions.astype(np.float64)[:, None] * inv[None, :]
        cos = np.concatenate([np.cos(ang)] * 2, axis=1)
        sin = np.concatenate([np.sin(ang)] * 2, axis=1)
        return cos.astype(np.float32), sin.astype(np.float32)

    cos_a, sin_a = cos_sin(np.arange(s // tb) * tb)
    cos_b, sin_b = cos_sin(np.arange(tb))
    return (jnp.asarray(np.stack([cos_a, sin_a], axis=1)), jnp.asarray(np.stack([cos_b, sin_b], axis=0)))


def _lower_bounds_kernel(p_ref, o_ref):
    p = p_ref[...]
    n = p.shape[0]
    m = p[0:1, :]
    for j in range(1, n):
        m = jnp.maximum(m, p[j:j + 1, :])
    ex = [jnp.exp(p[j:j + 1, :] - m) for j in range(n)]
    den = ex[0]
    for j in range(1, n):
        den = den + ex[j]
    acc = jnp.zeros_like(den)
    rows = []
    for j in range(n):
        acc = acc + ex[j] / den
        rows.append(acc)
    o_ref[...] = jnp.concatenate(rows, axis=0)


def _lower_bounds(p):
    return pl.pallas_call(
        _lower_bounds_kernel,
        out_shape=jax.ShapeDtypeStruct(p.shape, _f32),
        name="hgrn2_lower_bounds",
    )(p)


def kernel(x, c, w_ada, b_ada, w_in, hg_lower_bounds, hg_norm, ret_norm, w_out, pool_w, pool_b,
           pool_scale, w_router, b_router, w_gu, b_gu, w_down, b_down, final_norm):
    b, s, d = x.shape
    assert b == 1 and d == _D
    depth = w_ada.shape[0]
    assert depth == 2
    xs = x.reshape(s, d)
    mod = _ada(c, w_ada, b_ada)
    lb_all = _lower_bounds(hg_lower_bounds)
    x1, hn2, logits = _mixer0(xs, mod[0], w_in[0], lb_all[0], hg_norm[0], ret_norm[0], w_out[0],
                              w_router[0], b_router[0])
    xs = _moe(x1, hn2, logits, mod[0], final_norm, w_gu, b_gu, w_down, b_down, 0, False)
    x1, hn2, logits = _pool_mixer(xs, mod[1], pool_w[0], pool_b[0], pool_scale[0], w_router[1],
                                  b_router[1])
    xs = _moe(x1, hn2, logits, mod[1], final_norm, w_gu, b_gu, w_down, b_down, 1, True)
    return xs.reshape(b, s, d)
```

```python
import functools
import math

import jax
import jax.numpy as jnp
import numpy as np
from jax import lax
from jax.experimental import pallas as pl
from jax.experimental.pallas import tpu as pltpu

_LANES = 128
_SUBLANES = 8
_D = 1024
_E = 32
_K = 4
_F = 1024
_HEADS = 4
_DH = 128
_CH = 64
_SUB = 16
_EPS = 1e-6
_BLK = 256
_TB = 256
_TBP = 512
_TT = 512
_ALIGN = _SUBLANES
_RT_ROWS = 16
_BULK = 64
_SORT_CHUNKS = 3
_AHEAD = 4
_RING = 2 * _AHEAD
_NEG = -1e30
_LOG2E = 1.4426950408889634
_SWIGLU_LIMIT = 7.0
_SWIGLU_ALPHA = 1.702
_POOL_WINDOWS = (2, 4, 8, 16)
_HALO = 16
_VMEM_BYTES = 64 * 1024 * 1024
_VMEM_LIMIT = _VMEM_BYTES // 8 * 7

_f32 = jnp.float32
_bf16 = jnp.bfloat16


def _dot(a, b):
    return jnp.dot(a, b, preferred_element_type=_f32)


def _dot_nt(a, b):
    return lax.dot_general(a, b, (((1,), (1,)), ((), ())), preferred_element_type=_f32)


def _split3(a):
    a0 = a.astype(_bf16)
    r1 = a - a0.astype(_f32)
    a1 = r1.astype(_bf16)
    a2 = (r1 - a1.astype(_f32)).astype(_bf16)
    return a0, a1, a2


def _dot_exact_lhs(m, a):
    a0, a1, a2 = _split3(a)
    return _dot(m, a0) + _dot(m, a1) + _dot(m, a2)


def _rms(x):
    return x * lax.rsqrt(jnp.mean(x * x, axis=-1, keepdims=True) + _EPS)


def _silu(x):
    return x * jax.nn.sigmoid(x)


def _ada_kernel(c_ref, w_ref, b_ref, o_ref):
    c = c_ref[...]
    cond = _silu(c)
    o_ref[0] = jnp.sum(w_ref[0] * cond, axis=0, keepdims=True) + b_ref[0]


def _ada(c, w_ada, b_ada):
    depth = w_ada.shape[0]
    n = depth * 6
    out = pl.pallas_call(
        _ada_kernel,
        grid=(n,),
        in_specs=[
            pl.BlockSpec((_D, 1), lambda j: (0, 0)),
            pl.BlockSpec((1, _D, _D), lambda j: (j // 6, 0, j % 6)),
            pl.BlockSpec((1, 1, _D), lambda j: (j, 0, 0)),
        ],
        out_specs=pl.BlockSpec((1, 1, _D), lambda j: (j, 0, 0)),
        out_shape=jax.ShapeDtypeStruct((n, 1, _D), _f32),
        name="ada_mod",
    )(c.reshape(_D, 1), w_ada, b_ada.reshape(n, 1, _D))
    return out.reshape(depth, 6, _D)


def _ffn_prologue(x1, mod, wr_ref, br_ref, x1_ref, hn2_ref, lg_ref):
    sh_f, sc_f = mod[3:4, :], mod[4:5, :]
    hn2 = _rms(x1) * (1.0 + sc_f) + sh_f
    x1_ref[...] = x1
    a0, a1, _ = _split3(hn2)
    hn2_ref[...] = a0
    w0, w1, _ = _split3(wr_ref[...])
    both = _dot(a0, jnp.concatenate([w0, w1], axis=1))
    lg_ref[...] = (both[:, :_E] + both[:, _E:] + _dot(a1, w0) + br_ref[...]).T


def _mixer0_kernel(x_ref, mod_ref, win_ref, lb_ref, hgg_ref, rtg_ref, wout_ref, rot_a_ref, rot_b_ref,
                   tri_ref, dmat_ref, wr_ref, br_ref, x1_ref, hn2_ref, lg_ref,
                   q_s, k_s, v_s, b_s, o_s, st_hg, st_rt):
    tb = x_ref.shape[0]
    nch = tb // _CH
    nsub = _CH // _SUB

    @pl.when(pl.program_id(0) == 0)
    def _():
        st_hg[...] = jnp.zeros_like(st_hg)
        st_rt[...] = jnp.zeros_like(st_rt)

    x = x_ref[...]
    mod = mod_ref[...]
    sh_m, sc_m, g_m = mod[0:1, :], mod[1:2, :], mod[2:3, :]
    hb = (_rms(x) * (1.0 + sc_m) + sh_m).astype(_bf16)
    width = _HEADS * _DH

    def sec(j):
        return _dot(hb, win_ref[:, j * width:(j + 1) * width])

    hq = sec(0)
    lb = lb_ref[...]
    fg = lb + (1.0 - lb) * jax.nn.sigmoid(sec(1))
    hi = sec(2)
    g = jnp.log(fg) * _LOG2E
    bcum = _dot_exact_lhs(tri_ref[...], g)
    qh = _silu(hq)
    kk = 1.0 - fg
    for h in range(_HEADS):
        sl = slice(h * _DH, (h + 1) * _DH)
        q_s[h] = qh[:, sl]
        k_s[h] = kk[:, sl]
        v_s[h] = hi[:, sl]
        b_s[h] = bcum[:, sl]

    hgg = hgg_ref[...]
    row64 = lax.broadcasted_iota(jnp.int32, (_HEADS, _CH, _DH), 1)
    row16 = lax.broadcasted_iota(jnp.int32, (_HEADS * nsub, _SUB, 1), 1)
    lead16 = lax.broadcasted_iota(jnp.int32, (_HEADS * nsub, _SUB, _CH), 0)
    col16 = lax.broadcasted_iota(jnp.int32, (_HEADS * nsub, _SUB, _CH), 2) - _SUB * (lead16 & (nsub - 1))

    def chunk(c, carry):
        rows = pl.ds(pl.multiple_of(c * _CH, _CH), _CH)
        q = q_s[:, rows, :]
        k = k_s[:, rows, :]
        v = v_s[:, rows, :]
        b = b_s[:, rows, :]
        st = st_hg[...]
        vb = v.astype(_bf16)
        qe = (q * jnp.exp2(b)).astype(_bf16)
        o = jnp.einsum('htd,hed->hte', qe, st.astype(_bf16), preferred_element_type=_f32)
        blocks = [jnp.zeros((_HEADS, _SUB, _CH), _f32)]
        for i in range(1, nsub):
            ref = b[:, _SUB * i - 1:_SUB * i, :]
            qp = q[:, _SUB * i:_SUB * (i + 1), :] * jnp.exp2(b[:, _SUB * i:_SUB * (i + 1), :] - ref)
            kp = k * jnp.exp2(jnp.where(row64 < _SUB * i, ref - b, _NEG))
            blocks.append(jnp.einsum('htd,hsd->hts', qp.astype(_bf16), kp.astype(_bf16),
                                     preferred_element_type=_f32))
        a = jnp.concatenate(blocks, axis=1)
        q3 = q.reshape(_HEADS * nsub, _SUB, _DH)
        k3 = k.reshape(_HEADS * nsub, _SUB, _DH)
        b3 = b.reshape(_HEADS * nsub, _SUB, _DH)
        ad = jnp.zeros((_HEADS * nsub, _SUB, _CH), _f32)
        for s in range(_SUB):
            e = jnp.exp2(b3 - b3[:, s:s + 1, :])
            w = jnp.sum(q3 * e * k3[:, s:s + 1, :], axis=-1, keepdims=True)
            ad = jnp.where((col16 == s) & (row16 >= s), w, ad)
        a = a + ad.reshape(_HEADS, _CH, _CH)
        o = o + jnp.einsum('hts,hse->hte', a.astype(_bf16), vb, preferred_element_type=_f32)
        bl = b[:, _CH - 1:_CH, :]
        kd = (k * jnp.exp2(bl - b)).astype(_bf16)
        dec = jnp.exp2(bl)
        for h in range(_HEADS):
            st_hg[h] = st[h] * dec[h] + _dot(v[h].T.astype(_bf16), kd[h])
        o = o * lax.rsqrt(jnp.mean(o * o, axis=-1, keepdims=True) + _EPS) * hgg
        o_s[:, rows, :] = o
        return carry

    lax.fori_loop(0, nch, chunk, 0, unroll=True)
    hog = sec(3)
    cat_hg = jnp.concatenate([o_s[h] for h in range(_HEADS)], axis=1) * _silu(hog)

    rq = sec(4)
    rk = sec(5)
    rv = sec(6)
    cos_a, sin_a = rot_a_ref[0, 0:1, :], rot_a_ref[0, 1:2, :]
    cos_b, sin_b = rot_b_ref[0], rot_b_ref[1]
    first_half = lax.broadcasted_iota(jnp.int32, (1, _DH), 1) < _DH // 2
    cosf = cos_a * cos_b - sin_a * sin_b
    sinf = (sin_a * cos_b + cos_a * sin_b) * jnp.where(first_half, -1.0, 1.0)
    tcol = lax.broadcasted_iota(jnp.int32, (tb, 1), 0).astype(_f32)
    rtg = rtg_ref[...]
    outs = []
    for h in range(_HEADS):
        sl = slice(h * _DH, (h + 1) * _DH)
        lg = math.log(1.0 - 2.0 ** (-5.0 - h))
        xq = rq[:, sl]
        xk = rk[:, sl]
        q = xq * cosf + pltpu.roll(xq, _DH // 2, 1) * sinf
        k = (xk * cosf + pltpu.roll(xk, _DH // 2, 1) * sinf) * (_DH ** -0.5)
        v = rv[:, sl]
        scores = _dot_nt(q.astype(_bf16), k.astype(_bf16)) * dmat_ref[h]
        o = _dot(scores.astype(_bf16), v.astype(_bf16))
        st = st_rt[h]
        qd = q * jnp.exp((tcol + 1.0) * lg)
        o = o + _dot_nt(qd.astype(_bf16), st.astype(_bf16))
        kd = k * jnp.exp((tb - 1.0 - tcol) * lg)
        st_rt[h] = st * math.exp(tb * lg) + _dot(v.T.astype(_bf16), kd.astype(_bf16))
        o = o * lax.rsqrt(jnp.mean(o * o, axis=-1, keepdims=True) + _EPS) * rtg[h]
        outs.append(o)
    rg = sec(7)
    cat_rt = jnp.concatenate(outs, axis=1) * _silu(rg)

    cat = jnp.concatenate([cat_hg, cat_rt], axis=1).astype(_bf16)
    mix = _dot(cat, wout_ref[...])
    x1 = x + g_m * mix
    _ffn_prologue(x1, mod, wr_ref, br_ref, x1_ref, hn2_ref, lg_ref)


def _mixer0(x, mod, w_in, lb, hg_gain, ret_gain, w_out, w_router, b_router):
    s = x.shape[0]
    tb = min(_TB, s)
    rot_a, rot_b = _rotary_tables(s, tb)
    tri, dmat = _decay_tables(tb)
    width = _HEADS * _DH
    full = lambda shape: pl.BlockSpec(shape, lambda i: (0,) * len(shape))
    rowblk = lambda w: pl.BlockSpec((tb, w), lambda i: (i, 0))
    return pl.pallas_call(
        _mixer0_kernel,
        grid=(s // tb,),
        in_specs=[
            rowblk(_D), full((6, _D)), full((_D, 8 * width)), full((1, width)),
            full((_HEADS, 1, _DH)), full((_HEADS, 1, _DH)), full((2 * width, _D)),
            pl.BlockSpec((1, 2, _DH), lambda i: (i, 0, 0)), full((2, tb, _DH)),
            full((tb, tb)), full((_HEADS, tb, tb)), full((_D, _E)), full((1, _E)),
        ],
        out_specs=[rowblk(_D), rowblk(_D), pl.BlockSpec((_E, tb), lambda i: (0, i))],
        out_shape=[jax.ShapeDtypeStruct((s, _D), _f32), jax.ShapeDtypeStruct((s, _D), _bf16),
                   jax.ShapeDtypeStruct((_E, s), _f32)],
        scratch_shapes=[pltpu.VMEM((_HEADS, tb, _DH), _f32)] * 5
        + [pltpu.VMEM((_HEADS, _DH, _DH), _f32)] * 2,
        compiler_params=pltpu.CompilerParams(dimension_semantics=("arbitrary",),
                                             vmem_limit_bytes=_VMEM_LIMIT),
        name="mixer_hgrn2_retention",
    )(x, mod, w_in.astype(_bf16), lb.reshape(1, width), hg_gain.reshape(_HEADS, 1, _DH),
      ret_gain.reshape(_HEADS, 1, _DH), w_out.astype(_bf16), rot_a, rot_b,
      tri, dmat, w_router,
      b_router.reshape(1, _E))


def _pool_kernel(x_ref, mod_ref, pw_ref, pb_ref, ps_ref, wr_ref, br_ref,
                 x1_ref, hn2_ref, lg_ref, halo):
    tb = x_ref.shape[0]
    i = pl.program_id(0)

    @pl.when(i == 0)
    def _():
        halo[...] = jnp.zeros_like(halo)

    x = x_ref[...]
    mod = mod_ref[...]
    sh_m, sc_m, g_m = mod[0:1, :], mod[1:2, :], mod[2:3, :]
    hn = _rms(x) * (1.0 + sc_m) + sh_m
    ext = jnp.concatenate([halo[...], hn], axis=0)
    halo[...] = hn[tb - _HALO:, :]
    t = (i * tb + lax.broadcasted_iota(jnp.int32, (tb, 1), 0)).astype(_f32)
    gw = _D // len(_POOL_WINDOWS)
    ys = []
    for gi, w in enumerate(_POOL_WINDOWS):
        acc = ext[:, gi * gw:(gi + 1) * gw]
        shift = 1
        while shift < w:
            acc = acc + pltpu.roll(acc, shift, 0)
            shift *= 2
        win = acc[_HALO:, :]
        inv_cnt = 1.0 / jnp.minimum(t + 1.0, float(w))
        p = win * inv_cnt - hn[:, gi * gw:(gi + 1) * gw]
        ys.append(_dot(p.astype(_bf16), pw_ref[gi]) + pb_ref[gi])
    mix = jnp.concatenate(ys, axis=1) * ps_ref[...]
    x1 = x + g_m * mix
    _ffn_prologue(x1, mod, wr_ref, br_ref, x1_ref, hn2_ref, lg_ref)


def _pool_mixer(x, mod, pool_w, pool_b, pool_scale, w_router, b_router):
    s = x.shape[0]
    tb = min(_TBP, s)
    ng = len(_POOL_WINDOWS)
    gw = _D // ng
    full = lambda shape: pl.BlockSpec(shape, lambda i: (0,) * len(shape))
    rowblk = lambda w: pl.BlockSpec((tb, w), lambda i: (i, 0))
    return pl.pallas_call(
        _pool_kernel,
        grid=(s // tb,),
        in_specs=[rowblk(_D), full((6, _D)), full((ng, gw, gw)), full((ng, 1, gw)), full((1, _D)),
                  full((_D, _E)), full((1, _E))],
        out_specs=[rowblk(_D), rowblk(_D), pl.BlockSpec((_E, tb), lambda i: (0, i))],
        out_shape=[jax.ShapeDtypeStruct((s, _D), _f32), jax.ShapeDtypeStruct((s, _D), _bf16),
                   jax.ShapeDtypeStruct((_E, s), _f32)],
        scratch_shapes=[pltpu.VMEM((_HALO, _D), _f32)],
        compiler_params=pltpu.CompilerParams(dimension_semantics=("arbitrary",),
                                             vmem_limit_bytes=_VMEM_LIMIT),
        name="mixer_pool",
    )(x, mod, pool_w.astype(_bf16), pool_b.reshape(ng, 1, gw), pool_scale.reshape(1, _D),
      w_router, b_router.reshape(1, _E))


def _route_kernel(lg_ref, route_ref, route_t_ref, cnt_ref):
    tt = lg_ref.shape[1]
    vals = lg_ref[...]
    sub = lax.broadcasted_iota(jnp.int32, (_E, tt), 0).astype(_f32)
    tops, idxs, hots = [], [], []
    for _ in range(_K):
        m = jnp.max(vals, axis=0, keepdims=True)
        idx = jnp.min(jnp.where(vals == m, sub, float(_E)), axis=0, keepdims=True)
        hot = sub == idx
        vals = jnp.where(hot, -jnp.inf, vals)
        tops.append(m)
        idxs.append(idx)
        hots.append(hot)
    ex = [jnp.exp(m - tops[0]) for m in tops]
    den = ex[0] + ex[1] + ex[2] + ex[3]
    member = jnp.zeros((_E, tt), _f32)
    for hot in hots:
        member = member + jnp.where(hot, 1.0, 0.0)
    member_b = member.astype(_bf16)
    s_i = lax.broadcasted_iota(jnp.int32, (tt, tt), 0)
    t_i = lax.broadcasted_iota(jnp.int32, (tt, tt), 1)
    earlier = jnp.where(s_i < t_i, 1.0, 0.0).astype(_bf16)
    cum = _dot(member_b, earlier)
    cnt_ref[0] = _dot_nt(jnp.ones((_SUBLANES, tt), _bf16), member_b)[0:1, :]
    cnt = jnp.sum(member, axis=1, keepdims=True)
    r8 = jnp.floor((cnt + (_ALIGN - 1.0)) * (1.0 / _ALIGN)) * _ALIGN
    e_i = lax.broadcasted_iota(jnp.int32, (_E, _E), 0)
    j_i = lax.broadcasted_iota(jnp.int32, (_E, _E), 1)
    before = jnp.where(j_i < e_i, 1.0, 0.0).astype(_bf16)
    lo = _dot(before, jnp.broadcast_to(r8, (_E, _LANES)).astype(_bf16))[:, 0:1]
    row = cum + lo
    out_row = lax.broadcasted_iota(jnp.int32, (_RT_ROWS, tt), 0)
    out = jnp.zeros((_RT_ROWS, tt), _f32)
    for k in range(_K):
        pos = jnp.sum(jnp.where(hots[k], row, 0.0), axis=0, keepdims=True)
        out = jnp.where(out_row == k, idxs[k], out)
        out = jnp.where(out_row == _K + k, ex[k] / den, out)
        out = jnp.where(out_row == 2 * _K + k, pos, out)
    route_t_ref[...] = out
    route_ref[...] = jnp.zeros_like(route_ref)
    route_ref[:, 0:_RT_ROWS] = out.T


def _route(logits_t):
    s = logits_t.shape[1]
    tt = min(_TT, s)
    nt = s // tt
    return pl.pallas_call(
        _route_kernel,
        grid=(nt,),
        in_specs=[pl.BlockSpec((_E, tt), lambda i: (0, i))],
        out_specs=[pl.BlockSpec((tt, _LANES), lambda i: (i, 0)),
                   pl.BlockSpec((_RT_ROWS, tt), lambda i: (0, i)),
                   pl.BlockSpec((1, 1, _E), lambda i: (i, 0, 0))],
        out_shape=[jax.ShapeDtypeStruct((s, _LANES), _f32),
                   jax.ShapeDtypeStruct((_RT_ROWS, s), _f32),
                   jax.ShapeDtypeStruct((nt, 1, _E), _f32)],
        compiler_params=pltpu.CompilerParams(dimension_semantics=("arbitrary",)),
        name="route_topk_rank",
    )(logits_t)


def _offsets_kernel(cnt_ref, lo_ref, goff_ref, nch_ref, tail_ref, tsum_ref):
    cnt = cnt_ref[...]
    ntp = cnt.shape[0]
    r8 = jnp.floor((cnt + (_ALIGN - 1.0)) * (1.0 / _ALIGN)) * _ALIGN
    j_i = lax.broadcasted_iota(jnp.int32, (_E, _E), 0)
    e_i = lax.broadcasted_iota(jnp.int32, (_E, _E), 1)
    before = jnp.where(j_i < e_i, 1.0, 0.0).astype(_bf16)

    def times_before(a):
        a0, a1, a2 = _split3(a)
        return _dot(a0, before) + _dot(a1, before) + _dot(a2, before)

    lo = times_before(r8)
    tot = jnp.sum(r8, axis=0, keepdims=True)
    cap = jnp.floor((tot + (_BLK - 1.0)) * (1.0 / _BLK)) * _BLK
    start = times_before(jnp.broadcast_to(cap, (_SUBLANES, _E)))[0:1, :]
    a_i = lax.broadcasted_iota(jnp.int32, (ntp, ntp), 0)
    b_i = lax.broadcasted_iota(jnp.int32, (ntp, ntp), 1)
    earlier = jnp.where(b_i < a_i, 1.0, 0.0).astype(_bf16)
    goff = start + _dot_exact_lhs(earlier, r8)
    lo_ref[...] = lo.astype(jnp.int32)
    goff_ref[...] = goff.astype(jnp.int32)
    nch_ref[...] = (r8 * (1.0 / _ALIGN)).astype(jnp.int32)
    trow = lax.broadcasted_iota(jnp.int32, (_SUBLANES, _E), 0)
    end = start + cap
    used_blocks = jnp.max(end, axis=-1, keepdims=True) * (1.0 / _BLK)
    tail = jnp.where(trow == 0, start + tot,
                     jnp.where(trow == 1, (cap - tot) * (1.0 / _ALIGN),
                               jnp.where(trow == 2, used_blocks,
                                         jnp.where(trow == 3, start * (1.0 / _BLK),
                                                   jnp.where(trow == 4, cap * (1.0 / _BLK), 0.0)))))
    tail_ref[...] = tail.astype(jnp.int32)
    tsum = jnp.sum(r8, axis=-1, keepdims=True)
    tsum_ref[...] = jnp.broadcast_to(tsum, (ntp, _E)).astype(jnp.int32)


def _offsets(cnt):
    ntp = cnt.shape[0]
    i32 = jnp.int32
    return pl.pallas_call(
        _offsets_kernel,
        out_shape=[jax.ShapeDtypeStruct((ntp, _E), i32),
                   jax.ShapeDtypeStruct((ntp, _E), i32), jax.ShapeDtypeStruct((ntp, _E), i32),
                   jax.ShapeDtypeStruct((_SUBLANES, _E), i32), jax.ShapeDtypeStruct((ntp, _E), i32)],
        name="route_offsets",
    )(cnt)


_HI_MASK = -65536


def _pack_pairs(x):
    half = x.shape[1] // 2
    bits = lax.bitcast_convert_type(x, jnp.int32)
    return (bits[:, :half] & _HI_MASK) | lax.shift_right_logical(bits[:, half:], 16)


def _unpack_pairs(p):
    hi = lax.bitcast_convert_type(p & _HI_MASK, _f32).astype(_bf16)
    lo = lax.bitcast_convert_type(lax.shift_left(p, 16), _f32).astype(_bf16)
    return hi, lo


def _for_each_part(lo_s, goff_s, nch_s, tile, fn):
    def per_expert(e, carry):
        rows = nch_s[tile * _E + e] * _ALIGN
        lo = lo_s[tile * _E + e]
        go = goff_s[tile * _E + e]

        def bulk(j, c):
            fn(lo + j * _BULK, go + j * _BULK, _BULK)
            return c

        lax.fori_loop(0, lax.shift_right_logical(rows, _BULK.bit_length() - 1), bulk, 0)
        size = _BULK // 2
        while size >= _ALIGN:
            @pl.when((rows & size) != 0)
            def _(size=size):
                off = rows & (-2 * size)
                fn(lo + off, go + off, size)
            size //= 2
        return carry
    lax.fori_loop(0, _E, per_expert, 0, unroll=4)


def _part_copy(src, src_row, dst, dst_row, size, sem):
    return pltpu.make_async_copy(src.at[pl.ds(pl.multiple_of(src_row, _ALIGN), size)],
                                 dst.at[pl.ds(pl.multiple_of(dst_row, _ALIGN), size)], sem)


def _wait_rows(total, limit, copy_of_size):
    size = _ALIGN
    while size * 2 <= limit:
        size *= 2
    while size >= _ALIGN:
        @pl.when((total & size) != 0)
        def _(size=size):
            copy_of_size(size).wait()
        size //= 2


def _zero_unused_blocks(zblk, dst_ref, first, sem, wait):
    nb = dst_ref.shape[0] // _BLK

    def one(b, carry):
        copy = pltpu.make_async_copy(
            zblk, dst_ref.at[pl.ds(pl.multiple_of(b * _BLK, _BLK), _BLK)], sem)
        if wait:
            copy.wait()
        else:
            copy.start()
        return carry

    lax.fori_loop(first, nb, one, 0)


def _dispatch_kernel(lo_s, goff_s, nch_s, tail_s, tsum_s, hn_ref, route_t_ref, xs_ref,
                     buf, zblk, sem, *, nt):
    i = pl.program_id(0)
    tt = hn_ref.shape[0]
    cap = buf.shape[1]
    rt = route_t_ref[...]
    pos16 = [rt[2 * _K + k:2 * _K + k + 1, :].astype(jnp.int32).astype(jnp.int16)
             for k in range(_K)]

    slot = lax.rem(i, 2)
    tile_buf = buf.at[slot]
    fill_sem = sem.at[2]

    def wait_tile(tile, sl):
        _wait_rows(tsum_s[tile], cap,
                   lambda size: _part_copy(buf.at[sl], 0, xs_ref, 0, size, sem.at[sl]))

    def zero_fill(wait):
        def per_expert(e, carry):
            def one(j, c):
                copy = _part_copy(zblk, 0, xs_ref, tail_s[e] + j * _ALIGN, _ALIGN, fill_sem)
                if wait:
                    copy.wait()
                else:
                    copy.start()
                return c
            return lax.fori_loop(0, tail_s[_E + e], one, carry)

        lax.fori_loop(0, _E, per_expert, 0)
        _zero_unused_blocks(zblk, xs_ref, tail_s[2 * _E], fill_sem, wait)

    @pl.when(i == 0)
    def _():
        zblk[...] = jnp.zeros_like(zblk)
        zero_fill(wait=False)

    @pl.when(i > 1)
    def _():
        wait_tile(i - 2, slot)

    rows = cap // _SORT_CHUNKS
    hn = hn_ref[...]
    for c in range(_SORT_CHUNKS):
        r_iota = (lax.broadcasted_iota(jnp.int32, (rows, tt), 0) + c * rows).astype(jnp.int16)
        perm = jnp.zeros((rows, tt), _bf16)
        for k in range(_K):
            perm = jnp.where(r_iota == pos16[k], jnp.ones((), _bf16), perm)
        buf[slot, c * rows:(c + 1) * rows, :] = _pack_pairs(_dot(perm, hn))

    def start_part(local_row, global_row, size):
        _part_copy(tile_buf, local_row, xs_ref, global_row, size, sem.at[slot]).start()

    _for_each_part(lo_s, goff_s, nch_s, i, start_part)

    @pl.when(i == nt - 1)
    def _():
        zero_fill(wait=True)
        if nt > 1:
            wait_tile(i - 1, 1 - slot)
        wait_tile(i, slot)


def _dispatch(lo, goff, nch, tail, tsum, hn, route_t, n_rows):
    s = hn.shape[0]
    tt = min(_TT, s)
    nt = s // tt
    cap = tt * _K + _E * _ALIGN
    grid_spec = pltpu.PrefetchScalarGridSpec(
        num_scalar_prefetch=5,
        grid=(nt,),
        in_specs=[
            pl.BlockSpec((tt, _D), lambda i, *_: (i, 0)),
            pl.BlockSpec((_RT_ROWS, tt), lambda i, *_: (0, i)),
        ],
        out_specs=pl.BlockSpec(memory_space=pl.ANY),
        scratch_shapes=[pltpu.VMEM((2, cap, _D // 2), jnp.int32),
                        pltpu.VMEM((_BLK, _D // 2), jnp.int32), pltpu.SemaphoreType.DMA((3,))],
    )
    return pl.pallas_call(
        functools.partial(_dispatch_kernel, nt=nt),
        grid_spec=grid_spec,
        out_shape=jax.ShapeDtypeStruct((n_rows, _D // 2), jnp.int32),
        compiler_params=pltpu.CompilerParams(dimension_semantics=("arbitrary",),
                                             vmem_limit_bytes=_VMEM_LIMIT),
        name="moe_dispatch",
    )(lo, goff, nch, tail, tsum, hn, route_t)


def _expert_kernel(tail_s, wgu_ref, bgu_ref, wd_ref, bd_ref, xs_ref, ys_ref,
                   wgu_c, wd_c, xin, yout, zblk, sem_in, sem_out):
    e = pl.program_id(0)
    used = tail_s[2 * _E]
    first = tail_s[3 * _E + e]
    nblk = tail_s[4 * _E + e]

    def rows(g):
        return pl.ds(pl.multiple_of(g * _BLK, _BLK), _BLK)

    def in_copy(g, slot):
        return pltpu.make_async_copy(xs_ref.at[rows(g)], xin.at[slot], sem_in.at[slot])

    def out_copy(g, slot):
        return pltpu.make_async_copy(yout.at[slot], ys_ref.at[rows(g)], sem_out.at[slot])

    def slot_of(g):
        return lax.rem(g, _RING)

    @pl.when(e == 0)
    def _():
        for g in range(_AHEAD):
            @pl.when(g < used)
            def _(g=g):
                in_copy(g, g).start()
        zblk[...] = jnp.zeros_like(zblk)
        _zero_unused_blocks(zblk, ys_ref, used, sem_out.at[_RING], False)

    @pl.when(nblk > 0)
    def _():
        step = _LANES
        for r in range(0, _D, step):
            wgu_c[r:r + step, :] = wgu_ref[0, 0, r:r + step, :].astype(_bf16)
        for r in range(0, _F, step):
            wd_c[r:r + step, :] = wd_ref[0, 0, r:r + step, :].astype(_bf16)

    def unit(g, n):
        for d in range(n):
            in_copy(g + d, slot_of(g + d)).wait()
        for d in range(_AHEAD, _AHEAD + n):
            @pl.when(g + d < used)
            def _(d=d):
                in_copy(g + d, slot_of(g + d)).start()
        parts = [_unpack_pairs(xin[slot_of(g + d)]) for d in range(n)]
        x_hi = jnp.concatenate([p[0] for p in parts], axis=0)
        x_lo = jnp.concatenate([p[1] for p in parts], axis=0)
        gu = _dot(jnp.concatenate([x_hi, x_lo], axis=1), wgu_c[...]) + bgu_ref[0, 0]
        x_glu = jnp.minimum(gu[:, :_F], _SWIGLU_LIMIT)
        x_lin = jnp.clip(gu[:, _F:], -_SWIGLU_LIMIT, _SWIGLU_LIMIT)
        act = x_glu * jax.nn.sigmoid(_SWIGLU_ALPHA * x_glu) * (x_lin + 1.0)
        y = _dot(act.astype(_bf16), wd_c[...]) + bd_ref[0, 0]
        packed = _pack_pairs(y.astype(_bf16).astype(_f32))
        for d in range(n):
            @pl.when(g + d >= _RING)
            def _(d=d):
                out_copy(g + d - _RING, slot_of(g + d)).wait()
            yout[slot_of(g + d)] = packed[d * _BLK:(d + 1) * _BLK, :]
            out_copy(g + d, slot_of(g + d)).start()

    def widest(j, carry):
        unit(first + _AHEAD * j, _AHEAD)
        return carry

    lax.fori_loop(0, nblk // _AHEAD, widest, 0)
    n = _AHEAD // 2
    while n >= 1:
        @pl.when((nblk & n) != 0)
        def _(n=n):
            unit(first + (nblk & (-2 * n)), n)
        n //= 2

    @pl.when(e == _E - 1)
    def _():
        for d in range(1, _RING + 1):
            @pl.when(used >= d)
            def _(d=d):
                out_copy(used - d, slot_of(used - d)).wait()

        _zero_unused_blocks(zblk, ys_ref, used, sem_out.at[_RING], True)


def _experts(tail, xs, w_gu, b_gu, w_down, b_down, layer):
    n_rows = xs.shape[0]
    wsel = lambda e, *_: (layer, e, 0, 0)
    grid_spec = pltpu.PrefetchScalarGridSpec(
        num_scalar_prefetch=1,
        grid=(_E,),
        in_specs=[
            pl.BlockSpec((1, 1, _D, 2 * _F), wsel),
            pl.BlockSpec((1, 1, 1, 2 * _F), wsel),
            pl.BlockSpec((1, 1, _F, _D), wsel),
            pl.BlockSpec((1, 1, 1, _D), wsel),
            pl.BlockSpec(memory_space=pl.ANY),
        ],
        out_specs=pl.BlockSpec(memory_space=pl.ANY),
        scratch_shapes=[pltpu.VMEM((_D, 2 * _F), _bf16), pltpu.VMEM((_F, _D), _bf16),
                        pltpu.VMEM((_RING, _BLK, _D // 2), jnp.int32),
                        pltpu.VMEM((_RING, _BLK, _D // 2), jnp.int32),
                        pltpu.VMEM((_BLK, _D // 2), jnp.int32),
                        pltpu.SemaphoreType.DMA((_RING,)), pltpu.SemaphoreType.DMA((_RING + 1,))],
    )
    depth = w_gu.shape[0]
    return pl.pallas_call(
        _expert_kernel,
        grid_spec=grid_spec,
        out_shape=jax.ShapeDtypeStruct((n_rows, _D // 2), jnp.int32),
        compiler_params=pltpu.CompilerParams(dimension_semantics=("arbitrary",),
                                             vmem_limit_bytes=_VMEM_LIMIT),
        name="moe_experts",
    )(tail, w_gu, b_gu.reshape(depth, _E, 1, 2 * _F), w_down, b_down.reshape(depth, _E, 1, _D), xs)


def _combine_kernel(lo_s, goff_s, nch_s, tsum_s, route_ref, x1_ref, mod_ref, fin_ref, ys_ref,
                    out_ref, ybuf, sem, *, final, nt):
    i = pl.program_id(0)
    tt = x1_ref.shape[0]
    cap = ybuf.shape[1]
    slot = lax.rem(i, 2)

    def fetch(tile, dst):
        def start_part(local_row, global_row, size):
            _part_copy(ys_ref, global_row, dst, local_row, size, sem).start()
        _for_each_part(lo_s, goff_s, nch_s, tile, start_part)

    @pl.when(i == 0)
    def _():
        ybuf[...] = jnp.zeros_like(ybuf)
        fetch(0, ybuf.at[0])

    _wait_rows(tsum_s[i], cap, lambda size: _part_copy(ys_ref, 0, ybuf.at[slot], 0, size, sem))

    @pl.when(i + 1 < nt)
    def _():
        fetch(i + 1, ybuf.at[1 - slot])

    route = route_ref[...]
    pos16 = [route[:, 2 * _K + k:2 * _K + k + 1].astype(jnp.int32).astype(jnp.int16)
             for k in range(_K)]
    gate16 = [route[:, _K + k:_K + k + 1].astype(_bf16) for k in range(_K)]

    cols = cap // _SORT_CHUNKS
    moe_hi = jnp.zeros((tt, _D // 2), _f32)
    moe_lo = jnp.zeros((tt, _D // 2), _f32)
    for c in range(_SORT_CHUNKS):
        lane_r = (lax.broadcasted_iota(jnp.int32, (tt, cols), 1) + c * cols).astype(jnp.int16)
        wmat = jnp.zeros((tt, cols), _bf16)
        for k in range(_K):
            wmat = jnp.where(lane_r == pos16[k], gate16[k], wmat)
        y_hi, y_lo = _unpack_pairs(ybuf[slot, c * cols:(c + 1) * cols, :])
        moe_hi = moe_hi + _dot(wmat, y_hi)
        moe_lo = moe_lo + _dot(wmat, y_lo)
    moe = jnp.concatenate([moe_hi, moe_lo], axis=1)
    x2 = x1_ref[...] + mod_ref[5:6, :] * moe
    if final:
        x2 = _rms(x2) * fin_ref[...]
    out_ref[...] = x2


def _combine(lo, goff, nch, tsum, route, x1, mod, final_norm, ys, final):
    s = x1.shape[0]
    tt = min(_TT, s)
    nt = s // tt
    cap = tt * _K + _E * _ALIGN
    grid_spec = pltpu.PrefetchScalarGridSpec(
        num_scalar_prefetch=4,
        grid=(nt,),
        in_specs=[
            pl.BlockSpec((tt, _LANES), lambda i, *_: (i, 0)),
            pl.BlockSpec((tt, _D), lambda i, *_: (i, 0)),
            pl.BlockSpec((6, _D), lambda i, *_: (0, 0)),
            pl.BlockSpec((1, _D), lambda i, *_: (0, 0)),
            pl.BlockSpec(memory_space=pl.ANY),
        ],
        out_specs=pl.BlockSpec((tt, _D), lambda i, *_: (i, 0)),
        scratch_shapes=[pltpu.VMEM((2, cap, _D // 2), jnp.int32), pltpu.SemaphoreType.DMA(())],
    )
    return pl.pallas_call(
        functools.partial(_combine_kernel, final=final, nt=nt),
        grid_spec=grid_spec,
        out_shape=jax.ShapeDtypeStruct((s, _D), _f32),
        compiler_params=pltpu.CompilerParams(dimension_semantics=("arbitrary",),
                                             vmem_limit_bytes=_VMEM_LIMIT),
        name="moe_combine",
    )(lo, goff, nch, tsum, route, x1, mod, final_norm.reshape(1, _D), ys)


def _moe(x1, hn2, logits, mod, final_norm, w_gu, b_gu, w_down, b_down, layer, final):
    s = x1.shape[0]
    tt = min(_TT, s)
    nt = s // tt
    ntp = -(-nt // _SUBLANES) * _SUBLANES
    n_rows = -(-(s * _K + (_ALIGN - 1) * nt * _E) // _BLK) * _BLK + _E * _BLK
    route, route_t, cnt = _route(logits)
    cnt = jnp.pad(cnt.reshape(nt, _E), ((0, ntp - nt), (0, 0)))
    lo, goff, nch, tail, tsum = _offsets(cnt)
    lo_s, goff_s, nch_s = (a[:nt].reshape(nt * _E) for a in (lo, goff, nch))
    tail_s = tail[:5].reshape(5 * _E)
    tsum_s = tsum[:nt, 0]
    xs = _dispatch(lo_s, goff_s, nch_s, tail_s, tsum_s, hn2, route_t, n_rows)
    ys = _experts(tail_s, xs, w_gu, b_gu, w_down, b_down, layer)
    return _combine(lo_s, goff_s, nch_s, tsum_s, route, x1, mod, final_norm, ys, final)


def _decay_tables(tb):
    t = np.arange(tb)
    tri = ((t[:, None] // _CH) == (t[None, :] // _CH)) & (t[None, :] <= t[:, None])
    rel = (t[:, None] - t[None, :]).astype(np.float64)
    lg = np.log(1.0 - 2.0 ** (-5.0 - np.arange(_HEADS, dtype=np.float64)))
    dmat = np.where(rel >= 0, np.exp(np.maximum(rel, 0.0)[None] * lg[:, None, None]), 0.0)
    return jnp.asarray(tri.astype(np.float32), _bf16), jnp.asarray(dmat.astype(np.float32))


def _rotary_tables(s, tb):
    half = _DH // 2
    inv = (np.float32(10000.0) ** (-np.arange(half, dtype=np.float32) / np.float32(half)))
    inv = inv.astype(np.float64)

    def cos_sin(positions):
        ang = positions.astype(np.float64)[:, None] * inv[None, :]
        cos = np.concatenate([np.cos(ang)] * 2, axis=1)
        sin = np.concatenate([np.sin(ang)] * 2, axis=1)
        return cos.astype(np.float32), sin.astype(np.float32)

    cos_a, sin_a = cos_sin(np.arange(s // tb) * tb)
    cos_b, sin_b = cos_sin(np.arange(tb))
    return (jnp.asarray(np.stack([cos_a, sin_a], axis=1)), jnp.asarray(np.stack([cos_b, sin_b], axis=0)))


def _lower_bounds_kernel(p_ref, o_ref):
    p = p_ref[...]
    n = p.shape[0]
    m = p[0:1, :]
    for j in range(1, n):
        m = jnp.maximum(m, p[j:j + 1, :])
    ex = [jnp.exp(p[j:j + 1, :] - m) for j in range(n)]
    den = ex[0]
    for j in range(1, n):
        den = den + ex[j]
    acc = jnp.zeros_like(den)
    rows = []
    for j in range(n):
        acc = acc + ex[j] / den
        rows.append(acc)
    o_ref[...] = jnp.concatenate(rows, axis=0)


def _lower_bounds(p):
    return pl.pallas_call(
        _lower_bounds_kernel,
        out_shape=jax.ShapeDtypeStruct(p.shape, _f32),
        name="hgrn2_lower_bounds",
    )(p)


def kernel(x, c, w_ada, b_ada, w_in, hg_lower_bounds, hg_norm, ret_norm, w_out, pool_w, pool_b,
           pool_scale, w_router, b_router, w_gu, b_gu, w_down, b_down, final_norm):
    b, s, d = x.shape
    assert b == 1 and d == _D
    depth = w_ada.shape[0]
    assert depth == 2
    xs = x.reshape(s, d)
    mod = _ada(c, w_ada, b_ada)
    lb_all = _lower_bounds(hg_lower_bounds)
    x1, hn2, logits = _mixer0(xs, mod[0], w_in[0], lb_all[0], hg_norm[0], ret_norm[0], w_out[0],
                              w_router[0], b_router[0])
    xs = _moe(x1, hn2, logits, mod[0], final_norm, w_gu, b_gu, w_down, b_down, 0, False)
    x1, hn2, logits = _pool_mixer(xs, mod[1], pool_w[0], pool_b[0], pool_scale[0], w_router[1],
                                  b_router[1])
    xs = _moe(x1, hn2, logits, mod[1], final_norm, w_gu, b_gu, w_down, b_down, 1, True)
    return xs.reshape(b, s, d)
```

```python
import functools
import math

import jax
import jax.numpy as jnp
import numpy as np
from jax import lax
from jax.experimental import pallas as pl
from jax.experimental.pallas import tpu as pltpu

_LANES = 128
_SUBLANES = 8
_D = 1024
_E = 32
_K = 4
_F = 1024
_HEADS = 4
_DH = 128
_CH = 64
_SUB = 16
_EPS = 1e-6
_BLK = 256
_TB = 256
_TBP = 512
_TT = 512
_ALIGN = _SUBLANES
_RT_ROWS = 16
_BULK = 64
_SORT_CHUNKS = 3
_AHEAD = 4
_RING = 2 * _AHEAD
_NEG = -1e30
_LOG2E = 1.4426950408889634
_SWIGLU_LIMIT = 7.0
_SWIGLU_ALPHA = 1.702
_POOL_WINDOWS = (2, 4, 8, 16)
_HALO = 16
_VMEM_BYTES = 64 * 1024 * 1024
_VMEM_LIMIT = _VMEM_BYTES // 8 * 7

_f32 = jnp.float32
_bf16 = jnp.bfloat16


def _dot(a, b):
    return jnp.dot(a, b, preferred_element_type=_f32)


def _dot_nt(a, b):
    return lax.dot_general(a, b, (((1,), (1,)), ((), ())), preferred_element_type=_f32)


def _split3(a):
    a0 = a.astype(_bf16)
    r1 = a - a0.astype(_f32)
    a1 = r1.astype(_bf16)
    a2 = (r1 - a1.astype(_f32)).astype(_bf16)
    return a0, a1, a2


def _dot_exact_lhs(m, a):
    a0, a1, a2 = _split3(a)
    return _dot(m, a0) + _dot(m, a1) + _dot(m, a2)


def _rms(x):
    return x * lax.rsqrt(jnp.mean(x * x, axis=-1, keepdims=True) + _EPS)


def _silu(x):
    return x * jax.nn.sigmoid(x)


def _ada_kernel(c_ref, w_ref, b_ref, o_ref):
    c = c_ref[...]
    cond = _silu(c)
    o_ref[0] = jnp.sum(w_ref[0] * cond, axis=0, keepdims=True) + b_ref[0]


def _ada(c, w_ada, b_ada):
    depth = w_ada.shape[0]
    n = depth * 6
    out = pl.pallas_call(
        _ada_kernel,
        grid=(n,),
        in_specs=[
            pl.BlockSpec((_D, 1), lambda j: (0, 0)),
            pl.BlockSpec((1, _D, _D), lambda j: (j // 6, 0, j % 6)),
            pl.BlockSpec((1, 1, _D), lambda j: (j, 0, 0)),
        ],
        out_specs=pl.BlockSpec((1, 1, _D), lambda j: (j, 0, 0)),
        out_shape=jax.ShapeDtypeStruct((n, 1, _D), _f32),
        name="ada_mod",
    )(c.reshape(_D, 1), w_ada, b_ada.reshape(n, 1, _D))
    return out.reshape(depth, 6, _D)


def _ffn_prologue(x1, mod, wr_ref, br_ref, x1_ref, hn2_ref, lg_ref):
    sh_f, sc_f = mod[3:4, :], mod[4:5, :]
    hn2 = _rms(x1) * (1.0 + sc_f) + sh_f
    x1_ref[...] = x1
    a0, a1, _ = _split3(hn2)
    hn2_ref[...] = a0
    w0, w1, _ = _split3(wr_ref[...])
    both = _dot(a0, jnp.concatenate([w0, w1], axis=1))
    lg_ref[...] = (both[:, :_E] + both[:, _E:] + _dot(a1, w0) + br_ref[...]).T


def _mixer0_kernel(x_ref, mod_ref, win_ref, lb_ref, hgg_ref, rtg_ref, wout_ref, rot_a_ref, rot_b_ref,
                   tri_ref, dmat_ref, wr_ref, br_ref, x1_ref, hn2_ref, lg_ref,
                   q_s, k_s, v_s, b_s, o_s, st_hg, st_rt):
    tb = x_ref.shape[0]
    nch = tb // _CH
    nsub = _CH // _SUB

    @pl.when(pl.program_id(0) == 0)
    def _():
        st_hg[...] = jnp.zeros_like(st_hg)
        st_rt[...] = jnp.zeros_like(st_rt)

    x = x_ref[...]
    mod = mod_ref[...]
    sh_m, sc_m, g_m = mod[0:1, :], mod[1:2, :], mod[2:3, :]
    hb = (_rms(x) * (1.0 + sc_m) + sh_m).astype(_bf16)
    width = _HEADS * _DH

    def sec(j):
        return _dot(hb, win_ref[:, j * width:(j + 1) * width])

    hq = sec(0)
    lb = lb_ref[...]
    fg = lb + (1.0 - lb) * jax.nn.sigmoid(sec(1))
    hi = sec(2)
    g = jnp.log(fg) * _LOG2E
    bcum = _dot_exact_lhs(tri_ref[...], g)
    qh = _silu(hq)
    kk = 1.0 - fg
    for h in range(_HEADS):
        sl = slice(h * _DH, (h + 1) * _DH)
        q_s[h] = qh[:, sl]
        k_s[h] = kk[:, sl]
        v_s[h] = hi[:, sl]
        b_s[h] = bcum[:, sl]

    hgg = hgg_ref[...]
    row64 = lax.broadcasted_iota(jnp.int32, (_HEADS, _CH, _DH), 1)
    row16 = lax.broadcasted_iota(jnp.int32, (_HEADS * nsub, _SUB, 1), 1)
    lead16 = lax.broadcasted_iota(jnp.int32, (_HEADS * nsub, _SUB, _CH), 0)
    col16 = lax.broadcasted_iota(jnp.int32, (_HEADS * nsub, _SUB, _CH), 2) - _SUB * (lead16 & (nsub - 1))

    def chunk(c, carry):
        rows = pl.ds(pl.multiple_of(c * _CH, _CH), _CH)
        q = q_s[:, rows, :]
        k = k_s[:, rows, :]
        v = v_s[:, rows, :]
        b = b_s[:, rows, :]
        st = st_hg[...]
        vb = v.astype(_bf16)
        qe = (q * jnp.exp2(b)).astype(_bf16)
        o = jnp.einsum('htd,hed->hte', qe, st.astype(_bf16), preferred_element_type=_f32)
        blocks = [jnp.zeros((_HEADS, _SUB, _CH), _f32)]
        for i in range(1, nsub):
            ref = b[:, _SUB * i - 1:_SUB * i, :]
            qp = q[:, _SUB * i:_SUB * (i + 1), :] * jnp.exp2(b[:, _SUB * i:_SUB * (i + 1), :] - ref)
            kp = k * jnp.exp2(jnp.where(row64 < _SUB * i, ref - b, _NEG))
            blocks.append(jnp.einsum('htd,hsd->hts', qp.astype(_bf16), kp.astype(_bf16),
                                     preferred_element_type=_f32))
        a = jnp.concatenate(blocks, axis=1)
        q3 = q.reshape(_HEADS * nsub, _SUB, _DH)
        k3 = k.reshape(_HEADS * nsub, _SUB, _DH)
        b3 = b.reshape(_HEADS * nsub, _SUB, _DH)
        ad = jnp.zeros((_HEADS * nsub, _SUB, _CH), _f32)
        for s in range(_SUB):
            e = jnp.exp2(b3 - b3[:, s:s + 1, :])
            w = jnp.sum(q3 * e * k3[:, s:s + 1, :], axis=-1, keepdims=True)
            ad = jnp.where((col16 == s) & (row16 >= s), w, ad)
        a = a + ad.reshape(_HEADS, _CH, _CH)
        o = o + jnp.einsum('hts,hse->hte', a.astype(_bf16), vb, preferred_element_type=_f32)
        bl = b[:, _CH - 1:_CH, :]
        kd = (k * jnp.exp2(bl - b)).astype(_bf16)
        dec = jnp.exp2(bl)
        for h in range(_HEADS):
            st_hg[h] = st[h] * dec[h] + _dot(v[h].T.astype(_bf16), kd[h])
        o = o * lax.rsqrt(jnp.mean(o * o, axis=-1, keepdims=True) + _EPS) * hgg
        o_s[:, rows, :] = o
        return carry

    lax.fori_loop(0, nch, chunk, 0, unroll=True)
    hog = sec(3)
    cat_hg = jnp.concatenate([o_s[h] for h in range(_HEADS)], axis=1) * _silu(hog)

    rq = sec(4)
    rk = sec(5)
    rv = sec(6)
    cos_a, sin_a = rot_a_ref[0, 0:1, :], rot_a_ref[0, 1:2, :]
    cos_b, sin_b = rot_b_ref[0], rot_b_ref[1]
    first_half = lax.broadcasted_iota(jnp.int32, (1, _DH), 1) < _DH // 2
    cosf = cos_a * cos_b - sin_a * sin_b
    sinf = (sin_a * cos_b + cos_a * sin_b) * jnp.where(first_half, -1.0, 1.0)
    tcol = lax.broadcasted_iota(jnp.int32, (tb, 1), 0).astype(_f32)
    rtg = rtg_ref[...]
    outs = []
    for h in range(_HEADS):
        sl = slice(h * _DH, (h + 1) * _DH)
        lg = math.log(1.0 - 2.0 ** (-5.0 - h))
        xq = rq[:, sl]
        xk = rk[:, sl]
        q = xq * cosf + pltpu.roll(xq, _DH // 2, 1) * sinf
        k = (xk * cosf + pltpu.roll(xk, _DH // 2, 1) * sinf) * (_DH ** -0.5)
        v = rv[:, sl]
        scores = _dot_nt(q.astype(_bf16), k.astype(_bf16)) * dmat_ref[h]
        o = _dot(scores.astype(_bf16), v.astype(_bf16))
        st = st_rt[h]
        qd = q * jnp.exp((tcol + 1.0) * lg)
        o = o + _dot_nt(qd.astype(_bf16), st.astype(_bf16))
        kd = k * jnp.exp((tb - 1.0 - tcol) * lg)
        st_rt[h] = st * math.exp(tb * lg) + _dot(v.T.astype(_bf16), kd.astype(_bf16))
        o = o * lax.rsqrt(jnp.mean(o * o, axis=-1, keepdims=True) + _EPS) * rtg[h]
        outs.append(o)
    rg = sec(7)
    cat_rt = jnp.concatenate(outs, axis=1) * _silu(rg)

    cat = jnp.concatenate([cat_hg, cat_rt], axis=1).astype(_bf16)
    mix = _dot(cat, wout_ref[...])
    x1 = x + g_m * mix
    _ffn_prologue(x1, mod, wr_ref, br_ref, x1_ref, hn2_ref, lg_ref)


def _mixer0(x, mod, w_in, lb, hg_gain, ret_gain, w_out, w_router, b_router):
    s = x.shape[0]
    tb = min(_TB, s)
    rot_a, rot_b = _rotary_tables(s, tb)
    tri, dmat = _decay_tables(tb)
    width = _HEADS * _DH
    full = lambda shape: pl.BlockSpec(shape, lambda i: (0,) * len(shape))
    rowblk = lambda w: pl.BlockSpec((tb, w), lambda i: (i, 0))
    return pl.pallas_call(
        _mixer0_kernel,
        grid=(s // tb,),
        in_specs=[
            rowblk(_D), full((6, _D)), full((_D, 8 * width)), full((1, width)),
            full((_HEADS, 1, _DH)), full((_HEADS, 1, _DH)), full((2 * width, _D)),
            pl.BlockSpec((1, 2, _DH), lambda i: (i, 0, 0)), full((2, tb, _DH)),
            full((tb, tb)), full((_HEADS, tb, tb)), full((_D, _E)), full((1, _E)),
        ],
        out_specs=[rowblk(_D), rowblk(_D), pl.BlockSpec((_E, tb), lambda i: (0, i))],
        out_shape=[jax.ShapeDtypeStruct((s, _D), _f32), jax.ShapeDtypeStruct((s, _D), _bf16),
                   jax.ShapeDtypeStruct((_E, s), _f32)],
        scratch_shapes=[pltpu.VMEM((_HEADS, tb, _DH), _f32)] * 5
        + [pltpu.VMEM((_HEADS, _DH, _DH), _f32)] * 2,
        compiler_params=pltpu.CompilerParams(dimension_semantics=("arbitrary",),
                                             vmem_limit_bytes=_VMEM_LIMIT),
        name="mixer_hgrn2_retention",
    )(x, mod, w_in.astype(_bf16), lb.reshape(1, width), hg_gain.reshape(_HEADS, 1, _DH),
      ret_gain.reshape(_HEADS, 1, _DH), w_out.astype(_bf16), rot_a, rot_b,
      tri, dmat, w_router,
      b_router.reshape(1, _E))


def _pool_kernel(x_ref, mod_ref, pw_ref, pb_ref, ps_ref, wr_ref, br_ref,
                 x1_ref, hn2_ref, lg_ref, halo):
    tb = x_ref.shape[0]
    i = pl.program_id(0)

    @pl.when(i == 0)
    def _():
        halo[...] = jnp.zeros_like(halo)

    x = x_ref[...]
    mod = mod_ref[...]
    sh_m, sc_m, g_m = mod[0:1, :], mod[1:2, :], mod[2:3, :]
    hn = _rms(x) * (1.0 + sc_m) + sh_m
    ext = jnp.concatenate([halo[...], hn], axis=0)
    halo[...] = hn[tb - _HALO:, :]
    t = (i * tb + lax.broadcasted_iota(jnp.int32, (tb, 1), 0)).astype(_f32)
    gw = _D // len(_POOL_WINDOWS)
    ys = []
    for gi, w in enumerate(_POOL_WINDOWS):
        acc = ext[:, gi * gw:(gi + 1) * gw]
        shift = 1
        while shift < w:
            acc = acc + pltpu.roll(acc, shift, 0)
            shift *= 2
        win = acc[_HALO:, :]
        inv_cnt = 1.0 / jnp.minimum(t + 1.0, float(w))
        p = win * inv_cnt - hn[:, gi * gw:(gi + 1) * gw]
        ys.append(_dot(p.astype(_bf16), pw_ref[gi]) + pb_ref[gi])
    mix = jnp.concatenate(ys, axis=1) * ps_ref[...]
    x1 = x + g_m * mix
    _ffn_prologue(x1, mod, wr_ref, br_ref, x1_ref, hn2_ref, lg_ref)


def _pool_mixer(x, mod, pool_w, pool_b, pool_scale, w_router, b_router):
    s = x.shape[0]
    tb = min(_TBP, s)
    ng = len(_POOL_WINDOWS)
    gw = _D // ng
    full = lambda shape: pl.BlockSpec(shape, lambda i: (0,) * len(shape))
    rowblk = lambda w: pl.BlockSpec((tb, w), lambda i: (i, 0))
    return pl.pallas_call(
        _pool_kernel,
        grid=(s // tb,),
        in_specs=[rowblk(_D), full((6, _D)), full((ng, gw, gw)), full((ng, 1, gw)), full((1, _D)),
                  full((_D, _E)), full((1, _E))],
        out_specs=[rowblk(_D), rowblk(_D), pl.BlockSpec((_E, tb), lambda i: (0, i))],
        out_shape=[jax.ShapeDtypeStruct((s, _D), _f32), jax.ShapeDtypeStruct((s, _D), _bf16),
                   jax.ShapeDtypeStruct((_E, s), _f32)],
        scratch_shapes=[pltpu.VMEM((_HALO, _D), _f32)],
        compiler_params=pltpu.CompilerParams(dimension_semantics=("arbitrary",),
                                             vmem_limit_bytes=_VMEM_LIMIT),
        name="mixer_pool",
    )(x, mod, pool_w.astype(_bf16), pool_b.reshape(ng, 1, gw), pool_scale.reshape(1, _D),
      w_router, b_router.reshape(1, _E))


def _route_kernel(lg_ref, route_ref, route_t_ref, cnt_ref):
    tt = lg_ref.shape[1]
    vals = lg_ref[...]
    sub = lax.broadcasted_iota(jnp.int32, (_E, tt), 0).astype(_f32)
    tops, idxs, hots = [], [], []
    for _ in range(_K):
        m = jnp.max(vals, axis=0, keepdims=True)
        idx = jnp.min(jnp.where(vals == m, sub, float(_E)), axis=0, keepdims=True)
        hot = sub == idx
        vals = jnp.where(hot, -jnp.inf, vals)
        tops.append(m)
        idxs.append(idx)
        hots.append(hot)
    ex = [jnp.exp(m - tops[0]) for m in tops]
    den = ex[0] + ex[1] + ex[2] + ex[3]
    member = jnp.zeros((_E, tt), _f32)
    for hot in hots:
        member = member + jnp.where(hot, 1.0, 0.0)
    member_b = member.astype(_bf16)
    s_i = lax.broadcasted_iota(jnp.int32, (tt, tt), 0)
    t_i = lax.broadcasted_iota(jnp.int32, (tt, tt), 1)
    earlier = jnp.where(s_i < t_i, 1.0, 0.0).astype(_bf16)
    cum = _dot(member_b, earlier)
    cnt_ref[0] = _dot_nt(jnp.ones((_SUBLANES, tt), _bf16), member_b)[0:1, :]
    cnt = jnp.sum(member, axis=1, keepdims=True)
    r8 = jnp.floor((cnt + (_ALIGN - 1.0)) * (1.0 / _ALIGN)) * _ALIGN
    e_i = lax.broadcasted_iota(jnp.int32, (_E, _E), 0)
    j_i = lax.broadcasted_iota(jnp.int32, (_E, _E), 1)
    before = jnp.where(j_i < e_i, 1.0, 0.0).astype(_bf16)
    lo = _dot(before, jnp.broadcast_to(r8, (_E, _LANES)).astype(_bf16))[:, 0:1]
    row = cum + lo
    out_row = lax.broadcasted_iota(jnp.int32, (_RT_ROWS, tt), 0)
    out = jnp.zeros((_RT_ROWS, tt), _f32)
    for k in range(_K):
        pos = jnp.sum(jnp.where(hots[k], row, 0.0), axis=0, keepdims=True)
        out = jnp.where(out_row == k, idxs[k], out)
        out = jnp.where(out_row == _K + k, ex[k] / den, out)
        out = jnp.where(out_row == 2 * _K + k, pos, out)
    route_t_ref[...] = out
    route_ref[...] = jnp.zeros_like(route_ref)
    route_ref[:, 0:_RT_ROWS] = out.T


def _route(logits_t):
    s = logits_t.shape[1]
    tt = min(_TT, s)
    nt = s // tt
    return pl.pallas_call(
        _route_kernel,
        grid=(nt,),
        in_specs=[pl.BlockSpec((_E, tt), lambda i: (0, i))],
        out_specs=[pl.BlockSpec((tt, _LANES), lambda i: (i, 0)),
                   pl.BlockSpec((_RT_ROWS, tt), lambda i: (0, i)),
                   pl.BlockSpec((1, 1, _E), lambda i: (i, 0, 0))],
        out_shape=[jax.ShapeDtypeStruct((s, _LANES), _f32),
                   jax.ShapeDtypeStruct((_RT_ROWS, s), _f32),
                   jax.ShapeDtypeStruct((nt, 1, _E), _f32)],
        compiler_params=pltpu.CompilerParams(dimension_semantics=("arbitrary",)),
        name="route_topk_rank",
    )(logits_t)


def _offsets_kernel(cnt_ref, lo_ref, goff_ref, nch_ref, tail_ref, tsum_ref):
    cnt = cnt_ref[...]
    ntp = cnt.shape[0]
    r8 = jnp.floor((cnt + (_ALIGN - 1.0)) * (1.0 / _ALIGN)) * _ALIGN
    j_i = lax.broadcasted_iota(jnp.int32, (_E, _E), 0)
    e_i = lax.broadcasted_iota(jnp.int32, (_E, _E), 1)
    before = jnp.where(j_i < e_i, 1.0, 0.0).astype(_bf16)

    def times_before(a):
        a0, a1, a2 = _split3(a)
        return _dot(a0, before) + _dot(a1, before) + _dot(a2, before)

    lo = times_before(r8)
    tot = jnp.sum(r8, axis=0, keepdims=True)
    cap = jnp.floor((tot + (_BLK - 1.0)) * (1.0 / _BLK)) * _BLK
    start = times_before(jnp.broadcast_to(cap, (_SUBLANES, _E)))[0:1, :]
    a_i = lax.broadcasted_iota(jnp.int32, (ntp, ntp), 0)
    b_i = lax.broadcasted_iota(jnp.int32, (ntp, ntp), 1)
    earlier = jnp.where(b_i < a_i, 1.0, 0.0).astype(_bf16)
    goff = start + _dot_exact_lhs(earlier, r8)
    lo_ref[...] = lo.astype(jnp.int32)
    goff_ref[...] = goff.astype(jnp.int32)
    nch_ref[...] = (r8 * (1.0 / _ALIGN)).astype(jnp.int32)
    trow = lax.broadcasted_iota(jnp.int32, (_SUBLANES, _E), 0)
    end = start + cap
    used_blocks = jnp.max(end, axis=-1, keepdims=True) * (1.0 / _BLK)
    tail = jnp.where(trow == 0, start + tot,
                     jnp.where(trow == 1, (cap - tot) * (1.0 / _ALIGN),
                               jnp.where(trow == 2, used_blocks,
                                         jnp.where(trow == 3, start * (1.0 / _BLK),
                                                   jnp.where(trow == 4, cap * (1.0 / _BLK), 0.0)))))
    tail_ref[...] = tail.astype(jnp.int32)
    tsum = jnp.sum(r8, axis=-1, keepdims=True)
    tsum_ref[...] = jnp.broadcast_to(tsum, (ntp, _E)).astype(jnp.int32)


def _offsets(cnt):
    ntp = cnt.shape[0]
    i32 = jnp.int32
    return pl.pallas_call(
        _offsets_kernel,
        out_shape=[jax.ShapeDtypeStruct((ntp, _E), i32),
                   jax.ShapeDtypeStruct((ntp, _E), i32), jax.ShapeDtypeStruct((ntp, _E), i32),
                   jax.ShapeDtypeStruct((_SUBLANES, _E), i32), jax.ShapeDtypeStruct((ntp, _E), i32)],
        name="route_offsets",
    )(cnt)


_HI_MASK = -65536


def _pack_pairs(x):
    half = x.shape[1] // 2
    bits = lax.bitcast_convert_type(x, jnp.int32)
    return (bits[:, :half] & _HI_MASK) | lax.shift_right_logical(bits[:, half:], 16)


def _unpack_pairs(p):
    hi = lax.bitcast_convert_type(p & _HI_MASK, _f32).astype(_bf16)
    lo = lax.bitcast_convert_type(lax.shift_left(p, 16), _f32).astype(_bf16)
    return hi, lo


def _for_each_part(lo_s, goff_s, nch_s, tile, fn):
    def per_expert(e, carry):
        rows = nch_s[tile * _E + e] * _ALIGN
        lo = lo_s[tile * _E + e]
        go = goff_s[tile * _E + e]

        def bulk(j, c):
            fn(lo + j * _BULK, go + j * _BULK, _BULK, 0)
            return c

        lax.fori_loop(0, lax.shift_right_logical(rows, _BULK.bit_length() - 1), bulk, 0)
        size = _BULK // 2
        while size >= _ALIGN:
            @pl.when((rows & size) != 0)
            def _(size=size):
                off = rows & (-2 * size)
                fn(lo + off, go + off, size, 1)
            size //= 2
        return carry
    lax.fori_loop(0, _E, per_expert, 0, unroll=4)


def _part_copy(src, src_row, dst, dst_row, size, sem):
    return pltpu.make_async_copy(src.at[pl.ds(pl.multiple_of(src_row, _ALIGN), size)],
                                 dst.at[pl.ds(pl.multiple_of(dst_row, _ALIGN), size)], sem)


def _wait_rows(total, limit, copy_of_size):
    size = _ALIGN
    while size * 2 <= limit:
        size *= 2
    while size >= _ALIGN:
        @pl.when((total & size) != 0)
        def _(size=size):
            copy_of_size(size).wait()
        size //= 2


def _zero_unused_blocks(zblk, dst_ref, first, sem, wait):
    nb = dst_ref.shape[0] // _BLK

    def one(b, carry):
        copy = pltpu.make_async_copy(
            zblk, dst_ref.at[pl.ds(pl.multiple_of(b * _BLK, _BLK), _BLK)], sem)
        if wait:
            copy.wait()
        else:
            copy.start()
        return carry

    lax.fori_loop(first, nb, one, 0)


def _dispatch_kernel(lo_s, goff_s, nch_s, tail_s, tsum_s, hn_ref, route_t_ref, xs_ref,
                     buf, zblk, sem, *, nt):
    i = pl.program_id(0)
    tt = hn_ref.shape[0]
    cap = buf.shape[1]
    rt = route_t_ref[...]
    pos16 = [rt[2 * _K + k:2 * _K + k + 1, :].astype(jnp.int32).astype(jnp.int16)
             for k in range(_K)]

    slot = lax.rem(i, 2)
    tile_buf = buf.at[slot]
    fill_sem = sem.at[2]

    def wait_tile(tile, sl):
        _wait_rows(tsum_s[tile], cap,
                   lambda size: _part_copy(buf.at[sl], 0, xs_ref, 0, size, sem.at[sl]))

    def zero_fill(wait):
        def per_expert(e, carry):
            def one(j, c):
                copy = _part_copy(zblk, 0, xs_ref, tail_s[e] + j * _ALIGN, _ALIGN, fill_sem)
                if wait:
                    copy.wait()
                else:
                    copy.start()
                return c
            return lax.fori_loop(0, tail_s[_E + e], one, carry)

        lax.fori_loop(0, _E, per_expert, 0)
        _zero_unused_blocks(zblk, xs_ref, tail_s[2 * _E], fill_sem, wait)

    @pl.when(i == 0)
    def _():
        zblk[...] = jnp.zeros_like(zblk)
        zero_fill(wait=False)

    @pl.when(i > 1)
    def _():
        wait_tile(i - 2, slot)

    rows = cap // _SORT_CHUNKS
    hn = hn_ref[...]
    for c in range(_SORT_CHUNKS):
        r_iota = (lax.broadcasted_iota(jnp.int32, (rows, tt), 0) + c * rows).astype(jnp.int16)
        perm = jnp.zeros((rows, tt), _bf16)
        for k in range(_K):
            perm = jnp.where(r_iota == pos16[k], jnp.ones((), _bf16), perm)
        buf[slot, c * rows:(c + 1) * rows, :] = _pack_pairs(_dot(perm, hn))

    def start_part(local_row, global_row, size, priority):
        _part_copy(tile_buf, local_row, xs_ref, global_row, size, sem.at[slot]).start(
            priority=priority)

    _for_each_part(lo_s, goff_s, nch_s, i, start_part)

    @pl.when(i == nt - 1)
    def _():
        zero_fill(wait=True)
        if nt > 1:
            wait_tile(i - 1, 1 - slot)
        wait_tile(i, slot)


def _dispatch(lo, goff, nch, tail, tsum, hn, route_t, n_rows):
    s = hn.shape[0]
    tt = min(_TT, s)
    nt = s // tt
    cap = tt * _K + _E * _ALIGN
    grid_spec = pltpu.PrefetchScalarGridSpec(
        num_scalar_prefetch=5,
        grid=(nt,),
        in_specs=[
            pl.BlockSpec((tt, _D), lambda i, *_: (i, 0)),
            pl.BlockSpec((_RT_ROWS, tt), lambda i, *_: (0, i)),
        ],
        out_specs=pl.BlockSpec(memory_space=pl.ANY),
        scratch_shapes=[pltpu.VMEM((2, cap, _D // 2), jnp.int32),
                        pltpu.VMEM((_BLK, _D // 2), jnp.int32), pltpu.SemaphoreType.DMA((3,))],
    )
    return pl.pallas_call(
        functools.partial(_dispatch_kernel, nt=nt),
        grid_spec=grid_spec,
        out_shape=jax.ShapeDtypeStruct((n_rows, _D // 2), jnp.int32),
        compiler_params=pltpu.CompilerParams(dimension_semantics=("arbitrary",),
                                             vmem_limit_bytes=_VMEM_LIMIT),
        name="moe_dispatch",
    )(lo, goff, nch, tail, tsum, hn, route_t)


def _expert_kernel(tail_s, wgu_ref, bgu_ref, wd_ref, bd_ref, xs_ref, ys_ref,
                   wgu_c, wd_c, xin, yout, zblk, sem_in, sem_out):
    e = pl.program_id(0)
    used = tail_s[2 * _E]
    first = tail_s[3 * _E + e]
    nblk = tail_s[4 * _E + e]

    def rows(g):
        return pl.ds(pl.multiple_of(g * _BLK, _BLK), _BLK)

    def in_copy(g, slot):
        return pltpu.make_async_copy(xs_ref.at[rows(g)], xin.at[slot], sem_in.at[slot])

    def out_copy(g, slot):
        return pltpu.make_async_copy(yout.at[slot], ys_ref.at[rows(g)], sem_out.at[slot])

    def slot_of(g):
        return lax.rem(g, _RING)

    @pl.when(e == 0)
    def _():
        for g in range(_AHEAD):
            @pl.when(g < used)
            def _(g=g):
                in_copy(g, g).start()
        zblk[...] = jnp.zeros_like(zblk)
        _zero_unused_blocks(zblk, ys_ref, used, sem_out.at[_RING], False)

    @pl.when(nblk > 0)
    def _():
        step = _LANES
        for r in range(0, _D, step):
            wgu_c[r:r + step, :] = wgu_ref[0, 0, r:r + step, :].astype(_bf16)
        for r in range(0, _F, step):
            wd_c[r:r + step, :] = wd_ref[0, 0, r:r + step, :].astype(_bf16)

    def unit(g, n):
        for d in range(n):
            in_copy(g + d, slot_of(g + d)).wait()
        for d in range(_AHEAD, _AHEAD + n):
            @pl.when(g + d < used)
            def _(d=d):
                in_copy(g + d, slot_of(g + d)).start()
        parts = [_unpack_pairs(xin[slot_of(g + d)]) for d in range(n)]
        x_hi = jnp.concatenate([p[0] for p in parts], axis=0)
        x_lo = jnp.concatenate([p[1] for p in parts], axis=0)
        gu = _dot(jnp.concatenate([x_hi, x_lo], axis=1), wgu_c[...]) + bgu_ref[0, 0]
        x_glu = jnp.minimum(gu[:, :_F], _SWIGLU_LIMIT)
        x_lin = jnp.clip(gu[:, _F:], -_SWIGLU_LIMIT, _SWIGLU_LIMIT)
        act = x_glu * jax.nn.sigmoid(_SWIGLU_ALPHA * x_glu) * (x_lin + 1.0)
        y = _dot(act.astype(_bf16), wd_c[...]) + bd_ref[0, 0]
        packed = _pack_pairs(y.astype(_bf16).astype(_f32))
        for d in range(n):
            @pl.when(g + d >= _RING)
            def _(d=d):
                out_copy(g + d - _RING, slot_of(g + d)).wait()
            yout[slot_of(g + d)] = packed[d * _BLK:(d + 1) * _BLK, :]
            out_copy(g + d, slot_of(g + d)).start()

    def widest(j, carry):
        unit(first + _AHEAD * j, _AHEAD)
        return carry

    lax.fori_loop(0, nblk // _AHEAD, widest, 0)
    n = _AHEAD // 2
    while n >= 1:
        @pl.when((nblk & n) != 0)
        def _(n=n):
            unit(first + (nblk & (-2 * n)), n)
        n //= 2

    @pl.when(e == _E - 1)
    def _():
        for d in range(1, _RING + 1):
            @pl.when(used >= d)
            def _(d=d):
                out_copy(used - d, slot_of(used - d)).wait()

        _zero_unused_blocks(zblk, ys_ref, used, sem_out.at[_RING], True)


def _experts(tail, xs, w_gu, b_gu, w_down, b_down, layer):
    n_rows = xs.shape[0]
    wsel = lambda e, *_: (layer, e, 0, 0)
    grid_spec = pltpu.PrefetchScalarGridSpec(
        num_scalar_prefetch=1,
        grid=(_E,),
        in_specs=[
            pl.BlockSpec((1, 1, _D, 2 * _F), wsel),
            pl.BlockSpec((1, 1, 1, 2 * _F), wsel),
            pl.BlockSpec((1, 1, _F, _D), wsel),
            pl.BlockSpec((1, 1, 1, _D), wsel),
            pl.BlockSpec(memory_space=pl.ANY),
        ],
        out_specs=pl.BlockSpec(memory_space=pl.ANY),
        scratch_shapes=[pltpu.VMEM((_D, 2 * _F), _bf16), pltpu.VMEM((_F, _D), _bf16),
                        pltpu.VMEM((_RING, _BLK, _D // 2), jnp.int32),
                        pltpu.VMEM((_RING, _BLK, _D // 2), jnp.int32),
                        pltpu.VMEM((_BLK, _D // 2), jnp.int32),
                        pltpu.SemaphoreType.DMA((_RING,)), pltpu.SemaphoreType.DMA((_RING + 1,))],
    )
    depth = w_gu.shape[0]
    return pl.pallas_call(
        _expert_kernel,
        grid_spec=grid_spec,
        out_shape=jax.ShapeDtypeStruct((n_rows, _D // 2), jnp.int32),
        compiler_params=pltpu.CompilerParams(dimension_semantics=("arbitrary",),
                                             vmem_limit_bytes=_VMEM_LIMIT),
        name="moe_experts",
    )(tail, w_gu, b_gu.reshape(depth, _E, 1, 2 * _F), w_down, b_down.reshape(depth, _E, 1, _D), xs)


def _combine_kernel(lo_s, goff_s, nch_s, tsum_s, route_ref, x1_ref, mod_ref, fin_ref, ys_ref,
                    out_ref, ybuf, sem, *, final, nt):
    i = pl.program_id(0)
    tt = x1_ref.shape[0]
    cap = ybuf.shape[1]
    slot = lax.rem(i, 2)

    def fetch(tile, dst):
        def start_part(local_row, global_row, size, priority):
            _part_copy(ys_ref, global_row, dst, local_row, size, sem).start(priority=priority)
        _for_each_part(lo_s, goff_s, nch_s, tile, start_part)

    @pl.when(i == 0)
    def _():
        ybuf[...] = jnp.zeros_like(ybuf)
        fetch(0, ybuf.at[0])

    _wait_rows(tsum_s[i], cap, lambda size: _part_copy(ys_ref, 0, ybuf.at[slot], 0, size, sem))

    @pl.when(i + 1 < nt)
    def _():
        fetch(i + 1, ybuf.at[1 - slot])

    route = route_ref[...]
    pos16 = [route[:, 2 * _K + k:2 * _K + k + 1].astype(jnp.int32).astype(jnp.int16)
             for k in range(_K)]
    gate16 = [route[:, _K + k:_K + k + 1].astype(_bf16) for k in range(_K)]

    cols = cap // _SORT_CHUNKS
    moe_hi = jnp.zeros((tt, _D // 2), _f32)
    moe_lo = jnp.zeros((tt, _D // 2), _f32)
    for c in range(_SORT_CHUNKS):
        lane_r = (lax.broadcasted_iota(jnp.int32, (tt, cols), 1) + c * cols).astype(jnp.int16)
        wmat = jnp.zeros((tt, cols), _bf16)
        for k in range(_K):
            wmat = jnp.where(lane_r == pos16[k], gate16[k], wmat)
        y_hi, y_lo = _unpack_pairs(ybuf[slot, c * cols:(c + 1) * cols, :])
        moe_hi = moe_hi + _dot(wmat, y_hi)
        moe_lo = moe_lo + _dot(wmat, y_lo)
    moe = jnp.concatenate([moe_hi, moe_lo], axis=1)
    x2 = x1_ref[...] + mod_ref[5:6, :] * moe
    if final:
        x2 = _rms(x2) * fin_ref[...]
    out_ref[...] = x2


def _combine(lo, goff, nch, tsum, route, x1, mod, final_norm, ys, final):
    s = x1.shape[0]
    tt = min(_TT, s)
    nt = s // tt
    cap = tt * _K + _E * _ALIGN
    grid_spec = pltpu.PrefetchScalarGridSpec(
        num_scalar_prefetch=4,
        grid=(nt,),
        in_specs=[
            pl.BlockSpec((tt, _LANES), lambda i, *_: (i, 0)),
            pl.BlockSpec((tt, _D), lambda i, *_: (i, 0)),
            pl.BlockSpec((6, _D), lambda i, *_: (0, 0)),
            pl.BlockSpec((1, _D), lambda i, *_: (0, 0)),
            pl.BlockSpec(memory_space=pl.ANY),
        ],
        out_specs=pl.BlockSpec((tt, _D), lambda i, *_: (i, 0)),
        scratch_shapes=[pltpu.VMEM((2, cap, _D // 2), jnp.int32), pltpu.SemaphoreType.DMA(())],
    )
    return pl.pallas_call(
        functools.partial(_combine_kernel, final=final, nt=nt),
        grid_spec=grid_spec,
        out_shape=jax.ShapeDtypeStruct((s, _D), _f32),
        compiler_params=pltpu.CompilerParams(dimension_semantics=("arbitrary",),
                                             vmem_limit_bytes=_VMEM_LIMIT),
        name="moe_combine",
    )(lo, goff, nch, tsum, route, x1, mod, final_norm.reshape(1, _D), ys)


def _moe(x1, hn2, logits, mod, final_norm, w_gu, b_gu, w_down, b_down, layer, final):
    s = x1.shape[0]
    tt = min(_TT, s)
    nt = s // tt
    ntp = -(-nt // _SUBLANES) * _SUBLANES
    n_rows = -(-(s * _K + (_ALIGN - 1) * nt * _E) // _BLK) * _BLK + _E * _BLK
    route, route_t, cnt = _route(logits)
    cnt = jnp.pad(cnt.reshape(nt, _E), ((0, ntp - nt), (0, 0)))
    lo, goff, nch, tail, tsum = _offsets(cnt)
    lo_s, goff_s, nch_s = (a[:nt].reshape(nt * _E) for a in (lo, goff, nch))
    tail_s = tail[:5].reshape(5 * _E)
    tsum_s = tsum[:nt, 0]
    xs = _dispatch(lo_s, goff_s, nch_s, tail_s, tsum_s, hn2, route_t, n_rows)
    ys = _experts(tail_s, xs, w_gu, b_gu, w_down, b_down, layer)
    return _combine(lo_s, goff_s, nch_s, tsum_s, route, x1, mod, final_norm, ys, final)


def _decay_tables(tb):
    t = np.arange(tb)
    tri = ((t[:, None] // _CH) == (t[None, :] // _CH)) & (t[None, :] <= t[:, None])
    rel = (t[:, None] - t[None, :]).astype(np.float64)
    lg = np.log(1.0 - 2.0 ** (-5.0 - np.arange(_HEADS, dtype=np.float64)))
    dmat = np.where(rel >= 0, np.exp(np.maximum(rel, 0.0)[None] * lg[:, None, None]), 0.0)
    return jnp.asarray(tri.astype(np.float32), _bf16), jnp.asarray(dmat.astype(np.float32))


def _rotary_tables(s, tb):
    half = _DH // 2
    inv = (np.float32(10000.0) ** (-np.arange(half, dtype=np.float32) / np.float32(half)))
    inv = inv.astype(np.float64)

    def cos_sin(positions):
        ang = positions.astype(np.float64)[:, None] * inv[None, :]
        cos = np.concatenate([np.cos(ang)] * 2, axis=1)
        sin = np.concatenate([np.sin(ang)] * 2, axis=1)
        return cos.astype(np.float32), sin.astype(np.float32)

    cos_a, sin_a = cos_sin(np.arange(s // tb) * tb)
    cos_b, sin_b = cos_sin(np.arange(tb))
    return (jnp.asarray(np.stack([cos_a, sin_a], axis=1)), jnp.asarray(np.stack([cos_b, sin_b], axis=0)))


def _lower_bounds_kernel(p_ref, o_ref):
    p = p_ref[...]
    n = p.shape[0]
    m = p[0:1, :]
    for j in range(1, n):
        m = jnp.maximum(m, p[j:j + 1, :])
    ex = [jnp.exp(p[j:j + 1, :] - m) for j in range(n)]
    den = ex[0]
    for j in range(1, n):
        den = den + ex[j]
    acc = jnp.zeros_like(den)
    rows = []
    for j in range(n):
        acc = acc + ex[j] / den
        rows.append(acc)
    o_ref[...] = jnp.concatenate(rows, axis=0)


def _lower_bounds(p):
    return pl.pallas_call(
        _lower_bounds_kernel,
        out_shape=jax.ShapeDtypeStruct(p.shape, _f32),
        name="hgrn2_lower_bounds",
    )(p)


def kernel(x, c, w_ada, b_ada, w_in, hg_lower_bounds, hg_norm, ret_norm, w_out, pool_w, pool_b,
           pool_scale, w_router, b_router, w_gu, b_gu, w_down, b_down, final_norm):
    b, s, d = x.shape
    assert b == 1 and d == _D
    depth = w_ada.shape[0]
    assert depth == 2
    xs = x.reshape(s, d)
    mod = _ada(c, w_ada, b_ada)
    lb_all = _lower_bounds(hg_lower_bounds)
    x1, hn2, logits = _mixer0(xs, mod[0], w_in[0], lb_all[0], hg_norm[0], ret_norm[0], w_out[0],
                              w_router[0], b_router[0])
    xs = _moe(x1, hn2, logits, mod[0], final_norm, w_gu, b_gu, w_down, b_down, 0, False)
    x1, hn2, logits = _pool_mixer(xs, mod[1], pool_w[0], pool_b[0], pool_scale[0], w_router[1],
                                  b_router[1])
    xs = _moe(x1, hn2, logits, mod[1], final_norm, w_gu, b_gu, w_down, b_down, 1, True)
    return xs.reshape(b, s, d)
```
